```python
import math
import jax, jax.numpy as jnp
from jax import lax
import numpy as np

D_MODEL = 1024
BATCH = 8
SEQ = 2048
DEPTH = 2
DEC_BATCH = 32
DEC_SEQ = 4
PAST_LEN = 16384
PAGE_SIZE = 128

CONV_A_WIDTH = D_MODEL // 2
CONV_A_K = 3
ATT_HEADS = 8
ATT_HEAD_DIM = 64
ATT_WIDTH = ATT_HEADS * ATT_HEAD_DIM
MOBA_BLOCK = 256
MOBA_TOPK = 3
MOBA_QCHUNK = 16
IN0_WIDTH = 3 * CONV_A_WIDTH + 3 * ATT_WIDTH
MIX0_WIDTH = CONV_A_WIDTH + ATT_WIDTH

SSM_D_INNER = 2 * D_MODEL
SSM_HEAD_DIM = 64
SSM_HEADS = SSM_D_INNER // SSM_HEAD_DIM
SSM_GROUPS = 4
SSM_STATE = 128
SSM_CONV_K = 4
SSM_CHUNK = 128
SSM_CONV_DIM = SSM_D_INNER + 2 * SSM_GROUPS * SSM_STATE
SSM_IN_WIDTH = SSM_D_INNER + SSM_CONV_DIM + SSM_HEADS

FFN_HIDDEN = -(-8 * D_MODEL // (3 * 256)) * 256
NORM_EPS = 1e-6

kernel_name = 'hybrid_conv_moba_ssd_decoder_step'


def rmsnorm(x, w):
    xf = x.astype(jnp.float32)
    y = xf * lax.rsqrt(jnp.mean(xf * xf, axis=-1, keepdims=True) + NORM_EPS)
    return (y * w.astype(jnp.float32)).astype(x.dtype)


def ffn_sublayer(x, pre_w, w_gate, w_up, w_down, post_w):
    h = rmsnorm(x, pre_w)
    y = (jax.nn.silu(h @ w_gate) * (h @ w_up)) @ w_down
    return x + rmsnorm(y, post_w)


def causal_dwconv(u, hist, w):
    k_width = w.shape[0]
    t_len = u.shape[1]
    full = jnp.concatenate([hist.astype(u.dtype), u], axis=1)
    y = sum(w[j] * full[:, j:j + t_len] for j in range(k_width))
    return y, full[:, t_len:]


def moba_prompt(q, k, v):
    bn, s_len, n_h, dh = q.shape
    scale = dh ** -0.5
    nb = -(-s_len // MOBA_BLOCK)
    pad = nb * MOBA_BLOCK - s_len
    qh = q.transpose(0, 2, 1, 3)
    kh = jnp.pad(k.transpose(0, 2, 1, 3), ((0, 0), (0, 0), (0, pad), (0, 0)))
    vh = jnp.pad(v.transpose(0, 2, 1, 3), ((0, 0), (0, 0), (0, pad), (0, 0)))
    kb = kh.reshape(bn, n_h, nb, MOBA_BLOCK, dh)
    vb = vh.reshape(bn, n_h, nb, MOBA_BLOCK, dh)
    k_mean = jnp.mean(kb.astype(jnp.float32), axis=3)
    n_sel = min(MOBA_TOPK, nb - 1)
    bi = jnp.arange(bn)[:, None, None, None]
    hi = jnp.arange(n_h)[None, :, None, None]
    blk_ids = jnp.arange(nb)
    offs_q = jnp.arange(MOBA_QCHUNK)
    offs_k = jnp.arange(MOBA_BLOCK)

    def one_chunk(c):
        q0 = c * MOBA_QCHUNK
        own0 = (q0 // MOBA_BLOCK) * MOBA_BLOCK
        qc = lax.dynamic_slice_in_dim(qh, q0, MOBA_QCHUNK, axis=2)
        k_own = lax.dynamic_slice_in_dim(kh, own0, MOBA_BLOCK, axis=2)
        v_own = lax.dynamic_slice_in_dim(vh, own0, MOBA_BLOCK, axis=2)
        q_pos = q0 + offs_q
        k_pos = own0 + offs_k
        s_own = jnp.einsum('bhqd,bhkd->bhqk', qc, k_own, preferred_element_type=jnp.float32) * scale
        s_own = jnp.where(k_pos[None, :] <= q_pos[:, None], s_own, -jnp.inf)
        if n_sel == 0:
            p = jax.nn.softmax(s_own, axis=-1).astype(v.dtype)
            return jnp.einsum('bhqk,bhkd->bhqd', p, v_own)
        gate = jnp.einsum('bhqd,bhnd->bhqn', qc.astype(jnp.float32), k_mean)
        gate = jnp.where(blk_ids < q0 // MOBA_BLOCK, gate, -jnp.inf)
        g_top, sel = lax.top_k(gate, n_sel)
        k_sel = kb[bi, hi, sel].reshape(bn, n_h, MOBA_QCHUNK, n_sel * MOBA_BLOCK, dh)
        v_sel = vb[bi, hi, sel].reshape(bn, n_h, MOBA_QCHUNK, n_sel * MOBA_BLOCK, dh)
        s_sel = jnp.einsum('bhqd,bhqkd->bhqk', qc, k_sel, preferred_element_type=jnp.float32) * scale
        valid = jnp.repeat(jnp.isfinite(g_top), MOBA_BLOCK, axis=-1)
        s_sel = jnp.where(valid, s_sel, -jnp.inf)
        p = jax.nn.softmax(jnp.concatenate([s_own, s_sel], axis=-1), axis=-1).astype(v.dtype)
        return (jnp.einsum('bhqk,bhkd->bhqd', p[..., :MOBA_BLOCK], v_own)
                + jnp.einsum('bhqk,bhqkd->bhqd', p[..., MOBA_BLOCK:], v_sel))

    out = lax.map(one_chunk, jnp.arange(s_len // MOBA_QCHUNK))
    return out.transpose(1, 0, 3, 2, 4).reshape(bn, s_len, n_h * dh)


def moba_sample(q, k, v, cache_k, cache_v, page_table):
    bd, t_len, n_h, dh = q.shape
    scale = dh ** -0.5
    ppb = MOBA_BLOCK // PAGE_SIZE
    own0 = (PAST_LEN // MOBA_BLOCK) * MOBA_BLOCK
    n_cand = own0 // MOBA_BLOCK
    n_own_pages = (PAST_LEN - own0) // PAGE_SIZE
    qh = q.transpose(0, 2, 1, 3)
    kh = k.transpose(0, 2, 1, 3)
    vh = v.transpose(0, 2, 1, 3)
    own_pages = page_table[:, own0 // PAGE_SIZE: PAST_LEN // PAGE_SIZE]

    def own_rows(cache, new):
        rows = cache[own_pages].transpose(0, 2, 1, 3, 4).reshape(bd, n_h, n_own_pages * PAGE_SIZE, dh)
        return jnp.concatenate([rows.astype(new.dtype), new], axis=2)

    k_own = own_rows(cache_k, kh)
    v_own = own_rows(cache_v, vh)
    q_pos = PAST_LEN + jnp.arange(t_len)
    k_pos = own0 + jnp.arange(k_own.shape[2])
    s_own = jnp.einsum('bhqd,bhkd->bhqk', qh, k_own, preferred_element_type=jnp.float32) * scale
    s_own = jnp.where(k_pos[None, :] <= q_pos[:, None], s_own, -jnp.inf)
    if n_cand == 0:
        p = jax.nn.softmax(s_own, axis=-1).astype(v.dtype)
        out = jnp.einsum('bhqk,bhkd->bhqd', p, v_own)
        return out.transpose(0, 2, 1, 3).reshape(bd, t_len, n_h * dh)
    cand = cache_k[page_table[:, :n_cand * ppb]]
    k_mean = jnp.mean(cand.astype(jnp.float32).reshape(bd, n_cand, ppb, n_h, PAGE_SIZE, dh), axis=(2, 4))
    gate = jnp.einsum('bhqd,bnhd->bhqn', qh.astype(jnp.float32), k_mean)
    n_sel = min(MOBA_TOPK, n_cand)
    _, sel = lax.top_k(gate, n_sel)
    logical = sel[..., None] * ppb + jnp.arange(ppb)
    phys = page_table[jnp.arange(bd)[:, None, None, None, None], logical]
    hi = jnp.arange(n_h)[None, :, None, None, None]
    k_sel = cache_k[phys, hi].reshape(bd, n_h, t_len, n_sel * MOBA_BLOCK, dh).astype(kh.dtype)
    v_sel = cache_v[phys, hi].reshape(bd, n_h, t_len, n_sel * MOBA_BLOCK, dh).astype(vh.dtype)
    s_sel = jnp.einsum('bhqd,bhqkd->bhqk', qh, k_sel, preferred_element_type=jnp.float32) * scale
    n_own = k_own.shape[2]
    p = jax.nn.softmax(jnp.concatenate([s_own, s_sel], axis=-1), axis=-1).astype(v.dtype)
    out = (jnp.einsum('bhqk,bhkd->bhqd', p[..., :n_own], v_own)
           + jnp.einsum('bhqk,bhqkd->bhqd', p[..., n_own:], v_sel))
    return out.transpose(0, 2, 1, 3).reshape(bd, t_len, n_h * dh)


def even_mixer(h, conv_hist, w_in, conv_w, w_out, attend):
    bn, t_len, _ = h.shape
    split_at = [CONV_A_WIDTH, 2 * CONV_A_WIDTH, 3 * CONV_A_WIDTH,
                3 * CONV_A_WIDTH + ATT_WIDTH, 3 * CONV_A_WIDTH + 2 * ATT_WIDTH]
    gate_b, gate_c, u, q, k, v = jnp.split(h @ w_in, split_at, axis=-1)
    conv_out, new_hist = causal_dwconv(gate_c * u, conv_hist, conv_w)
    y_a = gate_b * conv_out
    heads = lambda t: t.reshape(bn, t_len, ATT_HEADS, ATT_HEAD_DIM)
    q, k, v = heads(q), heads(k), heads(v)
    y_b = attend(q, k, v)
    y = jnp.concatenate([y_a, y_b], axis=-1) @ w_out
    return y, new_hist, k, v


def ssd_scan(x, dt, a, b_in, c_in, init_state):
    bn, t_len, n_h, p_dim = x.shape
    n_g, n_s = b_in.shape[2], b_in.shape[3]
    r = n_h // n_g
    cl = SSM_CHUNK if t_len % SSM_CHUNK == 0 else t_len
    nc = t_len // cl
    f32 = jnp.float32
    xdt = (x.astype(f32) * dt[..., None]).reshape(bn, nc, cl, n_g, r, p_dim)
    a_cum = jnp.cumsum((dt * a).reshape(bn, nc, cl, n_g, r), axis=2)
    bc = b_in.astype(f32).reshape(bn, nc, cl, n_g, n_s)
    cc = c_in.astype(f32).reshape(bn, nc, cl, n_g, n_s)
    causal = jnp.tril(jnp.ones((cl, cl), bool))[None, None, :, :, None, None]
    seg = a_cum[:, :, :, None] - a_cum[:, :, None, :]
    decay = jnp.exp(jnp.where(causal, seg, -jnp.inf))
    cb = jnp.einsum('bclgn,bcsgn->bclsg', cc, bc)
    y_diag = jnp.einsum('bclsg,bclsgr,bcsgrp->bclgrp', cb, decay, xdt)
    decay_to_end = jnp.exp(a_cum[:, :, -1:] - a_cum)
    chunk_states = jnp.einsum('bclgn,bclgr,bclgrp->bcgrpn', bc, decay_to_end, xdt)
    chunk_decay = jnp.exp(a_cum[:, :, -1])

    def carry_step(s, inp):
        st, dec = inp
        return s * dec[..., None, None] + st, s

    s0 = init_state.astype(f32).reshape(bn, n_g, r, p_dim, n_s)
    s_final, s_enter = lax.scan(carry_step, s0,
                                (jnp.moveaxis(chunk_states, 1, 0), jnp.moveaxis(chunk_decay, 1, 0)))
    s_enter = jnp.moveaxis(s_enter, 0, 1)
    y_off = jnp.einsum('bclgn,bcgrpn,bclgr->bclgrp', cc, s_enter, jnp.exp(a_cum))
    y = (y_diag + y_off).reshape(bn, t_len, n_h, p_dim)
    return y, s_final.reshape(bn, n_h, p_dim, n_s).astype(init_state.dtype)


def mamba2_mixer(h, conv_hist, ssm_state, w_in, conv_w, conv_b, dt_bias, a_log, d_skip, norm_w, w_out):
    bn, t_len, _ = h.shape
    z, xbc, dt = jnp.split(h @ w_in, [SSM_D_INNER, SSM_D_INNER + SSM_CONV_DIM], axis=-1)
    xbc, new_conv_hist = causal_dwconv(xbc, conv_hist, conv_w)
    xbc = jax.nn.silu(xbc + conv_b)
    xs, b_in, c_in = jnp.split(xbc, [SSM_D_INNER, SSM_D_INNER + SSM_GROUPS * SSM_STATE], axis=-1)
    xs = xs.reshape(bn, t_len, SSM_HEADS, SSM_HEAD_DIM)
    b_in = b_in.reshape(bn, t_len, SSM_GROUPS, SSM_STATE)
    c_in = c_in.reshape(bn, t_len, SSM_GROUPS, SSM_STATE)
    dt = jax.nn.softplus(dt.astype(jnp.float32) + dt_bias.astype(jnp.float32))
    a = -jnp.exp(a_log.astype(jnp.float32))
    y, new_state = ssd_scan(xs, dt, a, b_in, c_in, ssm_state)
    y = y + d_skip.astype(jnp.float32)[:, None] * xs.astype(jnp.float32)
    y = y.astype(h.dtype).reshape(bn, t_len, SSM_D_INNER)
    y = rmsnorm(y * jax.nn.silu(z), norm_w)
    return y @ w_out, new_conv_hist, new_state


def setup_inputs(seed: int = 0) -> dict:
    key = jax.random.key(seed)
    keys = jax.random.split(key, 64)
    counter = [0]

    def nk():
        counter[0] += 1
        return keys[counter[0] - 1]

    f32 = jnp.float32
    dense = lambda fi, fo: jax.random.normal(nk(), (fi, fo), f32) * fi ** -0.5
    gain = lambda n: 1.0 + 0.05 * jax.random.normal(nk(), (n,), f32)
    n_pages = PAST_LEN // PAGE_SIZE
    n_pool = (5 * DEC_BATCH * n_pages + 3) // 4
    inp = {}
    inp['x_prompt'] = jax.random.normal(nk(), (BATCH, SEQ, D_MODEL), f32)
    inp['x_sample'] = jax.random.normal(nk(), (DEC_BATCH, DEC_SEQ, D_MODEL), f32)
    inp['cache_k'] = jax.random.normal(nk(), (n_pool, ATT_HEADS, PAGE_SIZE, ATT_HEAD_DIM), f32)
    inp['cache_v'] = jax.random.normal(nk(), (n_pool, ATT_HEADS, PAGE_SIZE, ATT_HEAD_DIM), f32)
    inp['page_table'] = jax.random.permutation(nk(), n_pool)[:DEC_BATCH * n_pages].reshape(DEC_BATCH, n_pages).astype(jnp.int32)
    inp['state_conv_a'] = jax.random.normal(nk(), (DEC_BATCH, CONV_A_K - 1, CONV_A_WIDTH), f32)
    inp['state_conv_ssm'] = jax.random.normal(nk(), (DEC_BATCH, SSM_CONV_K - 1, SSM_CONV_DIM), f32)
    inp['state_ssm'] = 0.1 * jax.random.normal(nk(), (DEC_BATCH, SSM_HEADS, SSM_HEAD_DIM, SSM_STATE), f32)
    inp['l0_norm_mix_pre'] = gain(D_MODEL)
    inp['l0_w_in'] = dense(D_MODEL, IN0_WIDTH)
    inp['l0_conv_w'] = jax.random.normal(nk(), (CONV_A_K, CONV_A_WIDTH), f32) * CONV_A_K ** -0.5
    inp['l0_w_out'] = dense(MIX0_WIDTH, D_MODEL)
    inp['l0_norm_mix_post'] = gain(D_MODEL)
    inp['l0_norm_ffn_pre'] = gain(D_MODEL)
    inp['l0_ffn_gate'] = dense(D_MODEL, FFN_HIDDEN)
    inp['l0_ffn_up'] = dense(D_MODEL, FFN_HIDDEN)
    inp['l0_ffn_down'] = dense(FFN_HIDDEN, D_MODEL)
    inp['l0_norm_ffn_post'] = gain(D_MODEL)
    inp['l1_norm_mix_pre'] = gain(D_MODEL)
    inp['l1_w_in'] = dense(D_MODEL, SSM_IN_WIDTH)
    inp['l1_conv_w'] = jax.random.normal(nk(), (SSM_CONV_K, SSM_CONV_DIM), f32) * SSM_CONV_K ** -0.5
    inp['l1_conv_b'] = 0.01 * jax.random.normal(nk(), (SSM_CONV_DIM,), f32)
    dt0 = jnp.exp(jax.random.uniform(nk(), (SSM_HEADS,), f32) * (math.log(0.1) - math.log(0.001)) + math.log(0.001))
    inp['l1_dt_bias'] = dt0 + jnp.log(-jnp.expm1(-dt0))
    inp['l1_a_log'] = jnp.log(jax.random.uniform(nk(), (SSM_HEADS,), f32, minval=1.0, maxval=16.0))
    inp['l1_d_skip'] = 1.0 + 0.1 * jax.random.normal(nk(), (SSM_HEADS,), f32)
    inp['l1_norm_gate'] = gain(SSM_D_INNER)
    inp['l1_w_out'] = dense(SSM_D_INNER, D_MODEL)
    inp['l1_norm_mix_post'] = gain(D_MODEL)
    inp['l1_norm_ffn_pre'] = gain(D_MODEL)
    inp['l1_ffn_gate'] = dense(D_MODEL, FFN_HIDDEN)
    inp['l1_ffn_up'] = dense(D_MODEL, FFN_HIDDEN)
    inp['l1_ffn_down'] = dense(FFN_HIDDEN, D_MODEL)
    inp['l1_norm_ffn_post'] = gain(D_MODEL)
    return inp


def reference(x_prompt, x_sample, cache_k, cache_v, page_table, state_conv_a, state_conv_ssm, state_ssm,
              l0_norm_mix_pre, l0_w_in, l0_conv_w, l0_w_out, l0_norm_mix_post,
              l0_norm_ffn_pre, l0_ffn_gate, l0_ffn_up, l0_ffn_down, l0_norm_ffn_post,
              l1_norm_mix_pre, l1_w_in, l1_conv_w, l1_conv_b, l1_dt_bias, l1_a_log, l1_d_skip,
              l1_norm_gate, l1_w_out, l1_norm_mix_post,
              l1_norm_ffn_pre, l1_ffn_gate, l1_ffn_up, l1_ffn_down, l1_norm_ffn_post):
    xp, xs = x_prompt, x_sample
    bp, s_len, _ = xp.shape
    attend_sample = lambda q, k, v: moba_sample(q, k, v, cache_k, cache_v, page_table)
    for layer in range(DEPTH):
        if layer % 2 == 0:
            hist0 = jnp.zeros((bp, CONV_A_K - 1, CONV_A_WIDTH), xp.dtype)
            yp, conv_a_prompt, kp, vp = even_mixer(rmsnorm(xp, l0_norm_mix_pre), hist0,
                                                   l0_w_in, l0_conv_w, l0_w_out, moba_prompt)
            ys, conv_a_sample, ks, vs = even_mixer(rmsnorm(xs, l0_norm_mix_pre), state_conv_a,
                                                   l0_w_in, l0_conv_w, l0_w_out, attend_sample)
            xp = xp + rmsnorm(yp, l0_norm_mix_post)
            xs = xs + rmsnorm(ys, l0_norm_mix_post)
            xp = ffn_sublayer(xp, l0_norm_ffn_pre, l0_ffn_gate, l0_ffn_up, l0_ffn_down, l0_norm_ffn_post)
            xs = ffn_sublayer(xs, l0_norm_ffn_pre, l0_ffn_gate, l0_ffn_up, l0_ffn_down, l0_norm_ffn_post)
            to_pages = lambda t: t.reshape(bp, s_len // PAGE_SIZE, PAGE_SIZE, ATT_HEADS, ATT_HEAD_DIM).transpose(0, 1, 3, 2, 4)
            k_prompt, v_prompt = to_pages(kp), to_pages(vp)
            k_sample, v_sample = ks.transpose(0, 2, 1, 3), vs.transpose(0, 2, 1, 3)
        else:
            conv0 = jnp.zeros((bp, SSM_CONV_K - 1, SSM_CONV_DIM), xp.dtype)
            ssm0 = jnp.zeros((bp, SSM_HEADS, SSM_HEAD_DIM, SSM_STATE), xp.dtype)
            yp, conv_ssm_prompt, ssm_prompt = mamba2_mixer(
                rmsnorm(xp, l1_norm_mix_pre), conv0, ssm0, l1_w_in, l1_conv_w, l1_conv_b,
                l1_dt_bias, l1_a_log, l1_d_skip, l1_norm_gate, l1_w_out)
            ys, conv_ssm_sample, ssm_sample = mamba2_mixer(
                rmsnorm(xs, l1_norm_mix_pre), state_conv_ssm, state_ssm, l1_w_in, l1_conv_w, l1_conv_b,
                l1_dt_bias, l1_a_log, l1_d_skip, l1_norm_gate, l1_w_out)
            xp = xp + rmsnorm(yp, l1_norm_mix_post)
            xs = xs + rmsnorm(ys, l1_norm_mix_post)
            xp = ffn_sublayer(xp, l1_norm_ffn_pre, l1_ffn_gate, l1_ffn_up, l1_ffn_down, l1_norm_ffn_post)
            xs = ffn_sublayer(xs, l1_norm_ffn_pre, l1_ffn_gate, l1_ffn_up, l1_ffn_down, l1_norm_ffn_post)
    return (xp, xs, k_prompt, v_prompt, k_sample, v_sample, conv_a_prompt, conv_a_sample,
            conv_ssm_prompt, conv_ssm_sample, ssm_prompt, ssm_sample)
```

```python
import functools

import jax
import jax.numpy as jnp
from jax import lax
from jax.experimental import pallas as pl
from jax.experimental.pallas import tpu as pltpu

F32 = jnp.float32
BF16 = jnp.bfloat16
HIGHEST = lax.Precision.HIGHEST

NORM_EPS = 1e-6
MOBA_BLOCK = 256
MOBA_TOPK = 3
SSM_CHUNK = 128
SSM_HEAD_DIM = 64
SSM_STATE = 128
SSM_GROUPS = 4
NEG = -1e30

VMEM_LIMIT_BYTES = 56 * 1024 * 1024
ROW_TILE = 512


def _params(*sem):
    return pltpu.CompilerParams(dimension_semantics=sem, vmem_limit_bytes=VMEM_LIMIT_BYTES)


def _rms(x, w):
    return x * lax.rsqrt(jnp.mean(x * x, axis=-1, keepdims=True) + NORM_EPS) * w


def _dot(a, b):
    return jnp.dot(a, b, preferred_element_type=F32)


def _dot_nt(a, b, precision=None):
    return lax.dot_general(a, b, (((1,), (1,)), ((), ())), preferred_element_type=F32,
                           precision=precision)


def _row_tile(m):
    return ROW_TILE if m % ROW_TILE == 0 else m


def _norm_matmul_kernel(x_ref, nw_ref, w_ref, o_ref, xn_ref):
    @pl.when(pl.program_id(1) == 0)
    def _():
        xn_ref[...] = _rms(x_ref[...], nw_ref[...]).astype(BF16)

    o_ref[...] = _dot(xn_ref[...], w_ref[...])


def norm_matmul(x, nw, w_bf16, tn=512):
    m, d = x.shape
    n = w_bf16.shape[1]
    tm = _row_tile(m)
    return pl.pallas_call(
        _norm_matmul_kernel,
        grid=(m // tm, n // tn),
        in_specs=[pl.BlockSpec((tm, d), lambda i, j: (i, 0)),
                  pl.BlockSpec((1, d), lambda i, j: (0, 0)),
                  pl.BlockSpec((d, tn), lambda i, j: (0, j))],
        out_specs=pl.BlockSpec((tm, tn), lambda i, j: (i, j)),
        out_shape=jax.ShapeDtypeStruct((m, n), F32),
        scratch_shapes=[pltpu.VMEM((tm, d), BF16)],
        compiler_params=_params("parallel", "arbitrary"),
        name="norm_matmul",
    )(x, nw.reshape(1, d), w_bf16)


def _norm_matmul3_kernel(x_ref, nw_ref, whi_ref, wlo_ref, o_ref):
    xn = _rms(x_ref[...], nw_ref[...])
    xh = xn.astype(BF16)
    xl = (xn - xh.astype(F32)).astype(BF16)
    o_ref[...] = _dot(xh, whi_ref[...]) + _dot(xl, whi_ref[...]) + _dot(xh, wlo_ref[...])


def norm_matmul3(x, nw, w_f32):
    m, d = x.shape
    n = w_f32.shape[1]
    tm = _row_tile(m)
    whi = w_f32.astype(BF16)
    wlo = (w_f32 - whi.astype(F32)).astype(BF16)
    return pl.pallas_call(
        _norm_matmul3_kernel,
        grid=(m // tm,),
        in_specs=[pl.BlockSpec((tm, d), lambda i: (i, 0)),
                  pl.BlockSpec((1, d), lambda i: (0, 0)),
                  pl.BlockSpec((d, n), lambda i: (0, 0)),
                  pl.BlockSpec((d, n), lambda i: (0, 0))],
        out_specs=pl.BlockSpec((tm, n), lambda i: (i, 0)),
        out_shape=jax.ShapeDtypeStruct((m, n), F32),
        compiler_params=_params("parallel"),
        name="norm_matmul3",
    )(x, nw.reshape(1, d), whi, wlo)


FFN_CHUNK = 256


def _ffn_kernel(x_ref, pre_ref, wg_ref, wu_ref, wd_ref, post_ref, o_ref):
    x = x_ref[...]
    h = _rms(x, pre_ref[...]).astype(BF16)
    hidden = wg_ref.shape[1]
    acc = jnp.zeros(x.shape, F32)
    for c in range(hidden // FFN_CHUNK):
        sl = slice(c * FFN_CHUNK, (c + 1) * FFN_CHUNK)
        g = _dot(h, wg_ref[:, sl])
        u = _dot(h, wu_ref[:, sl])
        a = (g * jax.nn.sigmoid(g) * u).astype(BF16)
        acc = acc + _dot(a, wd_ref[sl, :])
    o_ref[...] = x + _rms(acc, post_ref[...])


def ffn(x, pre_w, wg, wu, wd, post_w):
    m, d = x.shape
    hidden = wg.shape[1]
    tm = _row_tile(m)
    const = lambda i: (0, 0)
    return pl.pallas_call(
        _ffn_kernel,
        grid=(m // tm,),
        in_specs=[pl.BlockSpec((tm, d), lambda i: (i, 0)),
                  pl.BlockSpec((1, d), const),
                  pl.BlockSpec((d, hidden), const, pipeline_mode=pl.Buffered(1)),
                  pl.BlockSpec((d, hidden), const, pipeline_mode=pl.Buffered(1)),
                  pl.BlockSpec((hidden, d), const, pipeline_mode=pl.Buffered(1)),
                  pl.BlockSpec((1, d), const)],
        out_specs=pl.BlockSpec((tm, d), lambda i: (i, 0)),
        out_shape=jax.ShapeDtypeStruct((m, d), F32),
        compiler_params=_params("parallel"),
        name="ffn",
    )(x, pre_w.reshape(1, d), wg, wu, wd, post_w.reshape(1, d))


def _moba_prompt_kernel(q_ref, k_ref, v_ref, o_ref, kp_ref, vp_ref, vt_ref, km_ref, sel_ref):
    qi = pl.program_id(2)
    nb = k_ref.shape[1] // MOBA_BLOCK
    npages = kp_ref.shape[1]
    hd = kp_ref.shape[3]
    page = kp_ref.shape[4]
    ppb = MOBA_BLOCK // page

    @pl.when(qi == 0)
    def _():
        km_ref[...] = jnp.mean(k_ref[0].reshape(nb, MOBA_BLOCK, 2 * hd), axis=1)
        for pg in range(npages):
            rows = slice(pg * page, (pg + 1) * page)
            kt = k_ref[0, rows, :].T
            vt = v_ref[0, rows, :].T
            kp_ref[0, pg] = kt.reshape(2, hd, page)
            vp_ref[0, pg] = vt.reshape(2, hd, page)
            vt_ref[pg // ppb, :, (pg % ppb) * page:(pg % ppb + 1) * page] = vt.astype(BF16)

    row = lax.broadcasted_iota(jnp.int32, (MOBA_BLOCK, MOBA_BLOCK), 0)
    col = lax.broadcasted_iota(jnp.int32, (MOBA_BLOCK, MOBA_BLOCK), 1)
    causal = row <= col
    blk = lax.broadcasted_iota(jnp.int32, (nb, MOBA_BLOCK), 0)
    scale = hd ** -0.5

    outs = []
    for hh in range(2):
        lo = hh * hd
        q = q_ref[0][:, lo:lo + hd]
        gate = _dot_nt(km_ref[:, lo:lo + hd], q, precision=HIGHEST)
        cnt = jnp.zeros((nb, MOBA_BLOCK), F32)
        for jp in range(nb):
            gj = gate[jp:jp + 1, :]
            beats = (gj > gate) | ((gj == gate) & (jp < blk))
            cnt = cnt + jnp.where(beats, 1.0, 0.0) * (jp < qi).astype(F32)
        sel_ref[hh] = jnp.where((cnt < MOBA_TOPK) & (blk < qi), 1.0, 0.0)

        qs = (q * scale).astype(BF16)
        start = pl.multiple_of(qi * MOBA_BLOCK, MOBA_BLOCK)
        k_own = k_ref[0, pl.ds(start, MOBA_BLOCK), lo:lo + hd].astype(BF16)
        st = jnp.where(causal, _dot_nt(k_own, qs), NEG)
        m = jnp.max(st, axis=0, keepdims=True)
        p = jnp.exp(st - m)
        l = jnp.sum(p, axis=0, keepdims=True)
        acc = _dot(vt_ref[qi, lo:lo + hd, :], p.astype(BF16))

        def body(j, carry, lo=lo, qs=qs, hh=hh):
            m, l, acc = carry
            kj = k_ref[0, pl.ds(pl.multiple_of(j * MOBA_BLOCK, MOBA_BLOCK), MOBA_BLOCK),
                       lo:lo + hd].astype(BF16)
            picked = sel_ref[hh, pl.ds(j, 1), :] > 0.5
            st = jnp.where(picked, _dot_nt(kj, qs), NEG)
            m_new = jnp.maximum(m, jnp.max(st, axis=0, keepdims=True))
            alpha = jnp.exp(m - m_new)
            p = jnp.exp(st - m_new)
            l = alpha * l + jnp.sum(p, axis=0, keepdims=True)
            acc = alpha * acc + _dot(vt_ref[j, lo:lo + hd, :], p.astype(BF16))
            return m_new, l, acc

        m, l, acc = lax.fori_loop(0, qi, body, (m, l, acc))
        outs.append(acc / l)
    o_ref[0] = jnp.concatenate(outs, axis=0).T


def moba_prompt(h3, n_heads, hd, page):
    bsz, s_len, _ = h3.shape
    width = n_heads * hd
    lanes = 2 * hd
    pairs = n_heads // 2
    q0 = 3 * width // lanes
    nb = s_len // MOBA_BLOCK
    pages_shape = jax.ShapeDtypeStruct((bsz, s_len // page, n_heads, hd, page), F32)
    page_spec = pl.BlockSpec((1, s_len // page, 2, hd, page), lambda b, p, i: (b, 0, p, 0, 0))
    return pl.pallas_call(
        _moba_prompt_kernel,
        grid=(bsz, pairs, nb),
        in_specs=[pl.BlockSpec((1, MOBA_BLOCK, lanes), lambda b, p, i: (b, i, q0 + p)),
                  pl.BlockSpec((1, s_len, lanes), lambda b, p, i: (b, 0, q0 + pairs + p)),
                  pl.BlockSpec((1, s_len, lanes), lambda b, p, i: (b, 0, q0 + 2 * pairs + p))],
        out_specs=[pl.BlockSpec((1, MOBA_BLOCK, lanes), lambda b, p, i: (b, i, p)),
                   page_spec, page_spec],
        out_shape=[jax.ShapeDtypeStruct((bsz, s_len, width), F32), pages_shape, pages_shape],
        scratch_shapes=[pltpu.VMEM((nb, lanes, MOBA_BLOCK), BF16),
                        pltpu.VMEM((nb, lanes), F32),
                        pltpu.VMEM((2, nb, MOBA_BLOCK), F32)],
        compiler_params=_params("parallel", "parallel", "arbitrary"),
        name="moba_prompt",
    )(h3, h3, h3)


def _conv3_mix(gb, gcu, prev2, prev1, li, cw):
    s1 = jnp.where(li == 0, prev1[0], pltpu.roll(gcu, 1, 0))
    s2 = pltpu.roll(gcu, 2, 0)
    s2 = jnp.where(li == 0, prev2[0], jnp.where(li == 1, prev2[1], s2))
    conv = cw[0:1] * s2 + cw[1:2] * s1 + cw[2:3] * gcu
    return gb * conv


def _out_proj_residual(ya, yb, x, wo_ref, pw):
    half = ya.shape[1]
    y = _dot(ya.astype(BF16), wo_ref[0:half, :]) + _dot(yb.astype(BF16), wo_ref[half:, :])
    return x + _rms(y, pw)


def _l0_post_prompt_kernel(gb_ref, gc_ref, u_ref, gch_ref, uh_ref, yb_ref, x_ref, cw_ref, wo_ref,
                           pw_ref, xo_ref, st_ref, *, tiles_per_seq):
    i = pl.program_id(0)
    tm = gb_ref.shape[0]
    gcu = gc_ref[...] * u_ref[...]
    halo = gch_ref[...] * uh_ref[...]
    halo = halo * (i % tiles_per_seq != 0).astype(F32)
    li = lax.broadcasted_iota(jnp.int32, gcu.shape, 0)
    ya = _conv3_mix(gb_ref[...], gcu, (halo[6:7], halo[7:8]), (halo[7:8],), li, cw_ref[...])
    xo_ref[...] = _out_proj_residual(ya, yb_ref[...], x_ref[...], wo_ref, pw_ref[...])

    @pl.when(i % tiles_per_seq == tiles_per_seq - 1)
    def _():
        st_ref[0] = gcu[tm - 2:tm, :]


def l0_post_prompt(h2, yb, x, conv_w, wo, post_w, seq_len):
    m, d = x.shape
    width = yb.shape[1]
    tm = ROW_TILE
    tiles_per_seq = seq_len // tm
    halo_idx = lambda i: jnp.maximum(i * (tm // 8) - 1, 0)
    const = lambda i: (0, 0)
    return pl.pallas_call(
        functools.partial(_l0_post_prompt_kernel, tiles_per_seq=tiles_per_seq),
        grid=(m // tm,),
        in_specs=[pl.BlockSpec((tm, width), lambda i: (i, 0)),
                  pl.BlockSpec((tm, width), lambda i: (i, 1)),
                  pl.BlockSpec((tm, width), lambda i: (i, 2)),
                  pl.BlockSpec((8, width), lambda i: (halo_idx(i), 1)),
                  pl.BlockSpec((8, width), lambda i: (halo_idx(i), 2)),
                  pl.BlockSpec((tm, width), lambda i: (i, 0)),
                  pl.BlockSpec((tm, d), lambda i: (i, 0)),
                  pl.BlockSpec(conv_w.shape, const),
                  pl.BlockSpec(wo.shape, const),
                  pl.BlockSpec((1, d), const)],
        out_specs=[pl.BlockSpec((tm, d), lambda i: (i, 0)),
                   pl.BlockSpec((1, 2, width), lambda i: (i // tiles_per_seq, 0, 0))],
        out_shape=[jax.ShapeDtypeStruct((m, d), F32),
                   jax.ShapeDtypeStruct((m // seq_len, 2, width), F32)],
        compiler_params=_params("arbitrary"),
        name="l0_post_prompt",
    )(h2, h2, h2, h2, h2, yb, x, conv_w, wo, post_w.reshape(1, d))


def _l0_post_sample_kernel(gb_ref, gc_ref, u_ref, hist_ref, yb_ref, x_ref, cw_ref, wo_ref, pw_ref,
                           xo_ref, gcu_ref, *, seq_len):
    gcu = gc_ref[...] * u_ref[...]
    rows = gcu.shape[0]
    t = lax.broadcasted_iota(jnp.int32, gcu.shape, 0) % seq_len
    hist = hist_ref[...]
    hist_next = pltpu.roll(hist, rows - 1, 0)
    ya = _conv3_mix(gb_ref[...], gcu, (hist, hist), (hist_next,), t, cw_ref[...])
    xo_ref[...] = _out_proj_residual(ya, yb_ref[...], x_ref[...], wo_ref, pw_ref[...])
    gcu_ref[...] = gcu


def l0_post_sample(h2, yb, x, hist_rows, conv_w, wo, post_w, seq_len):
    m, d = x.shape
    width = yb.shape[1]
    const = lambda i: (0, 0)
    return pl.pallas_call(
        functools.partial(_l0_post_sample_kernel, seq_len=seq_len),
        grid=(1,),
        in_specs=[pl.BlockSpec((m, width), lambda i: (0, 0)),
                  pl.BlockSpec((m, width), lambda i: (0, 1)),
                  pl.BlockSpec((m, width), lambda i: (0, 2)),
                  pl.BlockSpec((m, width), const),
                  pl.BlockSpec((m, width), const),
                  pl.BlockSpec((m, d), const),
                  pl.BlockSpec(conv_w.shape, const),
                  pl.BlockSpec(wo.shape, const),
                  pl.BlockSpec((1, d), const)],
        out_specs=[pl.BlockSpec((m, d), const), pl.BlockSpec((m, width), const)],
        out_shape=[jax.ShapeDtypeStruct((m, d), F32), jax.ShapeDtypeStruct((m, width), F32)],
        compiler_params=_params("arbitrary"),
        name="l0_post_sample",
    )(h2, h2, h2, hist_rows, yb, x, conv_w, wo, post_w.reshape(1, d))


MEAN_BUFFERS = 8


def _moba_select_kernel(pt_ref, qt_ref, ck_ref, sel_ref, buf, sem, km_ref, *, pages_per_block,
                        n_blk):
    bd, n_h, hd, t_len = qt_ref.shape
    page = ck_ref.shape[3]
    n_pages = n_blk * pages_per_block
    total = bd * n_pages

    def copy(i):
        slot = i % MEAN_BUFFERS
        return pltpu.make_async_copy(ck_ref.at[pt_ref[i]], buf.at[slot], sem.at[slot])

    for s in range(MEAN_BUFFERS - 1):
        copy(s).start()

    lane3 = lax.broadcasted_iota(jnp.int32, (n_h, hd, 128), 2)
    lane = lax.broadcasted_iota(jnp.int32, (n_h, 128), 1)
    sub = lax.broadcasted_iota(jnp.int32, (n_h, 128), 0)

    def per_seq(b, _):
        km_ref[...] = jnp.zeros(km_ref.shape, F32)

        def per_block(n, _):
            acc = jnp.zeros((n_h, hd, page), F32)
            for jj in range(pages_per_block):
                i = b * n_pages + n * pages_per_block + jj
                nxt = i + MEAN_BUFFERS - 1

                @pl.when(nxt < total)
                def _():
                    copy(nxt).start()

                copy(i).wait()
                acc = acc + buf[i % MEAN_BUFFERS]
            mean = jnp.sum(acc, axis=-1, keepdims=True) * (1.0 / (pages_per_block * page))
            km_ref[...] = jnp.where(lane3 == n, mean, km_ref[...])
            return 0

        lax.fori_loop(0, n_blk, per_block, 0)
        out = jnp.zeros((n_h, 128), jnp.int32)
        for t in range(t_len):
            gate = jnp.full((n_h, 128), -jnp.inf, F32)
            for h in range(n_h):
                g_h = jnp.sum(km_ref[h] * qt_ref[b, h][:, t:t + 1], axis=0, keepdims=True)
                gate = jnp.where(sub == h, g_h, gate)
            gate = jnp.where(lane < n_blk, gate, -jnp.inf)
            for r in range(MOBA_TOPK):
                best = jnp.max(gate, axis=-1, keepdims=True)
                idx = jnp.min(jnp.where(gate == best, lane, 128), axis=-1, keepdims=True)
                out = jnp.where(lane == t * MOBA_TOPK + r, idx, out)
                gate = jnp.where(lane == idx, -jnp.inf, gate)
        sel_ref[b] = out
        return 0

    lax.fori_loop(0, bd, per_seq, 0)


def moba_select(page_table, qt4, cache_kt):
    bd, n_pages = page_table.shape
    _, n_h, hd, t_len = qt4.shape
    page = cache_kt.shape[3]
    ppb = MOBA_BLOCK // page
    n_blk = n_pages // ppb
    assert n_blk <= 128 and t_len * MOBA_TOPK <= 128
    grid_spec = pltpu.PrefetchScalarGridSpec(
        num_scalar_prefetch=1,
        grid=(1,),
        in_specs=[pl.BlockSpec(qt4.shape, lambda i, pt: (0, 0, 0, 0)),
                  pl.BlockSpec(memory_space=pl.ANY)],
        out_specs=pl.BlockSpec((bd, n_h, 128), lambda i, pt: (0, 0, 0)),
        scratch_shapes=[pltpu.VMEM((MEAN_BUFFERS, n_h, hd, page), F32),
                        pltpu.SemaphoreType.DMA((MEAN_BUFFERS,)),
                        pltpu.VMEM((n_h, hd, 128), F32)],
    )
    return pl.pallas_call(
        functools.partial(_moba_select_kernel, pages_per_block=ppb, n_blk=n_blk),
        grid_spec=grid_spec,
        out_shape=jax.ShapeDtypeStruct((bd, n_h, 128), jnp.int32),
        compiler_params=_params("arbitrary"),
        name="moba_select",
    )(page_table.reshape(-1), qt4, cache_kt)


def _moba_attend_kernel(pt_ref, sel_ref, qt_ref, knt_ref, vnt_ref, ck_ref, cv_ref, o_ref,
                        kbuf, vbuf, sem, *, pages_per_block, n_pages):
    bd, n_h, hd, t_len = qt_ref.shape
    page = ck_ref.shape[3]
    per_t = MOBA_TOPK * pages_per_block
    scale = hd ** -0.5
    step = lax.broadcasted_iota(jnp.int32, (1, t_len), 1)
    step_col = lax.broadcasted_iota(jnp.int32, (hd, t_len), 1)

    def copies(pair, slot):
        b = pair // n_h
        h = pair % n_h
        out = []
        for t in range(t_len):
            for r in range(MOBA_TOPK):
                blk = sel_ref[(b * n_h + h) * (t_len * MOBA_TOPK) + t * MOBA_TOPK + r]
                for jj in range(pages_per_block):
                    phys = pt_ref[b * n_pages + blk * pages_per_block + jj]
                    s = t * per_t + r * pages_per_block + jj
                    out.append(pltpu.make_async_copy(ck_ref.at[phys, h], kbuf.at[slot, s],
                                                     sem.at[slot]))
                    out.append(pltpu.make_async_copy(cv_ref.at[phys, h], vbuf.at[slot, s],
                                                     sem.at[slot]))
        return out

    for c in copies(0, 0):
        c.start()

    def per_pair(pair, _):
        slot = pair % 2
        b = pair // n_h
        h = pair % n_h

        @pl.when(pair + 1 < bd * n_h)
        def _():
            for c in copies(pair + 1, 1 - slot):
                c.start()

        for c in copies(pair, slot):
            c.wait()

        q_all = qt_ref[b, h] * scale
        k_new = knt_ref[b, h]
        v_new = vnt_ref[b, h]
        out = jnp.zeros((hd, t_len), F32)
        for t in range(t_len):
            qc = q_all[:, t:t + 1]
            s_sel = [jnp.sum(kbuf[slot, t * per_t + s] * qc, axis=0, keepdims=True)
                     for s in range(per_t)]
            s_own = jnp.where(step <= t, jnp.sum(k_new * qc, axis=0, keepdims=True), NEG)
            m = jnp.max(s_own, axis=-1, keepdims=True)
            for s in s_sel:
                m = jnp.maximum(m, jnp.max(s, axis=-1, keepdims=True))
            p_own = jnp.exp(s_own - m)
            l = jnp.sum(p_own, axis=-1, keepdims=True)
            acc_own = jnp.sum(v_new * p_own, axis=-1, keepdims=True)
            acc = jnp.zeros((hd, page), F32)
            for si, s in enumerate(s_sel):
                p = jnp.exp(s - m)
                l = l + jnp.sum(p, axis=-1, keepdims=True)
                acc = acc + vbuf[slot, t * per_t + si] * p
            o_t = (acc_own + jnp.sum(acc, axis=-1, keepdims=True)) / l
            out = jnp.where(step_col == t, o_t, out)
        o_ref[b, h] = out
        return 0

    lax.fori_loop(0, bd * n_h, per_pair, 0)


def moba_attend(page_table, sel, qt4, knt4, vnt4, cache_kt, cache_vt):
    bd, n_pages = page_table.shape
    _, n_h, hd, t_len = qt4.shape
    page = cache_kt.shape[3]
    ppb = MOBA_BLOCK // page
    n_slab = t_len * MOBA_TOPK * ppb
    full = pl.BlockSpec(qt4.shape, lambda i, pt, sl: (0, 0, 0, 0))
    grid_spec = pltpu.PrefetchScalarGridSpec(
        num_scalar_prefetch=2,
        grid=(1,),
        in_specs=[full, full, full, pl.BlockSpec(memory_space=pl.ANY),
                  pl.BlockSpec(memory_space=pl.ANY)],
        out_specs=full,
        scratch_shapes=[pltpu.VMEM((2, n_slab, hd, page), F32),
                        pltpu.VMEM((2, n_slab, hd, page), F32),
                        pltpu.SemaphoreType.DMA((2,))],
    )
    return pl.pallas_call(
        functools.partial(_moba_attend_kernel, pages_per_block=ppb, n_pages=n_pages),
        grid_spec=grid_spec,
        out_shape=jax.ShapeDtypeStruct(qt4.shape, F32),
        compiler_params=_params("arbitrary"),
        name="moba_attend",
    )(page_table.reshape(-1), sel.reshape(-1), qt4, knt4, vnt4, cache_kt, cache_vt)


def _softplus(x):
    return jnp.maximum(x, 0.0) + jnp.log1p(jnp.exp(-jnp.abs(x)))


def _ssd_kernel(*refs, d_inner, rows_valid, has_halo):
    if has_halo:
        (zx_ref, halo_ref, dt_ref, s0_ref, cw_ref, cb_ref, dtb_ref, alog_ref, dskip_ref, e_ref,
         y_ref, cs_ref, fs_ref, st_ref, xin_ref, dtin_ref, y_scr) = refs
        hist_ref = None
    else:
        (zx_ref, hist_ref, dt_ref, s0_ref, cw_ref, cb_ref, dtb_ref, alog_ref, dskip_ref, e_ref,
         y_ref, cs_ref, fs_ref, st_ref, xin_ref, dtin_ref, y_scr) = refs
        halo_ref = None
    c = pl.program_id(1)
    n_chunks = pl.num_programs(1)
    cl = SSM_CHUNK
    rows_blk = zx_ref.shape[1]
    conv_dim = cw_ref.shape[1]
    n_st = SSM_STATE
    hp = SSM_HEAD_DIM
    heads_per_group = d_inner // hp // SSM_GROUPS
    gw = heads_per_group * hp

    @pl.when(c == 0)
    def _():
        st_ref[...] = s0_ref[0].T

    if rows_blk == cl:
        xbc = zx_ref[0, :, d_inner:d_inner + conv_dim]
        dt_raw = dt_ref[0]
    else:
        xin_ref[...] = jnp.zeros(xin_ref.shape, F32)
        dtin_ref[...] = jnp.zeros(dtin_ref.shape, F32)
        xin_ref[0:rows_blk, :] = zx_ref[0, :, d_inner:d_inner + conv_dim]
        dtin_ref[0:rows_blk, :] = dt_ref[0]
        xbc = xin_ref[...]
        dt_raw = dtin_ref[...]

    if has_halo:
        prev = halo_ref[0, :, d_inner:d_inner + conv_dim] * (c != 0).astype(F32)
        p5, p6, p7 = prev[5:6], prev[6:7], prev[7:8]
    else:
        hist = hist_ref[0]
        p5, p6, p7 = hist[0:1], hist[1:2], hist[2:3]
    li = lax.broadcasted_iota(jnp.int32, xbc.shape, 0)
    s1 = jnp.where(li == 0, p7, pltpu.roll(xbc, 1, 0))
    s2 = jnp.where(li == 0, p6, jnp.where(li == 1, p7, pltpu.roll(xbc, 2, 0)))
    s3 = jnp.where(li == 0, p5, jnp.where(li == 1, p6, jnp.where(li == 2, p7,
                                                                  pltpu.roll(xbc, 3, 0))))
    cw = cw_ref[...]
    conv = cw[0:1] * s3 + cw[1:2] * s2 + cw[2:3] * s1 + cw[3:4] * xbc + cb_ref[...]
    act = conv * jax.nn.sigmoid(conv)

    @pl.when(c == n_chunks - 1)
    def _():
        cs_ref[0] = xbc[rows_valid - 3:rows_valid, :]

    dt = _softplus(dt_raw + dtb_ref[...])
    if rows_valid < cl:
        dt = jnp.where(lax.broadcasted_iota(jnp.int32, dt.shape, 0) < rows_valid, dt, 0.0)
    a = -jnp.exp(alog_ref[...])
    dta = dt * a
    r_i = lax.broadcasted_iota(jnp.int32, (cl, cl), 0)
    c_i = lax.broadcasted_iota(jnp.int32, (cl, cl), 1)
    causal = c_i <= r_i
    tril = jnp.where(causal, 1.0, 0.0)
    acum = jnp.dot(tril, dta, preferred_element_type=F32, precision=HIGHEST)
    acum_t = acum.T
    alast = acum[cl - 1:cl, :]
    expand = e_ref[...]

    def widen(v):
        return jnp.dot(v, expand, preferred_element_type=F32, precision=HIGHEST)

    dt_w = widen(dt)
    eac_w = widen(jnp.exp(acum))
    dte_w = widen(jnp.exp(alast - acum))
    cd_w = widen(jnp.broadcast_to(jnp.exp(alast), (8, alast.shape[1])))[0:1]

    xs = act[:, 0:d_inner]
    xdt = xs * dt_w
    xdt_b = xdt.astype(BF16)
    xdtd_b = (xdt * dte_w).astype(BF16)
    for g in range(SSM_GROUPS):
        b_g = act[:, d_inner + g * n_st:d_inner + (g + 1) * n_st]
        c_g = act[:, d_inner + (SSM_GROUPS + g) * n_st:d_inner + (SSM_GROUPS + g + 1) * n_st]
        c_gb = c_g.astype(BF16)
        cb = _dot_nt(c_gb, b_g.astype(BF16))
        s_g = st_ref[:, g * gw:(g + 1) * gw]
        y_off = _dot(c_gb, s_g.astype(BF16)) * eac_w[:, g * gw:(g + 1) * gw]
        for r in range(heads_per_group):
            h = g * heads_per_group + r
            seg = acum[:, h:h + 1] - acum_t[h:h + 1, :]
            decay = jnp.exp(jnp.where(causal, seg, NEG))
            y_d = _dot((cb * decay).astype(BF16), xdt_b[:, h * hp:(h + 1) * hp])
            y_scr[:, h * hp:(h + 1) * hp] = y_d + y_off[:, r * hp:(r + 1) * hp]
        st_ref[:, g * gw:(g + 1) * gw] = (s_g * cd_w[:, g * gw:(g + 1) * gw]
                                          + _dot(b_g.T.astype(BF16),
                                                 xdtd_b[:, g * gw:(g + 1) * gw]))
    y = y_scr[...] + dskip_ref[...] * xs
    y_ref[0] = y[0:rows_blk, :]

    @pl.when(c == n_chunks - 1)
    def _():
        fs_ref[0] = st_ref[...].T


def ssd_mixer(zx3, dt3, hist, state0, conv_w, conv_b, dt_bias, a_log, d_skip, d_inner):
    nseq, t_len, _ = zx3.shape
    conv_dim = conv_w.shape[1]
    n_heads = d_inner // SSM_HEAD_DIM
    cl = SSM_CHUNK
    has_halo = hist is None
    if has_halo:
        rows_blk, rows_valid, n_chunks = cl, cl, t_len // cl
    else:
        rows_blk, rows_valid, n_chunks = t_len, t_len, 1
    pad = lambda v: jnp.pad(v.astype(F32), (0, 128 - n_heads)).reshape(1, 128)
    expand = (jnp.arange(128)[:, None] == (jnp.arange(d_inner)[None, :] // SSM_HEAD_DIM)).astype(F32)
    d_wide = jnp.repeat(d_skip.astype(F32), SSM_HEAD_DIM).reshape(1, d_inner)
    const2 = lambda s, c: (0, 0)
    if has_halo:
        second = pl.BlockSpec((1, 8, zx3.shape[2]),
                              lambda s, c: (s, jnp.maximum(c * (cl // 8) - 1, 0), 0))
        second_arg = zx3
    else:
        second = pl.BlockSpec((1,) + hist.shape[1:], lambda s, c: (s, 0, 0))
        second_arg = hist
    outs = pl.pallas_call(
        functools.partial(_ssd_kernel, d_inner=d_inner, rows_valid=rows_valid, has_halo=has_halo),
        grid=(nseq, n_chunks),
        in_specs=[pl.BlockSpec((1, rows_blk, zx3.shape[2]), lambda s, c: (s, c, 0)),
                  second,
                  pl.BlockSpec((1, rows_blk, 128), lambda s, c: (s, c, 0)),
                  pl.BlockSpec((1, d_inner, SSM_STATE), lambda s, c: (s, 0, 0)),
                  pl.BlockSpec(conv_w.shape, const2),
                  pl.BlockSpec((1, conv_dim), const2),
                  pl.BlockSpec((1, 128), const2),
                  pl.BlockSpec((1, 128), const2),
                  pl.BlockSpec((1, d_inner), const2),
                  pl.BlockSpec((128, d_inner), const2)],
        out_specs=[pl.BlockSpec((1, rows_blk, d_inner), lambda s, c: (s, c, 0)),
                   pl.BlockSpec((1, 3, conv_dim), lambda s, c: (s, 0, 0)),
                   pl.BlockSpec((1, d_inner, SSM_STATE), lambda s, c: (s, 0, 0))],
        out_shape=[jax.ShapeDtypeStruct((nseq, t_len, d_inner), F32),
                   jax.ShapeDtypeStruct((nseq, 3, conv_dim), F32),
                   jax.ShapeDtypeStruct((nseq, d_inner, SSM_STATE), F32)],
        scratch_shapes=[pltpu.VMEM((SSM_STATE, d_inner), F32),
                        pltpu.VMEM((cl, conv_dim), F32),
                        pltpu.VMEM((cl, 128), F32),
                        pltpu.VMEM((cl, d_inner), F32)],
        compiler_params=_params("parallel", "arbitrary"),
        name="ssd_mixer",
    )(zx3, second_arg, dt3, state0, conv_w, conv_b.reshape(1, conv_dim), pad(dt_bias), pad(a_log),
      d_wide, expand)
    return outs


def _l1_post_kernel(y_ref, z_ref, x_ref, gw_ref, wo_ref, pw_ref, o_ref):
    z = z_ref[...]
    g = y_ref[...] * (z * jax.nn.sigmoid(z))
    o = _dot(_rms(g, gw_ref[...]).astype(BF16), wo_ref[...])
    o_ref[...] = x_ref[...] + _rms(o, pw_ref[...])


def l1_post(y, zx, x, gate_w, wo, post_w):
    m, d = x.shape
    d_inner = y.shape[1]
    tm = _row_tile(m)
    const = lambda i: (0, 0)
    return pl.pallas_call(
        _l1_post_kernel,
        grid=(m // tm,),
        in_specs=[pl.BlockSpec((tm, d_inner), lambda i: (i, 0)),
                  pl.BlockSpec((tm, d_inner), lambda i: (i, 0)),
                  pl.BlockSpec((tm, d), lambda i: (i, 0)),
                  pl.BlockSpec((1, d_inner), const),
                  pl.BlockSpec(wo.shape, const),
                  pl.BlockSpec((1, d), const)],
        out_specs=pl.BlockSpec((tm, d), lambda i: (i, 0)),
        out_shape=jax.ShapeDtypeStruct((m, d), F32),
        compiler_params=_params("parallel"),
        name="l1_post",
    )(y, zx, x, gate_w.reshape(1, d_inner), wo, post_w.reshape(1, d))


def kernel(x_prompt, x_sample, cache_k, cache_v, page_table, state_conv_a, state_conv_ssm, state_ssm, l0_norm_mix_pre, l0_w_in, l0_conv_w, l0_w_out, l0_norm_mix_post, l0_norm_ffn_pre, l0_ffn_gate, l0_ffn_up, l0_ffn_down, l0_norm_ffn_post, l1_norm_mix_pre, l1_w_in, l1_conv_w, l1_conv_b, l1_dt_bias, l1_a_log, l1_d_skip, l1_norm_gate, l1_w_out, l1_norm_mix_post, l1_norm_ffn_pre, l1_ffn_gate, l1_ffn_up, l1_ffn_down, l1_norm_ffn_post):
    bp, s_len, d = x_prompt.shape
    bd, t_len, _ = x_sample.shape
    n_heads, page, hd = cache_k.shape[1:]
    width = n_heads * hd
    assert (page_table.shape[1] * page) % MOBA_BLOCK == 0
    conv_dim = l1_conv_w.shape[1]
    ssm_heads = l1_dt_bias.shape[0]
    d_inner = ssm_heads * SSM_HEAD_DIM

    bf = lambda w: w.astype(BF16)
    xp = x_prompt.reshape(bp * s_len, d)
    xs = x_sample.reshape(bd * t_len, d)

    w_in0 = bf(l0_w_in)
    w_out0 = bf(l0_w_out)
    hp_ = norm_matmul(xp, l0_norm_mix_pre, w_in0)
    hs_ = norm_matmul(xs, l0_norm_mix_pre, w_in0)

    yb_p, kp_t, vp_t = moba_prompt(hp_.reshape(bp, s_len, -1), n_heads, hd, page)
    k_prompt = jnp.swapaxes(kp_t, 3, 4)
    v_prompt = jnp.swapaxes(vp_t, 3, 4)
    xp, conv_a_prompt = l0_post_prompt(hp_, yb_p.reshape(bp * s_len, width), xp, l0_conv_w, w_out0,
                                       l0_norm_mix_post, s_len)

    qkv_t = hs_[:, 3 * width:].reshape(bd, t_len, 3, n_heads, hd).transpose(2, 0, 3, 4, 1)
    qt4, knt4, vnt4 = qkv_t[0], qkv_t[1], qkv_t[2]
    cache_kt = jnp.swapaxes(cache_k, 2, 3)
    cache_vt = jnp.swapaxes(cache_v, 2, 3)
    sel = moba_select(page_table, qt4, cache_kt)[:, :, :t_len * MOBA_TOPK]
    yb_s = moba_attend(page_table, sel, qt4, knt4, vnt4, cache_kt, cache_vt)
    yb_s = yb_s.transpose(0, 3, 1, 2).reshape(bd * t_len, width)
    hist_rows = jnp.pad(state_conv_a, ((0, 0), (0, t_len - state_conv_a.shape[1]), (0, 0)))
    xs, gcu_s = l0_post_sample(hs_, yb_s, xs, hist_rows.reshape(bd * t_len, width), l0_conv_w,
                               w_out0, l0_norm_mix_post, t_len)
    conv_a_sample = gcu_s.reshape(bd, t_len, width)[:, t_len - 2:]
    k_sample = knt4.transpose(0, 1, 3, 2)
    v_sample = vnt4.transpose(0, 1, 3, 2)

    wg0, wu0, wd0 = bf(l0_ffn_gate), bf(l0_ffn_up), bf(l0_ffn_down)
    xp = ffn(xp, l0_norm_ffn_pre, wg0, wu0, wd0, l0_norm_ffn_post)
    xs = ffn(xs, l0_norm_ffn_pre, wg0, wu0, wd0, l0_norm_ffn_post)

    w_zx = bf(l1_w_in[:, :d_inner + conv_dim])
    w_dt = jnp.pad(l1_w_in[:, d_inner + conv_dim:], ((0, 0), (0, 128 - ssm_heads)))
    w_out1 = bf(l1_w_out)
    zx_p = norm_matmul(xp, l1_norm_mix_pre, w_zx)
    zx_s = norm_matmul(xs, l1_norm_mix_pre, w_zx)
    dt_p = norm_matmul3(xp, l1_norm_mix_pre, w_dt)
    dt_s = norm_matmul3(xs, l1_norm_mix_pre, w_dt)

    zeros_state = jnp.zeros((bp, d_inner, SSM_STATE), F32)
    y_p, conv_ssm_prompt, fs_p = ssd_mixer(zx_p.reshape(bp, s_len, -1), dt_p.reshape(bp, s_len, 128),
                                           None, zeros_state, l1_conv_w, l1_conv_b, l1_dt_bias,
                                           l1_a_log, l1_d_skip, d_inner)
    y_s, conv_ssm_sample, fs_s = ssd_mixer(zx_s.reshape(bd, t_len, -1), dt_s.reshape(bd, t_len, 128),
                                           state_conv_ssm, state_ssm.reshape(bd, d_inner, SSM_STATE),
                                           l1_conv_w, l1_conv_b, l1_dt_bias, l1_a_log, l1_d_skip,
                                           d_inner)
    ssm_prompt = fs_p.reshape(bp, ssm_heads, SSM_HEAD_DIM, SSM_STATE)
    ssm_sample = fs_s.reshape(bd, ssm_heads, SSM_HEAD_DIM, SSM_STATE)

    xp = l1_post(y_p.reshape(bp * s_len, d_inner), zx_p, xp, l1_norm_gate, w_out1, l1_norm_mix_post)
    xs = l1_post(y_s.reshape(bd * t_len, d_inner), zx_s, xs, l1_norm_gate, w_out1, l1_norm_mix_post)

    wg1, wu1, wd1 = bf(l1_ffn_gate), bf(l1_ffn_up), bf(l1_ffn_down)
    xp = ffn(xp, l1_norm_ffn_pre, wg1, wu1, wd1, l1_norm_ffn_post)
    xs = ffn(xs, l1_norm_ffn_pre, wg1, wu1, wd1, l1_norm_ffn_post)

    return (xp.reshape(bp, s_len, d), xs.reshape(bd, t_len, d), k_prompt, v_prompt, k_sample,
            v_sample, conv_a_prompt, conv_a_sample, conv_ssm_prompt, conv_ssm_sample, ssm_prompt,
            ssm_sample)
```

```python
import functools

import jax
import jax.numpy as jnp
from jax import lax
from jax.experimental import pallas as pl
from jax.experimental.pallas import tpu as pltpu

F32 = jnp.float32
BF16 = jnp.bfloat16
HIGHEST = lax.Precision.HIGHEST

NORM_EPS = 1e-6
MOBA_BLOCK = 256
MOBA_TOPK = 3
SSM_CHUNK = 128
SSM_HEAD_DIM = 64
SSM_STATE = 128
SSM_GROUPS = 4
NEG = -1e30

VMEM_LIMIT_BYTES = 56 * 1024 * 1024
ROW_TILE = 512


def _params(*sem):
    return pltpu.CompilerParams(dimension_semantics=sem, vmem_limit_bytes=VMEM_LIMIT_BYTES)


def _rms(x, w):
    return x * lax.rsqrt(jnp.mean(x * x, axis=-1, keepdims=True) + NORM_EPS) * w


def _dot(a, b):
    return jnp.dot(a, b, preferred_element_type=F32)


def _dot_nt(a, b, precision=None):
    return lax.dot_general(a, b, (((1,), (1,)), ((), ())), preferred_element_type=F32,
                           precision=precision)


def _row_tile(m):
    return ROW_TILE if m % ROW_TILE == 0 else m


PROJ_ROW_TILE = 256
PROJ_COL_CHUNK = 1024


def _norm_matmul_kernel(*refs, narrow):
    if narrow:
        x_ref, nw_ref, w_ref, whi_ref, wlo_ref, o_ref, o2_ref = refs
    else:
        x_ref, nw_ref, w_ref, o_ref = refs
    xn = _rms(x_ref[...], nw_ref[...])
    xh = xn.astype(BF16)
    n = w_ref.shape[1]
    for c0 in range(0, n, PROJ_COL_CHUNK):
        c1 = min(c0 + PROJ_COL_CHUNK, n)
        o_ref[:, c0:c1] = _dot(xh, w_ref[:, c0:c1])
    if narrow:
        xl = (xn - xh.astype(F32)).astype(BF16)
        o2_ref[...] = _dot(xh, whi_ref[...]) + _dot(xl, whi_ref[...]) + _dot(xh, wlo_ref[...])


def norm_matmul(x, nw, w_bf16, w_narrow=None):
    m, d = x.shape
    n = w_bf16.shape[1]
    tm = PROJ_ROW_TILE if m % PROJ_ROW_TILE == 0 else m
    const = lambda i: (0, 0)
    resident = lambda shape: pl.BlockSpec(shape, const, pipeline_mode=pl.Buffered(1))
    in_specs = [pl.BlockSpec((tm, d), lambda i: (i, 0)), pl.BlockSpec((1, d), const),
                resident((d, n))]
    out_specs = [pl.BlockSpec((tm, n), lambda i: (i, 0))]
    out_shape = [jax.ShapeDtypeStruct((m, n), F32)]
    args = [x, nw.reshape(1, d), w_bf16]
    if w_narrow is not None:
        n2 = w_narrow.shape[1]
        whi = w_narrow.astype(BF16)
        wlo = (w_narrow - whi.astype(F32)).astype(BF16)
        in_specs += [resident((d, n2)), resident((d, n2))]
        out_specs.append(pl.BlockSpec((tm, n2), lambda i: (i, 0)))
        out_shape.append(jax.ShapeDtypeStruct((m, n2), F32))
        args += [whi, wlo]
    outs = pl.pallas_call(
        functools.partial(_norm_matmul_kernel, narrow=w_narrow is not None),
        grid=(m // tm,),
        in_specs=in_specs,
        out_specs=out_specs,
        out_shape=out_shape,
        compiler_params=_params("parallel"),
        name="norm_matmul",
    )(*args)
    return outs if w_narrow is not None else outs[0]


FFN_CHUNK = 256


def _ffn_kernel(x_ref, pre_ref, wg_ref, wu_ref, wd_ref, post_ref, o_ref):
    x = x_ref[...]
    h = _rms(x, pre_ref[...]).astype(BF16)
    hidden = wg_ref.shape[1]
    acc = jnp.zeros(x.shape, F32)
    for c in range(hidden // FFN_CHUNK):
        sl = slice(c * FFN_CHUNK, (c + 1) * FFN_CHUNK)
        g = _dot(h, wg_ref[:, sl])
        u = _dot(h, wu_ref[:, sl])
        a = (g * jax.nn.sigmoid(g) * u).astype(BF16)
        acc = acc + _dot(a, wd_ref[sl, :])
    o_ref[...] = x + _rms(acc, post_ref[...])


def ffn(x, pre_w, wg, wu, wd, post_w):
    m, d = x.shape
    hidden = wg.shape[1]
    tm = _row_tile(m)
    const = lambda i: (0, 0)
    return pl.pallas_call(
        _ffn_kernel,
        grid=(m // tm,),
        in_specs=[pl.BlockSpec((tm, d), lambda i: (i, 0)),
                  pl.BlockSpec((1, d), const),
                  pl.BlockSpec((d, hidden), const, pipeline_mode=pl.Buffered(1)),
                  pl.BlockSpec((d, hidden), const, pipeline_mode=pl.Buffered(1)),
                  pl.BlockSpec((hidden, d), const, pipeline_mode=pl.Buffered(1)),
                  pl.BlockSpec((1, d), const)],
        out_specs=pl.BlockSpec((tm, d), lambda i: (i, 0)),
        out_shape=jax.ShapeDtypeStruct((m, d), F32),
        compiler_params=_params("parallel"),
        name="ffn",
    )(x, pre_w.reshape(1, d), wg, wu, wd, post_w.reshape(1, d))


def _moba_prompt_kernel(q_ref, k_ref, v_ref, o_ref, kp_ref, vp_ref, vt_ref, kb_ref, km_ref,
                        sel_ref):
    qi = pl.program_id(2)
    nb = k_ref.shape[1] // MOBA_BLOCK
    npages = kp_ref.shape[1]
    hd = kp_ref.shape[3]
    page = kp_ref.shape[4]
    ppb = MOBA_BLOCK // page

    @pl.when(qi == 0)
    def _():
        kb_ref[...] = k_ref[0].astype(BF16)
        km_ref[...] = jnp.mean(k_ref[0].reshape(nb, MOBA_BLOCK, 2 * hd), axis=1)
        for pg in range(npages):
            rows = slice(pg * page, (pg + 1) * page)
            kt = k_ref[0, rows, :].T
            vt = v_ref[0, rows, :].T
            kp_ref[0, pg] = kt.reshape(2, hd, page)
            vp_ref[0, pg] = vt.reshape(2, hd, page)
            vt_ref[pg // ppb, :, (pg % ppb) * page:(pg % ppb + 1) * page] = vt.astype(BF16)

    row = lax.broadcasted_iota(jnp.int32, (MOBA_BLOCK, MOBA_BLOCK), 0)
    col_row = lax.broadcasted_iota(jnp.int32, (1, MOBA_BLOCK), 1)
    blk = lax.broadcasted_iota(jnp.int32, (nb, MOBA_BLOCK), 0)
    scale = hd ** -0.5

    qs = []
    for hh in range(2):
        lo = hh * hd
        q = q_ref[0][:, lo:lo + hd]
        gate = _dot_nt(km_ref[:, lo:lo + hd], q, precision=HIGHEST)
        cnt = jnp.zeros((nb, MOBA_BLOCK), F32)
        for jp in range(nb):
            gj = gate[jp:jp + 1, :]
            beats = (gj > gate) | ((gj == gate) & (jp < blk))
            cnt = cnt + jnp.where(beats, 1.0, 0.0) * (jp < qi).astype(F32)
        picked = ((cnt < MOBA_TOPK) & (blk < qi)) | (blk == qi)
        sel_ref[hh] = jnp.where(picked, 0, -4 * MOBA_BLOCK)
        qs.append((q * scale).astype(BF16))

    def scores(j, hh):
        jc = jnp.minimum(j, qi)
        kj = kb_ref[pl.ds(pl.multiple_of(jc * MOBA_BLOCK, MOBA_BLOCK), MOBA_BLOCK),
                    hh * hd:(hh + 1) * hd]
        shift = jnp.where(j < qi, MOBA_BLOCK, jnp.where(j == qi, 0, -4 * MOBA_BLOCK))
        limit = col_row + (1 + shift) + sel_ref[hh, pl.ds(jc, 1), :]
        return jc, jnp.where(row < limit, _dot_nt(kj, qs[hh]), -jnp.inf)

    def body(i, carry):
        new = []
        for hh in range(2):
            m, l, acc = carry[hh]
            ja, sa = scores(2 * i, hh)
            jb, sb = scores(2 * i + 1, hh)
            m_new = jnp.maximum(m, jnp.max(jnp.maximum(sa, sb), axis=0, keepdims=True))
            alpha = jnp.exp(m - m_new)
            pa = jnp.exp(sa - m_new)
            pb = jnp.exp(sb - m_new)
            l = alpha * l + jnp.sum(pa + pb, axis=0, keepdims=True)
            acc = (alpha * acc + _dot(vt_ref[ja, hh * hd:(hh + 1) * hd, :], pa.astype(BF16))
                   + _dot(vt_ref[jb, hh * hd:(hh + 1) * hd, :], pb.astype(BF16)))
            new.append((m_new, l, acc))
        return tuple(new)

    init = (jnp.full((1, MOBA_BLOCK), NEG, F32), jnp.zeros((1, MOBA_BLOCK), F32),
            jnp.zeros((hd, MOBA_BLOCK), F32))
    res = lax.fori_loop(0, qi // 2 + 1, body, (init, init))
    o_ref[0] = jnp.concatenate([acc / l for (_, l, acc) in res], axis=0).T


def moba_prompt(h3, n_heads, hd, page):
    bsz, s_len, _ = h3.shape
    width = n_heads * hd
    lanes = 2 * hd
    pairs = n_heads // 2
    q0 = 3 * width // lanes
    nb = s_len // MOBA_BLOCK
    pages_shape = jax.ShapeDtypeStruct((bsz, s_len // page, n_heads, hd, page), F32)
    page_spec = pl.BlockSpec((1, s_len // page, 2, hd, page), lambda b, p, i: (b, 0, p, 0, 0))
    return pl.pallas_call(
        _moba_prompt_kernel,
        grid=(bsz, pairs, nb),
        in_specs=[pl.BlockSpec((1, MOBA_BLOCK, lanes), lambda b, p, i: (b, i, q0 + p)),
                  pl.BlockSpec((1, s_len, lanes), lambda b, p, i: (b, 0, q0 + pairs + p)),
                  pl.BlockSpec((1, s_len, lanes), lambda b, p, i: (b, 0, q0 + 2 * pairs + p))],
        out_specs=[pl.BlockSpec((1, MOBA_BLOCK, lanes), lambda b, p, i: (b, i, p)),
                   page_spec, page_spec],
        out_shape=[jax.ShapeDtypeStruct((bsz, s_len, width), F32), pages_shape, pages_shape],
        scratch_shapes=[pltpu.VMEM((nb, lanes, MOBA_BLOCK), BF16),
                        pltpu.VMEM((s_len, lanes), BF16),
                        pltpu.VMEM((nb, lanes), F32),
                        pltpu.VMEM((2, nb, MOBA_BLOCK), jnp.int32)],
        compiler_params=_params("parallel", "parallel", "arbitrary"),
        name="moba_prompt",
    )(h3, h3, h3)


def _conv3_mix(gb, gcu, prev2, prev1, li, cw):
    s1 = jnp.where(li == 0, prev1[0], pltpu.roll(gcu, 1, 0))
    s2 = pltpu.roll(gcu, 2, 0)
    s2 = jnp.where(li == 0, prev2[0], jnp.where(li == 1, prev2[1], s2))
    conv = cw[0:1] * s2 + cw[1:2] * s1 + cw[2:3] * gcu
    return gb * conv


def _out_proj_residual(ya, yb, x, wo_ref, pw):
    half = ya.shape[1]
    y = _dot(ya.astype(BF16), wo_ref[0:half, :]) + _dot(yb.astype(BF16), wo_ref[half:, :])
    return x + _rms(y, pw)


def _l0_post_prompt_kernel(gb_ref, gc_ref, u_ref, gch_ref, uh_ref, yb_ref, x_ref, cw_ref, wo_ref,
                           pw_ref, xo_ref, st_ref, *, tiles_per_seq):
    i = pl.program_id(0)
    tm = gb_ref.shape[0]
    gcu = gc_ref[...] * u_ref[...]
    halo = gch_ref[...] * uh_ref[...]
    halo = halo * (i % tiles_per_seq != 0).astype(F32)
    li = lax.broadcasted_iota(jnp.int32, gcu.shape, 0)
    ya = _conv3_mix(gb_ref[...], gcu, (halo[6:7], halo[7:8]), (halo[7:8],), li, cw_ref[...])
    xo_ref[...] = _out_proj_residual(ya, yb_ref[...], x_ref[...], wo_ref, pw_ref[...])

    @pl.when(i % tiles_per_seq == tiles_per_seq - 1)
    def _():
        st_ref[0] = gcu[tm - 2:tm, :]


def l0_post_prompt(h2, yb, x, conv_w, wo, post_w, seq_len):
    m, d = x.shape
    width = yb.shape[1]
    tm = ROW_TILE
    tiles_per_seq = seq_len // tm
    halo_idx = lambda i: jnp.maximum(i * (tm // 8) - 1, 0)
    const = lambda i: (0, 0)
    return pl.pallas_call(
        functools.partial(_l0_post_prompt_kernel, tiles_per_seq=tiles_per_seq),
        grid=(m // tm,),
        in_specs=[pl.BlockSpec((tm, width), lambda i: (i, 0)),
                  pl.BlockSpec((tm, width), lambda i: (i, 1)),
                  pl.BlockSpec((tm, width), lambda i: (i, 2)),
                  pl.BlockSpec((8, width), lambda i: (halo_idx(i), 1)),
                  pl.BlockSpec((8, width), lambda i: (halo_idx(i), 2)),
                  pl.BlockSpec((tm, width), lambda i: (i, 0)),
                  pl.BlockSpec((tm, d), lambda i: (i, 0)),
                  pl.BlockSpec(conv_w.shape, const),
                  pl.BlockSpec(wo.shape, const),
                  pl.BlockSpec((1, d), const)],
        out_specs=[pl.BlockSpec((tm, d), lambda i: (i, 0)),
                   pl.BlockSpec((1, 2, width), lambda i: (i // tiles_per_seq, 0, 0))],
        out_shape=[jax.ShapeDtypeStruct((m, d), F32),
                   jax.ShapeDtypeStruct((m // seq_len, 2, width), F32)],
        compiler_params=_params("arbitrary"),
        name="l0_post_prompt",
    )(h2, h2, h2, h2, h2, yb, x, conv_w, wo, post_w.reshape(1, d))


def _l0_post_sample_kernel(gb_ref, gc_ref, u_ref, hist_ref, yb_ref, x_ref, cw_ref, wo_ref, pw_ref,
                           xo_ref, gcu_ref, *, seq_len):
    gcu = gc_ref[...] * u_ref[...]
    rows = gcu.shape[0]
    t = lax.broadcasted_iota(jnp.int32, gcu.shape, 0) % seq_len
    hist = hist_ref[...]
    hist_next = pltpu.roll(hist, rows - 1, 0)
    ya = _conv3_mix(gb_ref[...], gcu, (hist, hist), (hist_next,), t, cw_ref[...])
    xo_ref[...] = _out_proj_residual(ya, yb_ref[...], x_ref[...], wo_ref, pw_ref[...])
    gcu_ref[...] = gcu


def l0_post_sample(h2, yb, x, hist_rows, conv_w, wo, post_w, seq_len):
    m, d = x.shape
    width = yb.shape[1]
    const = lambda i: (0, 0)
    return pl.pallas_call(
        functools.partial(_l0_post_sample_kernel, seq_len=seq_len),
        grid=(1,),
        in_specs=[pl.BlockSpec((m, width), lambda i: (0, 0)),
                  pl.BlockSpec((m, width), lambda i: (0, 1)),
                  pl.BlockSpec((m, width), lambda i: (0, 2)),
                  pl.BlockSpec((m, width), const),
                  pl.BlockSpec((m, width), const),
                  pl.BlockSpec((m, d), const),
                  pl.BlockSpec(conv_w.shape, const),
                  pl.BlockSpec(wo.shape, const),
                  pl.BlockSpec((1, d), const)],
        out_specs=[pl.BlockSpec((m, d), const), pl.BlockSpec((m, width), const)],
        out_shape=[jax.ShapeDtypeStruct((m, d), F32), jax.ShapeDtypeStruct((m, width), F32)],
        compiler_params=_params("arbitrary"),
        name="l0_post_sample",
    )(h2, h2, h2, hist_rows, yb, x, conv_w, wo, post_w.reshape(1, d))


MEAN_BUFFERS = 16


def _moba_select_kernel(pt_ref, qt_ref, ck_ref, sel_ref, buf, sem, km_ref, *, pages_per_block,
                        n_blk):
    bd, n_h, hd, t_len = qt_ref.shape
    page = ck_ref.shape[3]
    n_pages = n_blk * pages_per_block
    total = bd * n_pages

    def copy(i):
        slot = i % MEAN_BUFFERS
        return pltpu.make_async_copy(ck_ref.at[pt_ref[i]], buf.at[slot], sem.at[slot])

    for s in range(MEAN_BUFFERS - 1):
        copy(s).start()

    lane3 = lax.broadcasted_iota(jnp.int32, (n_h, hd, 128), 2)
    lane = lax.broadcasted_iota(jnp.int32, (n_h, 128), 1)
    sub = lax.broadcasted_iota(jnp.int32, (n_h, 128), 0)

    def per_seq(b, _):
        km_ref[...] = jnp.zeros(km_ref.shape, F32)

        def per_block(n, _):
            acc = jnp.zeros((n_h, hd, page), F32)
            for jj in range(pages_per_block):
                i = b * n_pages + n * pages_per_block + jj
                nxt = i + MEAN_BUFFERS - 1

                @pl.when(nxt < total)
                def _():
                    copy(nxt).start()

                copy(i).wait()
                acc = acc + buf[i % MEAN_BUFFERS]
            mean = jnp.sum(acc, axis=-1, keepdims=True) * (1.0 / (pages_per_block * page))
            km_ref[...] = jnp.where(lane3 == n, mean, km_ref[...])
            return 0

        lax.fori_loop(0, n_blk, per_block, 0, unroll=2)
        out = jnp.zeros((n_h, 128), jnp.int32)
        for t in range(t_len):
            gate = jnp.full((n_h, 128), -jnp.inf, F32)
            for h in range(n_h):
                g_h = jnp.sum(km_ref[h] * qt_ref[b, h][:, t:t + 1], axis=0, keepdims=True)
                gate = jnp.where(sub == h, g_h, gate)
            gate = jnp.where(lane < n_blk, gate, -jnp.inf)
            for r in range(MOBA_TOPK):
                best = jnp.max(gate, axis=-1, keepdims=True)
                idx = jnp.min(jnp.where(gate == best, lane, 128), axis=-1, keepdims=True)
                out = jnp.where(lane == t * MOBA_TOPK + r, idx, out)
                gate = jnp.where(lane == idx, -jnp.inf, gate)
        sel_ref[b] = out
        return 0

    lax.fori_loop(0, bd, per_seq, 0)


def moba_select(page_table, qt4, cache_kt):
    bd, n_pages = page_table.shape
    _, n_h, hd, t_len = qt4.shape
    page = cache_kt.shape[3]
    ppb = MOBA_BLOCK // page
    n_blk = n_pages // ppb
    assert n_blk <= 128 and t_len * MOBA_TOPK <= 128
    grid_spec = pltpu.PrefetchScalarGridSpec(
        num_scalar_prefetch=1,
        grid=(1,),
        in_specs=[pl.BlockSpec(qt4.shape, lambda i, pt: (0, 0, 0, 0)),
                  pl.BlockSpec(memory_space=pl.ANY)],
        out_specs=pl.BlockSpec((bd, n_h, 128), lambda i, pt: (0, 0, 0)),
        scratch_shapes=[pltpu.VMEM((MEAN_BUFFERS, n_h, hd, page), F32),
                        pltpu.SemaphoreType.DMA((MEAN_BUFFERS,)),
                        pltpu.VMEM((n_h, hd, 128), F32)],
    )
    return pl.pallas_call(
        functools.partial(_moba_select_kernel, pages_per_block=ppb, n_blk=n_blk),
        grid_spec=grid_spec,
        out_shape=jax.ShapeDtypeStruct((bd, n_h, 128), jnp.int32),
        compiler_params=_params("arbitrary"),
        name="moba_select",
    )(page_table.reshape(-1), qt4, cache_kt)


def _moba_attend_kernel(pt_ref, sel_ref, qt_ref, knt_ref, vnt_ref, ck_ref, cv_ref, o_ref,
                        kbuf, vbuf, sem, *, pages_per_block, n_pages):
    bd, n_h, hd, t_len = qt_ref.shape
    page = ck_ref.shape[3]
    per_t = MOBA_TOPK * pages_per_block
    scale = hd ** -0.5
    step = lax.broadcasted_iota(jnp.int32, (1, t_len), 1)
    step_col = lax.broadcasted_iota(jnp.int32, (hd, t_len), 1)

    def copies(pair, slot):
        b = pair // n_h
        h = pair % n_h
        out = []
        for t in range(t_len):
            for r in range(MOBA_TOPK):
                blk = sel_ref[(b * n_h + h) * (t_len * MOBA_TOPK) + t * MOBA_TOPK + r]
                for jj in range(pages_per_block):
                    phys = pt_ref[b * n_pages + blk * pages_per_block + jj]
                    s = t * per_t + r * pages_per_block + jj
                    out.append(pltpu.make_async_copy(ck_ref.at[phys, h], kbuf.at[slot, s],
                                                     sem.at[slot]))
                    out.append(pltpu.make_async_copy(cv_ref.at[phys, h], vbuf.at[slot, s],
                                                     sem.at[slot]))
        return out

    for c in copies(0, 0):
        c.start()

    def per_pair(pair, _):
        slot = pair % 2
        b = pair // n_h
        h = pair % n_h

        @pl.when(pair + 1 < bd * n_h)
        def _():
            for c in copies(pair + 1, 1 - slot):
                c.start()

        for c in copies(pair, slot):
            c.wait()

        q_all = qt_ref[b, h] * scale
        k_new = knt_ref[b, h]
        v_new = vnt_ref[b, h]
        out = jnp.zeros((hd, t_len), F32)
        for t in range(t_len):
            qc = q_all[:, t:t + 1]
            s_sel = [jnp.sum(kbuf[slot, t * per_t + s] * qc, axis=0, keepdims=True)
                     for s in range(per_t)]
            s_own = jnp.where(step <= t, jnp.sum(k_new * qc, axis=0, keepdims=True), NEG)
            m = jnp.max(s_own, axis=-1, keepdims=True)
            for s in s_sel:
                m = jnp.maximum(m, jnp.max(s, axis=-1, keepdims=True))
            p_own = jnp.exp(s_own - m)
            l = jnp.sum(p_own, axis=-1, keepdims=True)
            acc_own = jnp.sum(v_new * p_own, axis=-1, keepdims=True)
            acc = jnp.zeros((hd, page), F32)
            for si, s in enumerate(s_sel):
                p = jnp.exp(s - m)
                l = l + jnp.sum(p, axis=-1, keepdims=True)
                acc = acc + vbuf[slot, t * per_t + si] * p
            o_t = (acc_own + jnp.sum(acc, axis=-1, keepdims=True)) / l
            out = jnp.where(step_col == t, o_t, out)
        o_ref[b, h] = out
        return 0

    lax.fori_loop(0, bd * n_h, per_pair, 0)


def moba_attend(page_table, sel, qt4, knt4, vnt4, cache_kt, cache_vt):
    bd, n_pages = page_table.shape
    _, n_h, hd, t_len = qt4.shape
    page = cache_kt.shape[3]
    ppb = MOBA_BLOCK // page
    n_slab = t_len * MOBA_TOPK * ppb
    full = pl.BlockSpec(qt4.shape, lambda i, pt, sl: (0, 0, 0, 0))
    grid_spec = pltpu.PrefetchScalarGridSpec(
        num_scalar_prefetch=2,
        grid=(1,),
        in_specs=[full, full, full, pl.BlockSpec(memory_space=pl.ANY),
                  pl.BlockSpec(memory_space=pl.ANY)],
        out_specs=full,
        scratch_shapes=[pltpu.VMEM((2, n_slab, hd, page), F32),
                        pltpu.VMEM((2, n_slab, hd, page), F32),
                        pltpu.SemaphoreType.DMA((2,))],
    )
    return pl.pallas_call(
        functools.partial(_moba_attend_kernel, pages_per_block=ppb, n_pages=n_pages),
        grid_spec=grid_spec,
        out_shape=jax.ShapeDtypeStruct(qt4.shape, F32),
        compiler_params=_params("arbitrary"),
        name="moba_attend",
    )(page_table.reshape(-1), sel.reshape(-1), qt4, knt4, vnt4, cache_kt, cache_vt)


def _softplus(x):
    return jnp.maximum(x, 0.0) + jnp.log1p(jnp.exp(-jnp.abs(x)))


def _ssd_kernel(*refs, d_inner, rows_valid, has_halo):
    if has_halo:
        (zx_ref, halo_ref, dt_ref, s0_ref, cw_ref, cb_ref, dtb_ref, alog_ref, dskip_ref, e_ref,
         y_ref, cs_ref, fs_ref, st_ref, xin_ref, dtin_ref, y_scr) = refs
        hist_ref = None
    else:
        (zx_ref, hist_ref, dt_ref, s0_ref, cw_ref, cb_ref, dtb_ref, alog_ref, dskip_ref, e_ref,
         y_ref, cs_ref, fs_ref, st_ref, xin_ref, dtin_ref, y_scr) = refs
        halo_ref = None
    c = pl.program_id(1)
    n_chunks = pl.num_programs(1)
    cl = SSM_CHUNK
    rows_blk = zx_ref.shape[1]
    conv_dim = cw_ref.shape[1]
    n_st = SSM_STATE
    hp = SSM_HEAD_DIM
    heads_per_group = d_inner // hp // SSM_GROUPS
    gw = heads_per_group * hp

    @pl.when(c == 0)
    def _():
        st_ref[...] = s0_ref[0].T

    if has_halo:
        xin_ref[0:8, :] = halo_ref[0, :, d_inner:d_inner + conv_dim] * (c != 0).astype(F32)
    else:
        xin_ref[0:8, :] = jnp.zeros((8, conv_dim), F32)
        xin_ref[5:8, :] = hist_ref[0]
    if rows_blk == cl:
        xin_ref[8:8 + cl, :] = zx_ref[0, :, d_inner:d_inner + conv_dim]
        dt_raw = dt_ref[0]
    else:
        xin_ref[8:8 + cl, :] = jnp.zeros((cl, conv_dim), F32)
        dtin_ref[...] = jnp.zeros(dtin_ref.shape, F32)
        xin_ref[8:8 + rows_blk, :] = zx_ref[0, :, d_inner:d_inner + conv_dim]
        dtin_ref[0:rows_blk, :] = dt_ref[0]
        dt_raw = dtin_ref[...]

    cw = cw_ref[...]
    xbc = xin_ref[8:8 + cl, :]
    conv = (cw[0:1] * xin_ref[5:5 + cl, :] + cw[1:2] * xin_ref[6:6 + cl, :]
            + cw[2:3] * xin_ref[7:7 + cl, :] + cw[3:4] * xbc + cb_ref[...])
    act = conv * jax.nn.sigmoid(conv)

    @pl.when(c == n_chunks - 1)
    def _():
        cs_ref[0] = xbc[rows_valid - 3:rows_valid, :]

    dt = _softplus(dt_raw + dtb_ref[...])
    if rows_valid < cl:
        dt = jnp.where(lax.broadcasted_iota(jnp.int32, dt.shape, 0) < rows_valid, dt, 0.0)
    a = -jnp.exp(alog_ref[...])
    dta = dt * a
    r_i = lax.broadcasted_iota(jnp.int32, (cl, cl), 0)
    c_i = lax.broadcasted_iota(jnp.int32, (cl, cl), 1)
    causal = c_i <= r_i
    tril = jnp.where(causal, 1.0, 0.0)
    acum = jnp.dot(tril, dta, preferred_element_type=F32, precision=HIGHEST)
    acum_t = acum.T
    alast = acum[cl - 1:cl, :]
    narrow = jnp.concatenate([dt, jnp.exp(acum), jnp.exp(alast - acum),
                              jnp.broadcast_to(jnp.exp(alast), (8, alast.shape[1]))], axis=0)
    hi = narrow.astype(BF16)
    mid = (narrow - hi.astype(F32)).astype(BF16)
    wide = _dot(jnp.concatenate([hi, mid], axis=1), e_ref[...])
    dt_w = wide[0:cl]
    eac_w = wide[cl:2 * cl]
    dte_w = wide[2 * cl:3 * cl]
    cd_w = wide[3 * cl:3 * cl + 1]

    xs = act[:, 0:d_inner]
    xdt = xs * dt_w
    xdt_b = xdt.astype(BF16)
    xdtd_b = (xdt * dte_w).astype(BF16)
    for g in range(SSM_GROUPS):
        b_g = act[:, d_inner + g * n_st:d_inner + (g + 1) * n_st]
        c_g = act[:, d_inner + (SSM_GROUPS + g) * n_st:d_inner + (SSM_GROUPS + g + 1) * n_st]
        c_gb = c_g.astype(BF16)
        cb = _dot_nt(c_gb, b_g.astype(BF16))
        s_g = st_ref[:, g * gw:(g + 1) * gw]
        y_off = _dot(c_gb, s_g.astype(BF16)) * eac_w[:, g * gw:(g + 1) * gw]
        for r in range(heads_per_group):
            h = g * heads_per_group + r
            seg = acum[:, h:h + 1] - acum_t[h:h + 1, :]
            decay = jnp.exp(jnp.where(causal, seg, NEG))
            y_d = _dot((cb * decay).astype(BF16), xdt_b[:, h * hp:(h + 1) * hp])
            y_scr[:, h * hp:(h + 1) * hp] = y_d + y_off[:, r * hp:(r + 1) * hp]
        st_ref[:, g * gw:(g + 1) * gw] = (s_g * cd_w[:, g * gw:(g + 1) * gw]
                                          + _dot(b_g.T.astype(BF16),
                                                 xdtd_b[:, g * gw:(g + 1) * gw]))
    y = y_scr[...] + dskip_ref[...] * xs
    y_ref[0] = y[0:rows_blk, :]

    @pl.when(c == n_chunks - 1)
    def _():
        fs_ref[0] = st_ref[...].T


def ssd_mixer(zx3, dt3, hist, state0, conv_w, conv_b, dt_bias, a_log, d_skip, d_inner):
    nseq, t_len, _ = zx3.shape
    conv_dim = conv_w.shape[1]
    n_heads = d_inner // SSM_HEAD_DIM
    cl = SSM_CHUNK
    has_halo = hist is None
    if has_halo:
        rows_blk, rows_valid, n_chunks = cl, cl, t_len // cl
    else:
        rows_blk, rows_valid, n_chunks = t_len, t_len, 1
    pad = lambda v: jnp.pad(v.astype(F32), (0, 128 - n_heads)).reshape(1, 128)
    expand = (jnp.arange(256)[:, None] % 128
              == (jnp.arange(d_inner)[None, :] // SSM_HEAD_DIM)).astype(BF16)
    d_wide = jnp.repeat(d_skip.astype(F32), SSM_HEAD_DIM).reshape(1, d_inner)
    const2 = lambda s, c: (0, 0)
    if has_halo:
        second = pl.BlockSpec((1, 8, zx3.shape[2]),
                              lambda s, c: (s, jnp.maximum(c * (cl // 8) - 1, 0), 0))
        second_arg = zx3
    else:
        second = pl.BlockSpec((1,) + hist.shape[1:], lambda s, c: (s, 0, 0))
        second_arg = hist
    outs = pl.pallas_call(
        functools.partial(_ssd_kernel, d_inner=d_inner, rows_valid=rows_valid, has_halo=has_halo),
        grid=(nseq, n_chunks),
        in_specs=[pl.BlockSpec((1, rows_blk, zx3.shape[2]), lambda s, c: (s, c, 0)),
                  second,
                  pl.BlockSpec((1, rows_blk, 128), lambda s, c: (s, c, 0)),
                  pl.BlockSpec((1, d_inner, SSM_STATE), lambda s, c: (s, 0, 0)),
                  pl.BlockSpec(conv_w.shape, const2),
                  pl.BlockSpec((1, conv_dim), const2),
                  pl.BlockSpec((1, 128), const2),
                  pl.BlockSpec((1, 128), const2),
                  pl.BlockSpec((1, d_inner), const2),
                  pl.BlockSpec((256, d_inner), const2)],
        out_specs=[pl.BlockSpec((1, rows_blk, d_inner), lambda s, c: (s, c, 0)),
                   pl.BlockSpec((1, 3, conv_dim), lambda s, c: (s, 0, 0)),
                   pl.BlockSpec((1, d_inner, SSM_STATE), lambda s, c: (s, 0, 0))],
        out_shape=[jax.ShapeDtypeStruct((nseq, t_len, d_inner), F32),
                   jax.ShapeDtypeStruct((nseq, 3, conv_dim), F32),
                   jax.ShapeDtypeStruct((nseq, d_inner, SSM_STATE), F32)],
        scratch_shapes=[pltpu.VMEM((SSM_STATE, d_inner), F32),
                        pltpu.VMEM((cl + 8, conv_dim), F32),
                        pltpu.VMEM((cl, 128), F32),
                        pltpu.VMEM((cl, d_inner), F32)],
        compiler_params=_params("parallel", "arbitrary"),
        name="ssd_mixer",
    )(zx3, second_arg, dt3, state0, conv_w, conv_b.reshape(1, conv_dim), pad(dt_bias), pad(a_log),
      d_wide, expand)
    return outs


def _l1_post_kernel(y_ref, z_ref, x_ref, gw_ref, wo_ref, pw_ref, o_ref):
    z = z_ref[...]
    g = y_ref[...] * (z * jax.nn.sigmoid(z))
    o = _dot(_rms(g, gw_ref[...]).astype(BF16), wo_ref[...])
    o_ref[...] = x_ref[...] + _rms(o, pw_ref[...])


def l1_post(y, zx, x, gate_w, wo, post_w):
    m, d = x.shape
    d_inner = y.shape[1]
    tm = _row_tile(m)
    const = lambda i: (0, 0)
    return pl.pallas_call(
        _l1_post_kernel,
        grid=(m // tm,),
        in_specs=[pl.BlockSpec((tm, d_inner), lambda i: (i, 0)),
                  pl.BlockSpec((tm, d_inner), lambda i: (i, 0)),
                  pl.BlockSpec((tm, d), lambda i: (i, 0)),
                  pl.BlockSpec((1, d_inner), const),
                  pl.BlockSpec(wo.shape, const),
                  pl.BlockSpec((1, d), const)],
        out_specs=pl.BlockSpec((tm, d), lambda i: (i, 0)),
        out_shape=jax.ShapeDtypeStruct((m, d), F32),
        compiler_params=_params("parallel"),
        name="l1_post",
    )(y, zx, x, gate_w.reshape(1, d_inner), wo, post_w.reshape(1, d))


def kernel(x_prompt, x_sample, cache_k, cache_v, page_table, state_conv_a, state_conv_ssm, state_ssm, l0_norm_mix_pre, l0_w_in, l0_conv_w, l0_w_out, l0_norm_mix_post, l0_norm_ffn_pre, l0_ffn_gate, l0_ffn_up, l0_ffn_down, l0_norm_ffn_post, l1_norm_mix_pre, l1_w_in, l1_conv_w, l1_conv_b, l1_dt_bias, l1_a_log, l1_d_skip, l1_norm_gate, l1_w_out, l1_norm_mix_post, l1_norm_ffn_pre, l1_ffn_gate, l1_ffn_up, l1_ffn_down, l1_norm_ffn_post):
    bp, s_len, d = x_prompt.shape
    bd, t_len, _ = x_sample.shape
    n_heads, page, hd = cache_k.shape[1:]
    width = n_heads * hd
    assert (page_table.shape[1] * page) % MOBA_BLOCK == 0
    conv_dim = l1_conv_w.shape[1]
    ssm_heads = l1_dt_bias.shape[0]
    d_inner = ssm_heads * SSM_HEAD_DIM

    bf = lambda w: w.astype(BF16)
    xp = x_prompt.reshape(bp * s_len, d)
    xs = x_sample.reshape(bd * t_len, d)

    w_in0 = bf(l0_w_in)
    w_out0 = bf(l0_w_out)
    hp_ = norm_matmul(xp, l0_norm_mix_pre, w_in0)
    hs_ = norm_matmul(xs, l0_norm_mix_pre, w_in0)

    yb_p, kp_t, vp_t = moba_prompt(hp_.reshape(bp, s_len, -1), n_heads, hd, page)
    k_prompt = jnp.swapaxes(kp_t, 3, 4)
    v_prompt = jnp.swapaxes(vp_t, 3, 4)
    xp, conv_a_prompt = l0_post_prompt(hp_, yb_p.reshape(bp * s_len, width), xp, l0_conv_w, w_out0,
                                       l0_norm_mix_post, s_len)

    qkv_t = hs_[:, 3 * width:].reshape(bd, t_len, 3, n_heads, hd).transpose(2, 0, 3, 4, 1)
    qt4, knt4, vnt4 = qkv_t[0], qkv_t[1], qkv_t[2]
    cache_kt = jnp.swapaxes(cache_k, 2, 3)
    cache_vt = jnp.swapaxes(cache_v, 2, 3)
    sel = moba_select(page_table, qt4, cache_kt)[:, :, :t_len * MOBA_TOPK]
    yb_s = moba_attend(page_table, sel, qt4, knt4, vnt4, cache_kt, cache_vt)
    yb_s = yb_s.transpose(0, 3, 1, 2).reshape(bd * t_len, width)
    hist_rows = jnp.pad(state_conv_a, ((0, 0), (0, t_len - state_conv_a.shape[1]), (0, 0)))
    xs, gcu_s = l0_post_sample(hs_, yb_s, xs, hist_rows.reshape(bd * t_len, width), l0_conv_w,
                               w_out0, l0_norm_mix_post, t_len)
    conv_a_sample = gcu_s.reshape(bd, t_len, width)[:, t_len - 2:]
    k_sample = knt4.transpose(0, 1, 3, 2)
    v_sample = vnt4.transpose(0, 1, 3, 2)

    wg0, wu0, wd0 = bf(l0_ffn_gate), bf(l0_ffn_up), bf(l0_ffn_down)
    xp = ffn(xp, l0_norm_ffn_pre, wg0, wu0, wd0, l0_norm_ffn_post)
    xs = ffn(xs, l0_norm_ffn_pre, wg0, wu0, wd0, l0_norm_ffn_post)

    w_zx = bf(l1_w_in[:, :d_inner + conv_dim])
    w_dt = jnp.pad(l1_w_in[:, d_inner + conv_dim:], ((0, 0), (0, 128 - ssm_heads)))
    w_out1 = bf(l1_w_out)
    zx_p, dt_p = norm_matmul(xp, l1_norm_mix_pre, w_zx, w_dt)
    zx_s, dt_s = norm_matmul(xs, l1_norm_mix_pre, w_zx, w_dt)

    zeros_state = jnp.zeros((bp, d_inner, SSM_STATE), F32)
    y_p, conv_ssm_prompt, fs_p = ssd_mixer(zx_p.reshape(bp, s_len, -1), dt_p.reshape(bp, s_len, 128),
                                           None, zeros_state, l1_conv_w, l1_conv_b, l1_dt_bias,
                                           l1_a_log, l1_d_skip, d_inner)
    y_s, conv_ssm_sample, fs_s = ssd_mixer(zx_s.reshape(bd, t_len, -1), dt_s.reshape(bd, t_len, 128),
                                           state_conv_ssm, state_ssm.reshape(bd, d_inner, SSM_STATE),
                                           l1_conv_w, l1_conv_b, l1_dt_bias, l1_a_log, l1_d_skip,
                                           d_inner)
    ssm_prompt = fs_p.reshape(bp, ssm_heads, SSM_HEAD_DIM, SSM_STATE)
    ssm_sample = fs_s.reshape(bd, ssm_heads, SSM_HEAD_DIM, SSM_STATE)

    xp = l1_post(y_p.reshape(bp * s_len, d_inner), zx_p, xp, l1_norm_gate, w_out1, l1_norm_mix_post)
    xs = l1_post(y_s.reshape(bd * t_len, d_inner), zx_s, xs, l1_norm_gate, w_out1, l1_norm_mix_post)

    wg1, wu1, wd1 = bf(l1_ffn_gate), bf(l1_ffn_up), bf(l1_ffn_down)
    xp = ffn(xp, l1_norm_ffn_pre, wg1, wu1, wd1, l1_norm_ffn_post)
    xs = ffn(xs, l1_norm_ffn_pre, wg1, wu1, wd1, l1_norm_ffn_post)

    return (xp.reshape(bp, s_len, d), xs.reshape(bd, t_len, d), k_prompt, v_prompt, k_sample,
            v_sample, conv_a_prompt, conv_a_sample, conv_ssm_prompt, conv_ssm_sample, ssm_prompt,
            ssm_sample)
```

```python
import functools

import jax
import jax.numpy as jnp
from jax import lax
from jax.experimental import pallas as pl
from jax.experimental.pallas import tpu as pltpu

F32 = jnp.float32
BF16 = jnp.bfloat16
HIGHEST = lax.Precision.HIGHEST

NORM_EPS = 1e-6
MOBA_BLOCK = 256
MOBA_TOPK = 3
SSM_CHUNK = 128
SSM_HEAD_DIM = 64
SSM_STATE = 128
SSM_GROUPS = 4
NEG = -1e30
LOG2E = 1.4426950408889634
ONES_ROWS = 16

VMEM_LIMIT_BYTES = 56 * 1024 * 1024
ROW_TILE = 512


def _params(*sem):
    return pltpu.CompilerParams(dimension_semantics=sem, vmem_limit_bytes=VMEM_LIMIT_BYTES)


def _rms(x, w):
    return x * lax.rsqrt(jnp.mean(x * x, axis=-1, keepdims=True) + NORM_EPS) * w


def _dot(a, b):
    return jnp.dot(a, b, preferred_element_type=F32)


def _dot_nt(a, b, precision=None):
    return lax.dot_general(a, b, (((1,), (1,)), ((), ())), preferred_element_type=F32,
                           precision=precision)


def _row_tile(m):
    return ROW_TILE if m % ROW_TILE == 0 else m


PROJ_ROW_TILE = 256
PROJ_COL_CHUNK = 1024


def _norm_matmul_kernel(*refs, narrow):
    if narrow:
        x_ref, nw_ref, w_ref, whi_ref, wlo_ref, o_ref, o2_ref = refs
    else:
        x_ref, nw_ref, w_ref, o_ref = refs
    xn = _rms(x_ref[...], nw_ref[...])
    xh = xn.astype(BF16)
    n = w_ref.shape[1]
    for c0 in range(0, n, PROJ_COL_CHUNK):
        c1 = min(c0 + PROJ_COL_CHUNK, n)
        o_ref[:, c0:c1] = _dot(xh, w_ref[:, c0:c1])
    if narrow:
        xl = (xn - xh.astype(F32)).astype(BF16)
        o2_ref[...] = _dot(xh, whi_ref[...]) + _dot(xl, whi_ref[...]) + _dot(xh, wlo_ref[...])


def norm_matmul(x, nw, w_bf16, w_narrow=None):
    m, d = x.shape
    n = w_bf16.shape[1]
    tm = PROJ_ROW_TILE if m % PROJ_ROW_TILE == 0 else m
    const = lambda i: (0, 0)
    resident = lambda shape: pl.BlockSpec(shape, const, pipeline_mode=pl.Buffered(1))
    in_specs = [pl.BlockSpec((tm, d), lambda i: (i, 0)), pl.BlockSpec((1, d), const),
                resident((d, n))]
    out_specs = [pl.BlockSpec((tm, n), lambda i: (i, 0))]
    out_shape = [jax.ShapeDtypeStruct((m, n), F32)]
    args = [x, nw.reshape(1, d), w_bf16]
    if w_narrow is not None:
        n2 = w_narrow.shape[1]
        whi = w_narrow.astype(BF16)
        wlo = (w_narrow - whi.astype(F32)).astype(BF16)
        in_specs += [resident((d, n2)), resident((d, n2))]
        out_specs.append(pl.BlockSpec((tm, n2), lambda i: (i, 0)))
        out_shape.append(jax.ShapeDtypeStruct((m, n2), F32))
        args += [whi, wlo]
    outs = pl.pallas_call(
        functools.partial(_norm_matmul_kernel, narrow=w_narrow is not None),
        grid=(m // tm,),
        in_specs=in_specs,
        out_specs=out_specs,
        out_shape=out_shape,
        compiler_params=_params("parallel"),
        name="norm_matmul",
    )(*args)
    return outs if w_narrow is not None else outs[0]


FFN_CHUNK = 256


def _ffn_kernel(x_ref, pre_ref, wg_ref, wu_ref, wd_ref, post_ref, o_ref):
    x = x_ref[...]
    h = _rms(x, pre_ref[...]).astype(BF16)
    hidden = wg_ref.shape[1]
    acc = jnp.zeros(x.shape, F32)
    for c in range(hidden // FFN_CHUNK):
        sl = slice(c * FFN_CHUNK, (c + 1) * FFN_CHUNK)
        g = _dot(h, wg_ref[:, sl])
        u = _dot(h, wu_ref[:, sl])
        a = (g * jax.nn.sigmoid(g) * u).astype(BF16)
        acc = acc + _dot(a, wd_ref[sl, :])
    o_ref[...] = x + _rms(acc, post_ref[...])


def ffn(x, pre_w, wg, wu, wd, post_w):
    m, d = x.shape
    hidden = wg.shape[1]
    tm = _row_tile(m)
    const = lambda i: (0, 0)
    return pl.pallas_call(
        _ffn_kernel,
        grid=(m // tm,),
        in_specs=[pl.BlockSpec((tm, d), lambda i: (i, 0)),
                  pl.BlockSpec((1, d), const),
                  pl.BlockSpec((d, hidden), const, pipeline_mode=pl.Buffered(1)),
                  pl.BlockSpec((d, hidden), const, pipeline_mode=pl.Buffered(1)),
                  pl.BlockSpec((hidden, d), const, pipeline_mode=pl.Buffered(1)),
                  pl.BlockSpec((1, d), const)],
        out_specs=pl.BlockSpec((tm, d), lambda i: (i, 0)),
        out_shape=jax.ShapeDtypeStruct((m, d), F32),
        compiler_params=_params("parallel"),
        name="ffn",
    )(x, pre_w.reshape(1, d), wg, wu, wd, post_w.reshape(1, d))


def _moba_prompt_kernel(q_ref, k_ref, v_ref, o_ref, kp_ref, vp_ref, vt_ref, kb_ref, qb_ref,
                        sel_ref):
    qi = pl.program_id(2)
    s_len = k_ref.shape[1]
    nb = s_len // MOBA_BLOCK
    npages = kp_ref.shape[1]
    hd = kp_ref.shape[3]
    page = kp_ref.shape[4]
    ppb = MOBA_BLOCK // page
    scale = hd ** -0.5

    @pl.when(qi == 0)
    def _():
        kb_ref[...] = k_ref[0].astype(BF16)
        qb_ref[...] = (q_ref[0] * (scale * LOG2E)).astype(BF16)
        km = jnp.mean(k_ref[0].reshape(nb, MOBA_BLOCK, 2 * hd), axis=1)
        blk = lax.broadcasted_iota(jnp.int32, (nb, s_len), 0)
        own = lax.broadcasted_iota(jnp.int32, (nb, s_len), 1) // MOBA_BLOCK
        for hh in range(2):
            lo = hh * hd
            gate = _dot_nt(km[:, lo:lo + hd], q_ref[0, :, lo:lo + hd], precision=HIGHEST)
            cnt = jnp.zeros((nb, s_len), F32)
            for jp in range(nb - 1):
                gj = gate[jp:jp + 1, :]
                beats = ((gj > gate) | ((gj == gate) & (jp < blk))) & (jp < own)
                cnt = cnt + jnp.where(beats, 1.0, 0.0)
            picked = ((cnt < MOBA_TOPK) & (blk < own)) | (blk == own)
            limit_shift = jnp.where(picked, 0, -4 * MOBA_BLOCK)
            for qb in range(nb):
                sel_ref[hh, qb] = limit_shift[:, qb * MOBA_BLOCK:(qb + 1) * MOBA_BLOCK]
        for pg in range(npages):
            rows = slice(pg * page, (pg + 1) * page)
            kt = k_ref[0, rows, :].T
            vt = v_ref[0, rows, :].T
            kp_ref[0, pg] = kt.reshape(2, hd, page)
            vp_ref[0, pg] = vt.reshape(2, hd, page)
            cols = slice((pg % ppb) * page, (pg % ppb + 1) * page)
            for hh in range(2):
                vt_ref[pg // ppb, hh, 0:hd, cols] = vt[hh * hd:(hh + 1) * hd].astype(BF16)
        ones_row = jnp.where(lax.broadcasted_iota(jnp.int32, (ONES_ROWS, MOBA_BLOCK), 0) == 0, 1.0, 0.0)
        for j in range(nb):
            for hh in range(2):
                vt_ref[j, hh, hd:hd + ONES_ROWS, :] = ones_row.astype(BF16)

    row = lax.broadcasted_iota(jnp.int32, (MOBA_BLOCK, MOBA_BLOCK), 0)
    col_row = lax.broadcasted_iota(jnp.int32, (1, MOBA_BLOCK), 1)
    q_start = pl.multiple_of(qi * MOBA_BLOCK, MOBA_BLOCK)
    qs = [qb_ref[pl.ds(q_start, MOBA_BLOCK), hh * hd:(hh + 1) * hd] for hh in range(2)]

    def scores(j, hh):
        jc = jnp.minimum(j, qi)
        kj = kb_ref[pl.ds(pl.multiple_of(jc * MOBA_BLOCK, MOBA_BLOCK), MOBA_BLOCK),
                    hh * hd:(hh + 1) * hd]
        shift = jnp.where(j < qi, MOBA_BLOCK, jnp.where(j == qi, 0, -4 * MOBA_BLOCK))
        limit = col_row + (1 + shift) + sel_ref[hh, qi, pl.ds(jc, 1), :]
        return jnp.where(row < limit, _dot_nt(kj, qs[hh]), -jnp.inf)

    def body(i, carry):
        tiles = [scores(2 * i + b, hh) for hh in range(2) for b in range(2)]
        ja = jnp.minimum(2 * i, qi)
        jb = jnp.minimum(2 * i + 1, qi)
        new = []
        for hh in range(2):
            m, acc = carry[hh]
            sa, sb = tiles[2 * hh], tiles[2 * hh + 1]
            m_new = jnp.maximum(m, jnp.max(jnp.maximum(sa, sb), axis=0, keepdims=True))
            pa = jnp.exp2(sa - m_new).astype(BF16)
            pb = jnp.exp2(sb - m_new).astype(BF16)
            acc = jnp.exp2(m - m_new) * acc + _dot(vt_ref[ja, hh], pa) + _dot(vt_ref[jb, hh], pb)
            new.append((m_new, acc))
        return tuple(new)

    init = (jnp.full((1, MOBA_BLOCK), NEG, F32), jnp.zeros((hd + ONES_ROWS, MOBA_BLOCK), F32))
    res = lax.fori_loop(0, qi // 2 + 1, body, (init, init))
    o_ref[0] = jnp.concatenate([acc[0:hd] / acc[hd:hd + 1] for (_, acc) in res], axis=0).T


def moba_prompt(h3, n_heads, hd, page):
    bsz, s_len, _ = h3.shape
    width = n_heads * hd
    lanes = 2 * hd
    pairs = n_heads // 2
    q0 = 3 * width // lanes
    nb = s_len // MOBA_BLOCK
    pages_shape = jax.ShapeDtypeStruct((bsz, s_len // page, n_heads, hd, page), F32)
    page_spec = pl.BlockSpec((1, s_len // page, 2, hd, page), lambda b, p, i: (b, 0, p, 0, 0))
    return pl.pallas_call(
        _moba_prompt_kernel,
        grid=(bsz, pairs, nb),
        in_specs=[pl.BlockSpec((1, s_len, lanes), lambda b, p, i: (b, 0, q0 + p)),
                  pl.BlockSpec((1, s_len, lanes), lambda b, p, i: (b, 0, q0 + pairs + p)),
                  pl.BlockSpec((1, s_len, lanes), lambda b, p, i: (b, 0, q0 + 2 * pairs + p))],
        out_specs=[pl.BlockSpec((1, MOBA_BLOCK, lanes), lambda b, p, i: (b, i, p)),
                   page_spec, page_spec],
        out_shape=[jax.ShapeDtypeStruct((bsz, s_len, width), F32), pages_shape, pages_shape],
        scratch_shapes=[pltpu.VMEM((nb, 2, hd + ONES_ROWS, MOBA_BLOCK), BF16),
                        pltpu.VMEM((s_len, lanes), BF16),
                        pltpu.VMEM((s_len, lanes), BF16),
                        pltpu.VMEM((2, nb, nb, MOBA_BLOCK), jnp.int32)],
        compiler_params=_params("parallel", "parallel", "arbitrary"),
        name="moba_prompt",
    )(h3, h3, h3)


def _conv3_mix(gb, gcu, prev2, prev1, li, cw):
    s1 = jnp.where(li == 0, prev1[0], pltpu.roll(gcu, 1, 0))
    s2 = pltpu.roll(gcu, 2, 0)
    s2 = jnp.where(li == 0, prev2[0], jnp.where(li == 1, prev2[1], s2))
    conv = cw[0:1] * s2 + cw[1:2] * s1 + cw[2:3] * gcu
    return gb * conv


def _out_proj_residual(ya, yb, x, wo_ref, pw):
    half = ya.shape[1]
    y = _dot(ya.astype(BF16), wo_ref[0:half, :]) + _dot(yb.astype(BF16), wo_ref[half:, :])
    return x + _rms(y, pw)


def _l0_post_prompt_kernel(gb_ref, gc_ref, u_ref, gch_ref, uh_ref, yb_ref, x_ref, cw_ref, wo_ref,
                           pw_ref, xo_ref, st_ref, *, tiles_per_seq):
    i = pl.program_id(0)
    tm = gb_ref.shape[0]
    gcu = gc_ref[...] * u_ref[...]
    halo = gch_ref[...] * uh_ref[...]
    halo = halo * (i % tiles_per_seq != 0).astype(F32)
    li = lax.broadcasted_iota(jnp.int32, gcu.shape, 0)
    ya = _conv3_mix(gb_ref[...], gcu, (halo[6:7], halo[7:8]), (halo[7:8],), li, cw_ref[...])
    xo_ref[...] = _out_proj_residual(ya, yb_ref[...], x_ref[...], wo_ref, pw_ref[...])

    @pl.when(i % tiles_per_seq == tiles_per_seq - 1)
    def _():
        st_ref[0] = gcu[tm - 2:tm, :]


def l0_post_prompt(h2, yb, x, conv_w, wo, post_w, seq_len):
    m, d = x.shape
    width = yb.shape[1]
    tm = ROW_TILE
    tiles_per_seq = seq_len // tm
    halo_idx = lambda i: jnp.maximum(i * (tm // 8) - 1, 0)
    const = lambda i: (0, 0)
    return pl.pallas_call(
        functools.partial(_l0_post_prompt_kernel, tiles_per_seq=tiles_per_seq),
        grid=(m // tm,),
        in_specs=[pl.BlockSpec((tm, width), lambda i: (i, 0)),
                  pl.BlockSpec((tm, width), lambda i: (i, 1)),
                  pl.BlockSpec((tm, width), lambda i: (i, 2)),
                  pl.BlockSpec((8, width), lambda i: (halo_idx(i), 1)),
                  pl.BlockSpec((8, width), lambda i: (halo_idx(i), 2)),
                  pl.BlockSpec((tm, width), lambda i: (i, 0)),
                  pl.BlockSpec((tm, d), lambda i: (i, 0)),
                  pl.BlockSpec(conv_w.shape, const),
                  pl.BlockSpec(wo.shape, const),
                  pl.BlockSpec((1, d), const)],
        out_specs=[pl.BlockSpec((tm, d), lambda i: (i, 0)),
                   pl.BlockSpec((1, 2, width), lambda i: (i // tiles_per_seq, 0, 0))],
        out_shape=[jax.ShapeDtypeStruct((m, d), F32),
                   jax.ShapeDtypeStruct((m // seq_len, 2, width), F32)],
        compiler_params=_params("arbitrary"),
        name="l0_post_prompt",
    )(h2, h2, h2, h2, h2, yb, x, conv_w, wo, post_w.reshape(1, d))


def _l0_post_sample_kernel(gb_ref, gc_ref, u_ref, hist_ref, yb_ref, x_ref, cw_ref, wo_ref, pw_ref,
                           xo_ref, gcu_ref, *, seq_len):
    gcu = gc_ref[...] * u_ref[...]
    rows = gcu.shape[0]
    t = lax.broadcasted_iota(jnp.int32, gcu.shape, 0) % seq_len
    hist = hist_ref[...]
    hist_next = pltpu.roll(hist, rows - 1, 0)
    ya = _conv3_mix(gb_ref[...], gcu, (hist, hist), (hist_next,), t, cw_ref[...])
    xo_ref[...] = _out_proj_residual(ya, yb_ref[...], x_ref[...], wo_ref, pw_ref[...])
    gcu_ref[...] = gcu


def l0_post_sample(h2, yb, x, hist_rows, conv_w, wo, post_w, seq_len):
    m, d = x.shape
    width = yb.shape[1]
    const = lambda i: (0, 0)
    return pl.pallas_call(
        functools.partial(_l0_post_sample_kernel, seq_len=seq_len),
        grid=(1,),
        in_specs=[pl.BlockSpec((m, width), lambda i: (0, 0)),
                  pl.BlockSpec((m, width), lambda i: (0, 1)),
                  pl.BlockSpec((m, width), lambda i: (0, 2)),
                  pl.BlockSpec((m, width), const),
                  pl.BlockSpec((m, width), const),
                  pl.BlockSpec((m, d), const),
                  pl.BlockSpec(conv_w.shape, const),
                  pl.BlockSpec(wo.shape, const),
                  pl.BlockSpec((1, d), const)],
        out_specs=[pl.BlockSpec((m, d), const), pl.BlockSpec((m, width), const)],
        out_shape=[jax.ShapeDtypeStruct((m, d), F32), jax.ShapeDtypeStruct((m, width), F32)],
        compiler_params=_params("arbitrary"),
        name="l0_post_sample",
    )(h2, h2, h2, hist_rows, yb, x, conv_w, wo, post_w.reshape(1, d))


MEAN_BUFFERS = 16
BLOCKS_PER_TRIP = 2
ATTEND_SLOTS = 4


def _moba_select_kernel(pt_ref, qt_ref, ck_ref, sel_ref, buf, sem, km_ref, *, pages_per_block,
                        n_blk):
    bd, n_h, hd, t_len = qt_ref.shape
    page = ck_ref.shape[3]
    n_pages = n_blk * pages_per_block
    total = bd * n_pages

    def copy(i):
        slot = i % MEAN_BUFFERS
        return pltpu.make_async_copy(ck_ref.at[pt_ref[i]], buf.at[slot], sem.at[slot])

    for s in range(MEAN_BUFFERS - 1):
        copy(s).start()

    lane3 = lax.broadcasted_iota(jnp.int32, (n_h, hd, 128), 2)
    lane = lax.broadcasted_iota(jnp.int32, (n_h, 128), 1)
    sub = lax.broadcasted_iota(jnp.int32, (n_h, 128), 0)

    def per_seq(b, _):
        km_ref[...] = jnp.zeros(km_ref.shape, F32)

        def block_mean(n):
            acc = None
            for jj in range(pages_per_block):
                i = b * n_pages + n * pages_per_block + jj
                nxt = i + MEAN_BUFFERS - 1

                @pl.when(nxt < total)
                def _():
                    copy(nxt).start()

                copy(i).wait()
                acc = buf[i % MEAN_BUFFERS] if acc is None else acc + buf[i % MEAN_BUFFERS]
            return jnp.sum(acc, axis=-1, keepdims=True) * (1.0 / (pages_per_block * page))

        def per_group(g, _):
            means = [block_mean(g * BLOCKS_PER_TRIP + u) for u in range(BLOCKS_PER_TRIP)]
            km = km_ref[...]
            for u, mean in enumerate(means):
                km = jnp.where(lane3 == g * BLOCKS_PER_TRIP + u, mean, km)
            km_ref[...] = km
            return 0

        lax.fori_loop(0, n_blk // BLOCKS_PER_TRIP, per_group, 0)
        out = jnp.zeros((n_h, 128), jnp.int32)
        for t in range(t_len):
            gate = jnp.full((n_h, 128), -jnp.inf, F32)
            for h in range(n_h):
                g_h = jnp.sum(km_ref[h] * qt_ref[b, h][:, t:t + 1], axis=0, keepdims=True)
                gate = jnp.where(sub == h, g_h, gate)
            gate = jnp.where(lane < n_blk, gate, -jnp.inf)
            for r in range(MOBA_TOPK):
                best = jnp.max(gate, axis=-1, keepdims=True)
                idx = jnp.min(jnp.where(gate == best, lane, 128), axis=-1, keepdims=True)
                out = jnp.where(lane == t * MOBA_TOPK + r, idx, out)
                gate = jnp.where(lane == idx, -jnp.inf, gate)
        sel_ref[b] = out
        return 0

    lax.fori_loop(0, bd, per_seq, 0)


def moba_select(page_table, qt4, cache_kt):
    bd, n_pages = page_table.shape
    _, n_h, hd, t_len = qt4.shape
    page = cache_kt.shape[3]
    ppb = MOBA_BLOCK // page
    n_blk = n_pages // ppb
    assert n_blk <= 128 and t_len * MOBA_TOPK <= 128
    grid_spec = pltpu.PrefetchScalarGridSpec(
        num_scalar_prefetch=1,
        grid=(1,),
        in_specs=[pl.BlockSpec(qt4.shape, lambda i, pt: (0, 0, 0, 0)),
                  pl.BlockSpec(memory_space=pl.ANY)],
        out_specs=pl.BlockSpec((bd, n_h, 128), lambda i, pt: (0, 0, 0)),
        scratch_shapes=[pltpu.VMEM((MEAN_BUFFERS, n_h, hd, page), F32),
                        pltpu.SemaphoreType.DMA((MEAN_BUFFERS,)),
                        pltpu.VMEM((n_h, hd, 128), F32)],
    )
    return pl.pallas_call(
        functools.partial(_moba_select_kernel, pages_per_block=ppb, n_blk=n_blk),
        grid_spec=grid_spec,
        out_shape=jax.ShapeDtypeStruct((bd, n_h, 128), jnp.int32),
        compiler_params=_params("arbitrary"),
        name="moba_select",
    )(page_table.reshape(-1), qt4, cache_kt)


def _moba_attend_kernel(pt_ref, sel_ref, qt_ref, knt_ref, vnt_ref, ck_ref, cv_ref, o_ref,
                        kbuf, vbuf, sem, *, pages_per_block, n_pages):
    bd, n_h, hd, t_len = qt_ref.shape
    page = ck_ref.shape[3]
    per_t = MOBA_TOPK * pages_per_block
    scale = hd ** -0.5
    step = lax.broadcasted_iota(jnp.int32, (1, t_len), 1)
    step_col = lax.broadcasted_iota(jnp.int32, (hd, t_len), 1)

    def copies(pair, slot):
        b = pair // n_h
        h = pair % n_h
        out = []
        for t in range(t_len):
            for r in range(MOBA_TOPK):
                blk = sel_ref[(b * n_h + h) * (t_len * MOBA_TOPK) + t * MOBA_TOPK + r]
                for jj in range(pages_per_block):
                    phys = pt_ref[b * n_pages + blk * pages_per_block + jj]
                    s = t * per_t + r * pages_per_block + jj
                    out.append(pltpu.make_async_copy(ck_ref.at[phys, h], kbuf.at[slot, s],
                                                     sem.at[slot]))
                    out.append(pltpu.make_async_copy(cv_ref.at[phys, h], vbuf.at[slot, s],
                                                     sem.at[slot]))
        return out

    for ahead in range(ATTEND_SLOTS - 1):
        for c in copies(ahead, ahead):
            c.start()

    def per_pair(pair, _):
        slot = pair % ATTEND_SLOTS
        b = pair // n_h
        h = pair % n_h
        nxt = pair + ATTEND_SLOTS - 1

        @pl.when(nxt < bd * n_h)
        def _():
            for c in copies(nxt, nxt % ATTEND_SLOTS):
                c.start()

        for c in copies(pair, slot):
            c.wait()

        q_all = qt_ref[b, h] * scale
        k_new = knt_ref[b, h]
        v_new = vnt_ref[b, h]
        out = jnp.zeros((hd, t_len), F32)
        for t in range(t_len):
            qc = q_all[:, t:t + 1]
            s_sel = [jnp.sum(kbuf[slot, t * per_t + s] * qc, axis=0, keepdims=True)
                     for s in range(per_t)]
            s_own = jnp.where(step <= t, jnp.sum(k_new * qc, axis=0, keepdims=True), NEG)
            m = jnp.max(s_own, axis=-1, keepdims=True)
            for s in s_sel:
                m = jnp.maximum(m, jnp.max(s, axis=-1, keepdims=True))
            p_own = jnp.exp(s_own - m)
            l = jnp.sum(p_own, axis=-1, keepdims=True)
            acc_own = jnp.sum(v_new * p_own, axis=-1, keepdims=True)
            acc = jnp.zeros((hd, page), F32)
            for si, s in enumerate(s_sel):
                p = jnp.exp(s - m)
                l = l + jnp.sum(p, axis=-1, keepdims=True)
                acc = acc + vbuf[slot, t * per_t + si] * p
            o_t = (acc_own + jnp.sum(acc, axis=-1, keepdims=True)) / l
            out = jnp.where(step_col == t, o_t, out)
        o_ref[b, h] = out
        return 0

    lax.fori_loop(0, bd * n_h, per_pair, 0)


def moba_attend(page_table, sel, qt4, knt4, vnt4, cache_kt, cache_vt):
    bd, n_pages = page_table.shape
    _, n_h, hd, t_len = qt4.shape
    page = cache_kt.shape[3]
    ppb = MOBA_BLOCK // page
    n_slab = t_len * MOBA_TOPK * ppb
    full = pl.BlockSpec(qt4.shape, lambda i, pt, sl: (0, 0, 0, 0))
    grid_spec = pltpu.PrefetchScalarGridSpec(
        num_scalar_prefetch=2,
        grid=(1,),
        in_specs=[full, full, full, pl.BlockSpec(memory_space=pl.ANY),
                  pl.BlockSpec(memory_space=pl.ANY)],
        out_specs=full,
        scratch_shapes=[pltpu.VMEM((ATTEND_SLOTS, n_slab, hd, page), F32),
                        pltpu.VMEM((ATTEND_SLOTS, n_slab, hd, page), F32),
                        pltpu.SemaphoreType.DMA((ATTEND_SLOTS,))],
    )
    return pl.pallas_call(
        functools.partial(_moba_attend_kernel, pages_per_block=ppb, n_pages=n_pages),
        grid_spec=grid_spec,
        out_shape=jax.ShapeDtypeStruct(qt4.shape, F32),
        compiler_params=_params("arbitrary"),
        name="moba_attend",
    )(page_table.reshape(-1), sel.reshape(-1), qt4, knt4, vnt4, cache_kt, cache_vt)


def _softplus(x):
    return jnp.maximum(x, 0.0) + jnp.log1p(jnp.exp(-jnp.abs(x)))


def _ssd_kernel(*refs, d_inner, rows_valid, has_halo):
    if has_halo:
        (zx_ref, halo_ref, dt_ref, s0_ref, cw_ref, cb_ref, dtb_ref, alog_ref, dskip_ref, e_ref,
         y_ref, cs_ref, fs_ref, st_ref, xin_ref, dtin_ref, y_scr) = refs
        hist_ref = None
    else:
        (zx_ref, hist_ref, dt_ref, s0_ref, cw_ref, cb_ref, dtb_ref, alog_ref, dskip_ref, e_ref,
         y_ref, cs_ref, fs_ref, st_ref, xin_ref, dtin_ref, y_scr) = refs
        halo_ref = None
    c = pl.program_id(1)
    n_chunks = pl.num_programs(1)
    cl = SSM_CHUNK
    rows_blk = zx_ref.shape[1]
    conv_dim = cw_ref.shape[1]
    n_st = SSM_STATE
    hp = SSM_HEAD_DIM
    heads_per_group = d_inner // hp // SSM_GROUPS
    gw = heads_per_group * hp

    @pl.when(c == 0)
    def _():
        st_ref[...] = s0_ref[0].T

    if has_halo:
        xin_ref[0:8, :] = halo_ref[0, :, d_inner:d_inner + conv_dim] * (c != 0).astype(F32)
    else:
        xin_ref[0:8, :] = jnp.zeros((8, conv_dim), F32)
        xin_ref[5:8, :] = hist_ref[0]
    if rows_blk == cl:
        xin_ref[8:8 + cl, :] = zx_ref[0, :, d_inner:d_inner + conv_dim]
        dt_raw = dt_ref[0]
    else:
        xin_ref[8:8 + cl, :] = jnp.zeros((cl, conv_dim), F32)
        dtin_ref[...] = jnp.zeros(dtin_ref.shape, F32)
        xin_ref[8:8 + rows_blk, :] = zx_ref[0, :, d_inner:d_inner + conv_dim]
        dtin_ref[0:rows_blk, :] = dt_ref[0]
        dt_raw = dtin_ref[...]

    cw = cw_ref[...]
    xbc = xin_ref[8:8 + cl, :]
    conv = (cw[0:1] * xin_ref[5:5 + cl, :] + cw[1:2] * xin_ref[6:6 + cl, :]
            + cw[2:3] * xin_ref[7:7 + cl, :] + cw[3:4] * xbc + cb_ref[...])
    act = conv * jax.nn.sigmoid(conv)

    @pl.when(c == n_chunks - 1)
    def _():
        cs_ref[0] = xbc[rows_valid - 3:rows_valid, :]

    dt = _softplus(dt_raw + dtb_ref[...])
    if rows_valid < cl:
        dt = jnp.where(lax.broadcasted_iota(jnp.int32, dt.shape, 0) < rows_valid, dt, 0.0)
    a = -jnp.exp(alog_ref[...])
    dta = dt * a
    r_i = lax.broadcasted_iota(jnp.int32, (cl, cl), 0)
    c_i = lax.broadcasted_iota(jnp.int32, (cl, cl), 1)
    causal = c_i <= r_i
    first_head_lanes = lax.broadcasted_iota(jnp.int32, (cl, 2 * hp), 1) < hp
    tril = jnp.where(causal, 1.0, 0.0)
    acum = jnp.dot(tril, dta, preferred_element_type=F32, precision=HIGHEST)
    acum_t = acum.T
    alast = acum[cl - 1:cl, :]
    narrow = jnp.concatenate([dt, jnp.exp(acum), jnp.exp(alast - acum),
                              jnp.broadcast_to(jnp.exp(alast), (8, alast.shape[1]))], axis=0)
    hi = narrow.astype(BF16)
    mid = (narrow - hi.astype(F32)).astype(BF16)
    wide = _dot(jnp.concatenate([hi, mid], axis=1), e_ref[...])
    dt_w = wide[0:cl]
    eac_w = wide[cl:2 * cl]
    dte_w = wide[2 * cl:3 * cl]
    cd_w = wide[3 * cl:3 * cl + 1]

    xs = act[:, 0:d_inner]
    xdt = xs * dt_w
    xdt_b = xdt.astype(BF16)
    xdtd_b = (xdt * dte_w).astype(BF16)
    for g in range(SSM_GROUPS):
        b_g = act[:, d_inner + g * n_st:d_inner + (g + 1) * n_st]
        c_g = act[:, d_inner + (SSM_GROUPS + g) * n_st:d_inner + (SSM_GROUPS + g + 1) * n_st]
        c_gb = c_g.astype(BF16)
        cb = _dot_nt(c_gb, b_g.astype(BF16))
        s_g = st_ref[:, g * gw:(g + 1) * gw]
        y_off = _dot(c_gb, s_g.astype(BF16)) * eac_w[:, g * gw:(g + 1) * gw]
        for r in range(0, heads_per_group, 2):
            h = g * heads_per_group + r
            stacked = []
            for hh in (h, h + 1):
                seg = acum[:, hh:hh + 1] - acum_t[hh:hh + 1, :]
                decay = jnp.exp(jnp.where(causal, seg, NEG))
                stacked.append((cb * decay).astype(BF16))
            both = _dot(jnp.concatenate(stacked, axis=0), xdt_b[:, h * hp:(h + 2) * hp])
            y_d = jnp.where(first_head_lanes, both[0:cl], both[cl:2 * cl])
            y_scr[:, h * hp:(h + 2) * hp] = y_d + y_off[:, r * hp:(r + 2) * hp]
        st_ref[:, g * gw:(g + 1) * gw] = (s_g * cd_w[:, g * gw:(g + 1) * gw]
                                          + _dot(b_g.T.astype(BF16),
                                                 xdtd_b[:, g * gw:(g + 1) * gw]))
    y = y_scr[...] + dskip_ref[...] * xs
    y_ref[0] = y[0:rows_blk, :]

    @pl.when(c == n_chunks - 1)
    def _():
        fs_ref[0] = st_ref[...].T


def ssd_mixer(zx3, dt3, hist, state0, conv_w, conv_b, dt_bias, a_log, d_skip, d_inner):
    nseq, t_len, _ = zx3.shape
    conv_dim = conv_w.shape[1]
    n_heads = d_inner // SSM_HEAD_DIM
    cl = SSM_CHUNK
    has_halo = hist is None
    if has_halo:
        rows_blk, rows_valid, n_chunks = cl, cl, t_len // cl
    else:
        rows_blk, rows_valid, n_chunks = t_len, t_len, 1
    pad = lambda v: jnp.pad(v.astype(F32), (0, 128 - n_heads)).reshape(1, 128)
    expand = (jnp.arange(256)[:, None] % 128
              == (jnp.arange(d_inner)[None, :] // SSM_HEAD_DIM)).astype(BF16)
    d_wide = jnp.repeat(d_skip.astype(F32), SSM_HEAD_DIM).reshape(1, d_inner)
    const2 = lambda s, c: (0, 0)
    if has_halo:
        second = pl.BlockSpec((1, 8, zx3.shape[2]),
                              lambda s, c: (s, jnp.maximum(c * (cl // 8) - 1, 0), 0))
        second_arg = zx3
    else:
        second = pl.BlockSpec((1,) + hist.shape[1:], lambda s, c: (s, 0, 0))
        second_arg = hist
    outs = pl.pallas_call(
        functools.partial(_ssd_kernel, d_inner=d_inner, rows_valid=rows_valid, has_halo=has_halo),
        grid=(nseq, n_chunks),
        in_specs=[pl.BlockSpec((1, rows_blk, zx3.shape[2]), lambda s, c: (s, c, 0)),
                  second,
                  pl.BlockSpec((1, rows_blk, 128), lambda s, c: (s, c, 0)),
                  pl.BlockSpec((1, d_inner, SSM_STATE), lambda s, c: (s, 0, 0)),
                  pl.BlockSpec(conv_w.shape, const2),
                  pl.BlockSpec((1, conv_dim), const2),
                  pl.BlockSpec((1, 128), const2),
                  pl.BlockSpec((1, 128), const2),
                  pl.BlockSpec((1, d_inner), const2),
                  pl.BlockSpec((256, d_inner), const2)],
        out_specs=[pl.BlockSpec((1, rows_blk, d_inner), lambda s, c: (s, c, 0)),
                   pl.BlockSpec((1, 3, conv_dim), lambda s, c: (s, 0, 0)),
                   pl.BlockSpec((1, d_inner, SSM_STATE), lambda s, c: (s, 0, 0))],
        out_shape=[jax.ShapeDtypeStruct((nseq, t_len, d_inner), F32),
                   jax.ShapeDtypeStruct((nseq, 3, conv_dim), F32),
                   jax.ShapeDtypeStruct((nseq, d_inner, SSM_STATE), F32)],
        scratch_shapes=[pltpu.VMEM((SSM_STATE, d_inner), F32),
                        pltpu.VMEM((cl + 8, conv_dim), F32),
                        pltpu.VMEM((cl, 128), F32),
                        pltpu.VMEM((cl, d_inner), F32)],
        compiler_params=_params("parallel", "arbitrary"),
        name="ssd_mixer",
    )(zx3, second_arg, dt3, state0, conv_w, conv_b.reshape(1, conv_dim), pad(dt_bias), pad(a_log),
      d_wide, expand)
    return outs


def _l1_post_kernel(y_ref, z_ref, x_ref, gw_ref, wo_ref, pw_ref, o_ref):
    z = z_ref[...]
    g = y_ref[...] * (z * jax.nn.sigmoid(z))
    o = _dot(_rms(g, gw_ref[...]).astype(BF16), wo_ref[...])
    o_ref[...] = x_ref[...] + _rms(o, pw_ref[...])


def l1_post(y, zx, x, gate_w, wo, post_w):
    m, d = x.shape
    d_inner = y.shape[1]
    tm = _row_tile(m)
    const = lambda i: (0, 0)
    return pl.pallas_call(
        _l1_post_kernel,
        grid=(m // tm,),
        in_specs=[pl.BlockSpec((tm, d_inner), lambda i: (i, 0)),
                  pl.BlockSpec((tm, d_inner), lambda i: (i, 0)),
                  pl.BlockSpec((tm, d), lambda i: (i, 0)),
                  pl.BlockSpec((1, d_inner), const),
                  pl.BlockSpec(wo.shape, const),
                  pl.BlockSpec((1, d), const)],
        out_specs=pl.BlockSpec((tm, d), lambda i: (i, 0)),
        out_shape=jax.ShapeDtypeStruct((m, d), F32),
        compiler_params=_params("parallel"),
        name="l1_post",
    )(y, zx, x, gate_w.reshape(1, d_inner), wo, post_w.reshape(1, d))


def kernel(x_prompt, x_sample, cache_k, cache_v, page_table, state_conv_a, state_conv_ssm, state_ssm, l0_norm_mix_pre, l0_w_in, l0_conv_w, l0_w_out, l0_norm_mix_post, l0_norm_ffn_pre, l0_ffn_gate, l0_ffn_up, l0_ffn_down, l0_norm_ffn_post, l1_norm_mix_pre, l1_w_in, l1_conv_w, l1_conv_b, l1_dt_bias, l1_a_log, l1_d_skip, l1_norm_gate, l1_w_out, l1_norm_mix_post, l1_norm_ffn_pre, l1_ffn_gate, l1_ffn_up, l1_ffn_down, l1_norm_ffn_post):
    bp, s_len, d = x_prompt.shape
    bd, t_len, _ = x_sample.shape
    n_heads, page, hd = cache_k.shape[1:]
    width = n_heads * hd
    assert (page_table.shape[1] * page) % MOBA_BLOCK == 0
    conv_dim = l1_conv_w.shape[1]
    ssm_heads = l1_dt_bias.shape[0]
    d_inner = ssm_heads * SSM_HEAD_DIM

    bf = lambda w: w.astype(BF16)
    xp = x_prompt.reshape(bp * s_len, d)
    xs = x_sample.reshape(bd * t_len, d)

    w_in0 = bf(l0_w_in)
    w_out0 = bf(l0_w_out)
    hp_ = norm_matmul(xp, l0_norm_mix_pre, w_in0)
    hs_ = norm_matmul(xs, l0_norm_mix_pre, w_in0)

    yb_p, kp_t, vp_t = moba_prompt(hp_.reshape(bp, s_len, -1), n_heads, hd, page)
    k_prompt = jnp.swapaxes(kp_t, 3, 4)
    v_prompt = jnp.swapaxes(vp_t, 3, 4)
    xp, conv_a_prompt = l0_post_prompt(hp_, yb_p.reshape(bp * s_len, width), xp, l0_conv_w, w_out0,
                                       l0_norm_mix_post, s_len)

    qkv_t = hs_[:, 3 * width:].reshape(bd, t_len, 3, n_heads, hd).transpose(2, 0, 3, 4, 1)
    qt4, knt4, vnt4 = qkv_t[0], qkv_t[1], qkv_t[2]
    cache_kt = jnp.swapaxes(cache_k, 2, 3)
    cache_vt = jnp.swapaxes(cache_v, 2, 3)
    sel = moba_select(page_table, qt4, cache_kt)[:, :, :t_len * MOBA_TOPK]
    yb_s = moba_attend(page_table, sel, qt4, knt4, vnt4, cache_kt, cache_vt)
    yb_s = yb_s.transpose(0, 3, 1, 2).reshape(bd * t_len, width)
    hist_rows = jnp.pad(state_conv_a, ((0, 0), (0, t_len - state_conv_a.shape[1]), (0, 0)))
    xs, gcu_s = l0_post_sample(hs_, yb_s, xs, hist_rows.reshape(bd * t_len, width), l0_conv_w,
                               w_out0, l0_norm_mix_post, t_len)
    conv_a_sample = gcu_s.reshape(bd, t_len, width)[:, t_len - 2:]
    k_sample = knt4.transpose(0, 1, 3, 2)
    v_sample = vnt4.transpose(0, 1, 3, 2)

    wg0, wu0, wd0 = bf(l0_ffn_gate), bf(l0_ffn_up), bf(l0_ffn_down)
    xp = ffn(xp, l0_norm_ffn_pre, wg0, wu0, wd0, l0_norm_ffn_post)
    xs = ffn(xs, l0_norm_ffn_pre, wg0, wu0, wd0, l0_norm_ffn_post)

    w_zx = bf(l1_w_in[:, :d_inner + conv_dim])
    w_dt = jnp.pad(l1_w_in[:, d_inner + conv_dim:], ((0, 0), (0, 128 - ssm_heads)))
    w_out1 = bf(l1_w_out)
    zx_p, dt_p = norm_matmul(xp, l1_norm_mix_pre, w_zx, w_dt)
    zx_s, dt_s = norm_matmul(xs, l1_norm_mix_pre, w_zx, w_dt)

    zeros_state = jnp.zeros((bp, d_inner, SSM_STATE), F32)
    y_p, conv_ssm_prompt, fs_p = ssd_mixer(zx_p.reshape(bp, s_len, -1), dt_p.reshape(bp, s_len, 128),
                                           None, zeros_state, l1_conv_w, l1_conv_b, l1_dt_bias,
                                           l1_a_log, l1_d_skip, d_inner)
    y_s, conv_ssm_sample, fs_s = ssd_mixer(zx_s.reshape(bd, t_len, -1), dt_s.reshape(bd, t_len, 128),
                                           state_conv_ssm, state_ssm.reshape(bd, d_inner, SSM_STATE),
                                           l1_conv_w, l1_conv_b, l1_dt_bias, l1_a_log, l1_d_skip,
                                           d_inner)
    ssm_prompt = fs_p.reshape(bp, ssm_heads, SSM_HEAD_DIM, SSM_STATE)
    ssm_sample = fs_s.reshape(bd, ssm_heads, SSM_HEAD_DIM, SSM_STATE)

    xp = l1_post(y_p.reshape(bp * s_len, d_inner), zx_p, xp, l1_norm_gate, w_out1, l1_norm_mix_post)
    xs = l1_post(y_s.reshape(bd * t_len, d_inner), zx_s, xs, l1_norm_gate, w_out1, l1_norm_mix_post)

    wg1, wu1, wd1 = bf(l1_ffn_gate), bf(l1_ffn_up), bf(l1_ffn_down)
    xp = ffn(xp, l1_norm_ffn_pre, wg1, wu1, wd1, l1_norm_ffn_post)
    xs = ffn(xs, l1_norm_ffn_pre, wg1, wu1, wd1, l1_norm_ffn_post)

    return (xp.reshape(bp, s_len, d), xs.reshape(bd, t_len, d), k_prompt, v_prompt, k_sample,
            v_sample, conv_a_prompt, conv_a_sample, conv_ssm_prompt, conv_ssm_sample, ssm_prompt,
            ssm_sample)
```

```python
import functools

import jax
import jax.numpy as jnp
from jax import lax
from jax.experimental import pallas as pl
from jax.experimental.pallas import tpu as pltpu

F32 = jnp.float32
BF16 = jnp.bfloat16
HIGHEST = lax.Precision.HIGHEST

NORM_EPS = 1e-6
MOBA_BLOCK = 256
MOBA_TOPK = 3
SSM_CHUNK = 128
SSM_HEAD_DIM = 64
SSM_STATE = 128
SSM_GROUPS = 4
NEG = -1e30
LOG2E = 1.4426950408889634
ONES_ROWS = 16

VMEM_LIMIT_BYTES = 56 * 1024 * 1024
ROW_TILE = 512


def _params(*sem):
    return pltpu.CompilerParams(dimension_semantics=sem, vmem_limit_bytes=VMEM_LIMIT_BYTES)


def _rms(x, w):
    return x * lax.rsqrt(jnp.mean(x * x, axis=-1, keepdims=True) + NORM_EPS) * w


def _dot(a, b):
    return jnp.dot(a, b, preferred_element_type=F32)


def _dot_nt(a, b, precision=None):
    return lax.dot_general(a, b, (((1,), (1,)), ((), ())), preferred_element_type=F32,
                           precision=precision)


def _row_tile(m, tile=ROW_TILE):
    return tile if m % tile == 0 else m


PROJ_ROW_TILE = 512
PROJ_COL_CHUNK = 1024


def _norm_matmul_kernel(*refs, narrow):
    if narrow:
        x_ref, nw_ref, w_ref, whi_ref, wlo_ref, o_ref, o2_ref = refs
    else:
        x_ref, nw_ref, w_ref, o_ref = refs
    xn = _rms(x_ref[...], nw_ref[...])
    xh = xn.astype(BF16)
    n = w_ref.shape[1]
    for c0 in range(0, n, PROJ_COL_CHUNK):
        c1 = min(c0 + PROJ_COL_CHUNK, n)
        o_ref[:, c0:c1] = _dot(xh, w_ref[:, c0:c1])
    if narrow:
        xl = (xn - xh.astype(F32)).astype(BF16)
        o2_ref[...] = _dot(xh, whi_ref[...]) + _dot(xl, whi_ref[...]) + _dot(xh, wlo_ref[...])


def norm_matmul(x, nw, w_bf16, w_narrow=None):
    m, d = x.shape
    n = w_bf16.shape[1]
    tm = PROJ_ROW_TILE if m % PROJ_ROW_TILE == 0 else m
    const = lambda i: (0, 0)
    resident = lambda shape: pl.BlockSpec(shape, const, pipeline_mode=pl.Buffered(1))
    in_specs = [pl.BlockSpec((tm, d), lambda i: (i, 0)), pl.BlockSpec((1, d), const),
                resident((d, n))]
    out_specs = [pl.BlockSpec((tm, n), lambda i: (i, 0))]
    out_shape = [jax.ShapeDtypeStruct((m, n), F32)]
    args = [x, nw.reshape(1, d), w_bf16]
    if w_narrow is not None:
        n2 = w_narrow.shape[1]
        whi = w_narrow.astype(BF16)
        wlo = (w_narrow - whi.astype(F32)).astype(BF16)
        in_specs += [resident((d, n2)), resident((d, n2))]
        out_specs.append(pl.BlockSpec((tm, n2), lambda i: (i, 0)))
        out_shape.append(jax.ShapeDtypeStruct((m, n2), F32))
        args += [whi, wlo]
    outs = pl.pallas_call(
        functools.partial(_norm_matmul_kernel, narrow=w_narrow is not None),
        grid=(m // tm,),
        in_specs=in_specs,
        out_specs=out_specs,
        out_shape=out_shape,
        compiler_params=_params("parallel"),
        name="norm_matmul",
    )(*args)
    return outs if w_narrow is not None else outs[0]


FFN_CHUNK = 256
FFN_ROW_TILE = 1024


def _ffn_kernel(x_ref, pre_ref, wg_ref, wu_ref, wd_ref, post_ref, o_ref):
    x = x_ref[...]
    h = _rms(x, pre_ref[...]).astype(BF16)
    hidden = wg_ref.shape[1]
    acc = jnp.zeros(x.shape, F32)
    for c in range(hidden // FFN_CHUNK):
        sl = slice(c * FFN_CHUNK, (c + 1) * FFN_CHUNK)
        g = _dot(h, wg_ref[:, sl])
        u = _dot(h, wu_ref[:, sl])
        a = (g * jax.nn.sigmoid(g) * u).astype(BF16)
        acc = acc + _dot(a, wd_ref[sl, :])
    o_ref[...] = x + _rms(acc, post_ref[...])


def ffn(x, pre_w, wg, wu, wd, post_w):
    m, d = x.shape
    hidden = wg.shape[1]
    tm = _row_tile(m, FFN_ROW_TILE)
    const = lambda i: (0, 0)
    return pl.pallas_call(
        _ffn_kernel,
        grid=(m // tm,),
        in_specs=[pl.BlockSpec((tm, d), lambda i: (i, 0)),
                  pl.BlockSpec((1, d), const),
                  pl.BlockSpec((d, hidden), const, pipeline_mode=pl.Buffered(1)),
                  pl.BlockSpec((d, hidden), const, pipeline_mode=pl.Buffered(1)),
                  pl.BlockSpec((hidden, d), const, pipeline_mode=pl.Buffered(1)),
                  pl.BlockSpec((1, d), const)],
        out_specs=pl.BlockSpec((tm, d), lambda i: (i, 0)),
        out_shape=jax.ShapeDtypeStruct((m, d), F32),
        compiler_params=_params("parallel"),
        name="ffn",
    )(x, pre_w.reshape(1, d), wg, wu, wd, post_w.reshape(1, d))


def _moba_work_items(nb):
    items = [(qi, i, int(i == qi // 2)) for qi in range(nb) for i in range(qi // 2 + 1)]
    return [list(col) for col in zip(*items)]


def _moba_prompt_kernel(tab_ref, q_ref, k_ref, v_ref, o_ref, kp_ref, vp_ref, vt_ref, kb_ref,
                        qb_ref, sel_ref):
    s_len = k_ref.shape[1]
    nb = s_len // MOBA_BLOCK
    n_items = tab_ref.shape[0] // 3
    npages = kp_ref.shape[1]
    hd = kp_ref.shape[3]
    page = kp_ref.shape[4]
    ppb = MOBA_BLOCK // page
    scale = hd ** -0.5

    kb_ref[...] = k_ref[0].astype(BF16)
    qb_ref[...] = (q_ref[0] * (scale * LOG2E)).astype(BF16)
    km = jnp.mean(k_ref[0].reshape(nb, MOBA_BLOCK, 2 * hd), axis=1)
    blk = lax.broadcasted_iota(jnp.int32, (nb, s_len), 0)
    own = lax.broadcasted_iota(jnp.int32, (nb, s_len), 1) // MOBA_BLOCK
    for hh in range(2):
        lo = hh * hd
        gate = _dot_nt(km[:, lo:lo + hd], q_ref[0, :, lo:lo + hd], precision=HIGHEST)
        cnt = jnp.zeros((nb, s_len), F32)
        for jp in range(nb - 1):
            gj = gate[jp:jp + 1, :]
            beats = ((gj > gate) | ((gj == gate) & (jp < blk))) & (jp < own)
            cnt = cnt + jnp.where(beats, 1.0, 0.0)
        picked = ((cnt < MOBA_TOPK) & (blk < own)) | (blk == own)
        limit_shift = jnp.where(picked, 0, -4 * MOBA_BLOCK)
        for qb in range(nb):
            sel_ref[hh, qb] = limit_shift[:, qb * MOBA_BLOCK:(qb + 1) * MOBA_BLOCK]
    for pg in range(npages):
        rows = slice(pg * page, (pg + 1) * page)
        kt = k_ref[0, rows, :].T
        vt = v_ref[0, rows, :].T
        kp_ref[0, pg] = kt.reshape(2, hd, page)
        vp_ref[0, pg] = vt.reshape(2, hd, page)
        cols = slice((pg % ppb) * page, (pg % ppb + 1) * page)
        for hh in range(2):
            vt_ref[pg // ppb, hh, 0:hd, cols] = vt[hh * hd:(hh + 1) * hd].astype(BF16)
    ones_row = jnp.where(lax.broadcasted_iota(jnp.int32, (ONES_ROWS, MOBA_BLOCK), 0) == 0, 1.0, 0.0)
    for j in range(nb):
        for hh in range(2):
            vt_ref[j, hh, hd:hd + ONES_ROWS, :] = ones_row.astype(BF16)

    row = lax.broadcasted_iota(jnp.int32, (MOBA_BLOCK, MOBA_BLOCK), 0)
    col_row = lax.broadcasted_iota(jnp.int32, (1, MOBA_BLOCK), 1)

    def scores(qi, j, hh):
        jc = jnp.minimum(j, qi)
        q_tile = qb_ref[pl.ds(pl.multiple_of(qi * MOBA_BLOCK, MOBA_BLOCK), MOBA_BLOCK),
                        hh * hd:(hh + 1) * hd]
        k_tile = kb_ref[pl.ds(pl.multiple_of(jc * MOBA_BLOCK, MOBA_BLOCK), MOBA_BLOCK),
                        hh * hd:(hh + 1) * hd]
        shift = jnp.where(j < qi, MOBA_BLOCK, jnp.where(j == qi, 0, -4 * MOBA_BLOCK))
        limit = col_row + (1 + shift) + sel_ref[hh, qi, pl.ds(jc, 1), :]
        return jnp.where(row < limit, _dot_nt(k_tile, q_tile), -jnp.inf)

    def weighted_values(qi, i, state):
        ja = jnp.minimum(2 * i, qi)
        jb = jnp.minimum(2 * i + 1, qi)
        return [alpha * acc + _dot(vt_ref[ja, hh], pa) + _dot(vt_ref[jb, hh], pb)
                for hh, (_, acc, alpha, pa, pb) in enumerate(state)]

    def write_block(qi, accs):
        o_ref[0, pl.ds(pl.multiple_of(qi * MOBA_BLOCK, MOBA_BLOCK), MOBA_BLOCK), :] = (
            jnp.concatenate([acc[0:hd] / acc[hd:hd + 1] for acc in accs], axis=0).T)

    def body(t, state):
        qi, i = tab_ref[t], tab_ref[n_items + t]
        tiles = [scores(qi, 2 * i + b, hh) for hh in range(2) for b in range(2)]
        prev = jnp.maximum(t - 1, 0)
        qi_prev = tab_ref[prev]
        accs = weighted_values(qi_prev, tab_ref[n_items + prev], state)
        fresh = (i == 0).astype(F32)
        new = []
        for hh in range(2):
            m = state[hh][0]
            m = m + fresh * (NEG - m)
            sa, sb = tiles[2 * hh], tiles[2 * hh + 1]
            m_new = jnp.maximum(m, jnp.max(jnp.maximum(sa, sb), axis=0, keepdims=True))
            alpha = jnp.exp2(m - m_new) * (1.0 - fresh)
            new.append((m_new, accs[hh], alpha, jnp.exp2(sa - m_new).astype(BF16),
                        jnp.exp2(sb - m_new).astype(BF16)))

        @pl.when((tab_ref[2 * n_items + prev] == 1) & (t > 0))
        def _():
            write_block(qi_prev, accs)

        return tuple(new)

    no_p = jnp.zeros((MOBA_BLOCK, MOBA_BLOCK), BF16)
    init = (jnp.full((1, MOBA_BLOCK), NEG, F32), jnp.zeros((hd + ONES_ROWS, MOBA_BLOCK), F32),
            jnp.zeros((1, MOBA_BLOCK), F32), no_p, no_p)
    state = lax.fori_loop(0, n_items, body, (init, init))
    write_block(nb - 1, weighted_values(nb - 1, (nb - 1) // 2, state))


def moba_prompt(h3, n_heads, hd, page):
    bsz, s_len, _ = h3.shape
    width = n_heads * hd
    lanes = 2 * hd
    pairs = n_heads // 2
    q0 = 3 * width // lanes
    nb = s_len // MOBA_BLOCK
    pages_shape = jax.ShapeDtypeStruct((bsz, s_len // page, n_heads, hd, page), F32)
    page_spec = pl.BlockSpec((1, s_len // page, 2, hd, page), lambda b, p, tab: (b, 0, p, 0, 0))
    items = jnp.asarray(sum(_moba_work_items(nb), []), jnp.int32)
    grid_spec = pltpu.PrefetchScalarGridSpec(
        num_scalar_prefetch=1,
        grid=(bsz, pairs),
        in_specs=[pl.BlockSpec((1, s_len, lanes), lambda b, p, tab: (b, 0, q0 + p)),
                  pl.BlockSpec((1, s_len, lanes), lambda b, p, tab: (b, 0, q0 + pairs + p)),
                  pl.BlockSpec((1, s_len, lanes), lambda b, p, tab: (b, 0, q0 + 2 * pairs + p))],
        out_specs=[pl.BlockSpec((1, s_len, lanes), lambda b, p, tab: (b, 0, p)),
                   page_spec, page_spec],
        scratch_shapes=[pltpu.VMEM((nb, 2, hd + ONES_ROWS, MOBA_BLOCK), BF16),
                        pltpu.VMEM((s_len, lanes), BF16),
                        pltpu.VMEM((s_len, lanes), BF16),
                        pltpu.VMEM((2, nb, nb, MOBA_BLOCK), jnp.int32)],
    )
    return pl.pallas_call(
        _moba_prompt_kernel,
        grid_spec=grid_spec,
        out_shape=[jax.ShapeDtypeStruct((bsz, s_len, width), F32), pages_shape, pages_shape],
        compiler_params=_params("parallel", "parallel"),
        name="moba_prompt",
    )(items, h3, h3, h3)


def _conv3_mix(gb, gcu, prev2, prev1, li, cw):
    s1 = jnp.where(li == 0, prev1[0], pltpu.roll(gcu, 1, 0))
    s2 = pltpu.roll(gcu, 2, 0)
    s2 = jnp.where(li == 0, prev2[0], jnp.where(li == 1, prev2[1], s2))
    conv = cw[0:1] * s2 + cw[1:2] * s1 + cw[2:3] * gcu
    return gb * conv


def _out_proj_residual(ya, yb, x, wo_ref, pw):
    half = ya.shape[1]
    y = _dot(ya.astype(BF16), wo_ref[0:half, :]) + _dot(yb.astype(BF16), wo_ref[half:, :])
    return x + _rms(y, pw)


def _l0_post_prompt_kernel(gb_ref, gc_ref, u_ref, gch_ref, uh_ref, yb_ref, x_ref, cw_ref, wo_ref,
                           pw_ref, xo_ref, st_ref, *, tiles_per_seq):
    i = pl.program_id(0)
    tm = gb_ref.shape[0]
    gcu = gc_ref[...] * u_ref[...]
    halo = gch_ref[...] * uh_ref[...]
    halo = halo * (i % tiles_per_seq != 0).astype(F32)
    li = lax.broadcasted_iota(jnp.int32, gcu.shape, 0)
    ya = _conv3_mix(gb_ref[...], gcu, (halo[6:7], halo[7:8]), (halo[7:8],), li, cw_ref[...])
    xo_ref[...] = _out_proj_residual(ya, yb_ref[...], x_ref[...], wo_ref, pw_ref[...])

    @pl.when(i % tiles_per_seq == tiles_per_seq - 1)
    def _():
        st_ref[0] = gcu[tm - 2:tm, :]


def l0_post_prompt(h2, yb, x, conv_w, wo, post_w, seq_len):
    m, d = x.shape
    width = yb.shape[1]
    tm = ROW_TILE
    tiles_per_seq = seq_len // tm
    halo_idx = lambda i: jnp.maximum(i * (tm // 8) - 1, 0)
    const = lambda i: (0, 0)
    return pl.pallas_call(
        functools.partial(_l0_post_prompt_kernel, tiles_per_seq=tiles_per_seq),
        grid=(m // tm,),
        in_specs=[pl.BlockSpec((tm, width), lambda i: (i, 0)),
                  pl.BlockSpec((tm, width), lambda i: (i, 1)),
                  pl.BlockSpec((tm, width), lambda i: (i, 2)),
                  pl.BlockSpec((8, width), lambda i: (halo_idx(i), 1)),
                  pl.BlockSpec((8, width), lambda i: (halo_idx(i), 2)),
                  pl.BlockSpec((tm, width), lambda i: (i, 0)),
                  pl.BlockSpec((tm, d), lambda i: (i, 0)),
                  pl.BlockSpec(conv_w.shape, const),
                  pl.BlockSpec(wo.shape, const),
                  pl.BlockSpec((1, d), const)],
        out_specs=[pl.BlockSpec((tm, d), lambda i: (i, 0)),
                   pl.BlockSpec((1, 2, width), lambda i: (i // tiles_per_seq, 0, 0))],
        out_shape=[jax.ShapeDtypeStruct((m, d), F32),
                   jax.ShapeDtypeStruct((m // seq_len, 2, width), F32)],
        compiler_params=_params("arbitrary"),
        name="l0_post_prompt",
    )(h2, h2, h2, h2, h2, yb, x, conv_w, wo, post_w.reshape(1, d))


def _l0_post_sample_kernel(gb_ref, gc_ref, u_ref, hist_ref, yb_ref, x_ref, cw_ref, wo_ref, pw_ref,
                           xo_ref, gcu_ref, *, seq_len):
    gcu = gc_ref[...] * u_ref[...]
    rows = gcu.shape[0]
    t = lax.broadcasted_iota(jnp.int32, gcu.shape, 0) % seq_len
    hist = hist_ref[...]
    hist_next = pltpu.roll(hist, rows - 1, 0)
    ya = _conv3_mix(gb_ref[...], gcu, (hist, hist), (hist_next,), t, cw_ref[...])
    xo_ref[...] = _out_proj_residual(ya, yb_ref[...], x_ref[...], wo_ref, pw_ref[...])
    gcu_ref[...] = gcu


def l0_post_sample(h2, yb, x, hist_rows, conv_w, wo, post_w, seq_len):
    m, d = x.shape
    width = yb.shape[1]
    const = lambda i: (0, 0)
    return pl.pallas_call(
        functools.partial(_l0_post_sample_kernel, seq_len=seq_len),
        grid=(1,),
        in_specs=[pl.BlockSpec((m, width), lambda i: (0, 0)),
                  pl.BlockSpec((m, width), lambda i: (0, 1)),
                  pl.BlockSpec((m, width), lambda i: (0, 2)),
                  pl.BlockSpec((m, width), const),
                  pl.BlockSpec((m, width), const),
                  pl.BlockSpec((m, d), const),
                  pl.BlockSpec(conv_w.shape, const),
                  pl.BlockSpec(wo.shape, const),
                  pl.BlockSpec((1, d), const)],
        out_specs=[pl.BlockSpec((m, d), const), pl.BlockSpec((m, width), const)],
        out_shape=[jax.ShapeDtypeStruct((m, d), F32), jax.ShapeDtypeStruct((m, width), F32)],
        compiler_params=_params("arbitrary"),
        name="l0_post_sample",
    )(h2, h2, h2, hist_rows, yb, x, conv_w, wo, post_w.reshape(1, d))


MEAN_BUFFERS = 16
BLOCKS_PER_TRIP = 2
ATTEND_SLOTS = 4


def _moba_select_kernel(pt_ref, qt_ref, ck_ref, sel_ref, buf, sem, km_ref, *, pages_per_block,
                        n_blk):
    bd, n_h, hd, t_len = qt_ref.shape
    page = ck_ref.shape[3]
    n_pages = n_blk * pages_per_block
    total = bd * n_pages

    def copy(i):
        slot = i % MEAN_BUFFERS
        return pltpu.make_async_copy(ck_ref.at[pt_ref[i]], buf.at[slot], sem.at[slot])

    for s in range(MEAN_BUFFERS - 1):
        copy(s).start()

    lane3 = lax.broadcasted_iota(jnp.int32, (n_h, hd, 128), 2)
    lane = lax.broadcasted_iota(jnp.int32, (n_h, 128), 1)
    sub = lax.broadcasted_iota(jnp.int32, (n_h, 128), 0)

    def per_seq(b, _):
        km_ref[...] = jnp.zeros(km_ref.shape, F32)

        def block_mean(n):
            acc = None
            for jj in range(pages_per_block):
                i = b * n_pages + n * pages_per_block + jj
                nxt = i + MEAN_BUFFERS - 1

                @pl.when(nxt < total)
                def _():
                    copy(nxt).start()

                copy(i).wait()
                acc = buf[i % MEAN_BUFFERS] if acc is None else acc + buf[i % MEAN_BUFFERS]
            return jnp.sum(acc, axis=-1, keepdims=True) * (1.0 / (pages_per_block * page))

        def per_group(g, _):
            means = [block_mean(g * BLOCKS_PER_TRIP + u) for u in range(BLOCKS_PER_TRIP)]
            km = km_ref[...]
            for u, mean in enumerate(means):
                km = jnp.where(lane3 == g * BLOCKS_PER_TRIP + u, mean, km)
            km_ref[...] = km
            return 0

        lax.fori_loop(0, n_blk // BLOCKS_PER_TRIP, per_group, 0)
        out = jnp.zeros((n_h, 128), jnp.int32)
        for t in range(t_len):
            gate = jnp.full((n_h, 128), -jnp.inf, F32)
            for h in range(n_h):
                g_h = jnp.sum(km_ref[h] * qt_ref[b, h][:, t:t + 1], axis=0, keepdims=True)
                gate = jnp.where(sub == h, g_h, gate)
            gate = jnp.where(lane < n_blk, gate, -jnp.inf)
            for r in range(MOBA_TOPK):
                best = jnp.max(gate, axis=-1, keepdims=True)
                idx = jnp.min(jnp.where(gate == best, lane, 128), axis=-1, keepdims=True)
                out = jnp.where(lane == t * MOBA_TOPK + r, idx, out)
                gate = jnp.where(lane == idx, -jnp.inf, gate)
        sel_ref[b] = out
        return 0

    lax.fori_loop(0, bd, per_seq, 0)


def moba_select(page_table, qt4, cache_kt):
    bd, n_pages = page_table.shape
    _, n_h, hd, t_len = qt4.shape
    page = cache_kt.shape[3]
    ppb = MOBA_BLOCK // page
    n_blk = n_pages // ppb
    assert n_blk <= 128 and t_len * MOBA_TOPK <= 128
    grid_spec = pltpu.PrefetchScalarGridSpec(
        num_scalar_prefetch=1,
        grid=(1,),
        in_specs=[pl.BlockSpec(qt4.shape, lambda i, pt: (0, 0, 0, 0)),
                  pl.BlockSpec(memory_space=pl.ANY)],
        out_specs=pl.BlockSpec((bd, n_h, 128), lambda i, pt: (0, 0, 0)),
        scratch_shapes=[pltpu.VMEM((MEAN_BUFFERS, n_h, hd, page), F32),
                        pltpu.SemaphoreType.DMA((MEAN_BUFFERS,)),
                        pltpu.VMEM((n_h, hd, 128), F32)],
    )
    return pl.pallas_call(
        functools.partial(_moba_select_kernel, pages_per_block=ppb, n_blk=n_blk),
        grid_spec=grid_spec,
        out_shape=jax.ShapeDtypeStruct((bd, n_h, 128), jnp.int32),
        compiler_params=_params("arbitrary"),
        name="moba_select",
    )(page_table.reshape(-1), qt4, cache_kt)


def _moba_attend_kernel(pt_ref, sel_ref, qt_ref, knt_ref, vnt_ref, ck_ref, cv_ref, o_ref,
                        kbuf, vbuf, sem, *, pages_per_block, n_pages):
    bd, n_h, hd, t_len = qt_ref.shape
    page = ck_ref.shape[3]
    per_t = MOBA_TOPK * pages_per_block
    scale = hd ** -0.5
    step = lax.broadcasted_iota(jnp.int32, (1, t_len), 1)
    step_col = lax.broadcasted_iota(jnp.int32, (hd, t_len), 1)

    def copies(pair, slot):
        b = pair // n_h
        h = pair % n_h
        out = []
        for t in range(t_len):
            for r in range(MOBA_TOPK):
                blk = sel_ref[(b * n_h + h) * (t_len * MOBA_TOPK) + t * MOBA_TOPK + r]
                for jj in range(pages_per_block):
                    phys = pt_ref[b * n_pages + blk * pages_per_block + jj]
                    s = t * per_t + r * pages_per_block + jj
                    out.append(pltpu.make_async_copy(ck_ref.at[phys, h], kbuf.at[slot, s],
                                                     sem.at[slot]))
                    out.append(pltpu.make_async_copy(cv_ref.at[phys, h], vbuf.at[slot, s],
                                                     sem.at[slot]))
        return out

    for ahead in range(ATTEND_SLOTS - 1):
        for c in copies(ahead, ahead):
            c.start()

    def per_pair(pair, _):
        slot = pair % ATTEND_SLOTS
        b = pair // n_h
        h = pair % n_h
        nxt = pair + ATTEND_SLOTS - 1

        @pl.when(nxt < bd * n_h)
        def _():
            for c in copies(nxt, nxt % ATTEND_SLOTS):
                c.start()

        for c in copies(pair, slot):
            c.wait()

        q_all = qt_ref[b, h] * scale
        k_new = knt_ref[b, h]
        v_new = vnt_ref[b, h]
        out = jnp.zeros((hd, t_len), F32)
        for t in range(t_len):
            qc = q_all[:, t:t + 1]
            s_sel = [jnp.sum(kbuf[slot, t * per_t + s] * qc, axis=0, keepdims=True)
                     for s in range(per_t)]
            s_own = jnp.where(step <= t, jnp.sum(k_new * qc, axis=0, keepdims=True), NEG)
            m = jnp.max(s_own, axis=-1, keepdims=True)
            for s in s_sel:
                m = jnp.maximum(m, jnp.max(s, axis=-1, keepdims=True))
            p_own = jnp.exp(s_own - m)
            l = jnp.sum(p_own, axis=-1, keepdims=True)
            acc_own = jnp.sum(v_new * p_own, axis=-1, keepdims=True)
            acc = jnp.zeros((hd, page), F32)
            for si, s in enumerate(s_sel):
                p = jnp.exp(s - m)
                l = l + jnp.sum(p, axis=-1, keepdims=True)
                acc = acc + vbuf[slot, t * per_t + si] * p
            o_t = (acc_own + jnp.sum(acc, axis=-1, keepdims=True)) / l
            out = jnp.where(step_col == t, o_t, out)
        o_ref[b, h] = out
        return 0

    lax.fori_loop(0, bd * n_h, per_pair, 0)


def moba_attend(page_table, sel, qt4, knt4, vnt4, cache_kt, cache_vt):
    bd, n_pages = page_table.shape
    _, n_h, hd, t_len = qt4.shape
    page = cache_kt.shape[3]
    ppb = MOBA_BLOCK // page
    n_slab = t_len * MOBA_TOPK * ppb
    full = pl.BlockSpec(qt4.shape, lambda i, pt, sl: (0, 0, 0, 0))
    grid_spec = pltpu.PrefetchScalarGridSpec(
        num_scalar_prefetch=2,
        grid=(1,),
        in_specs=[full, full, full, pl.BlockSpec(memory_space=pl.ANY),
                  pl.BlockSpec(memory_space=pl.ANY)],
        out_specs=full,
        scratch_shapes=[pltpu.VMEM((ATTEND_SLOTS, n_slab, hd, page), F32),
                        pltpu.VMEM((ATTEND_SLOTS, n_slab, hd, page), F32),
                        pltpu.SemaphoreType.DMA((ATTEND_SLOTS,))],
    )
    return pl.pallas_call(
        functools.partial(_moba_attend_kernel, pages_per_block=ppb, n_pages=n_pages),
        grid_spec=grid_spec,
        out_shape=jax.ShapeDtypeStruct(qt4.shape, F32),
        compiler_params=_params("arbitrary"),
        name="moba_attend",
    )(page_table.reshape(-1), sel.reshape(-1), qt4, knt4, vnt4, cache_kt, cache_vt)


def _rows8(x):
    return x.reshape(x.shape[0] // 8, 8, x.shape[1])


def _softplus(x):
    return jnp.maximum(x, 0.0) + jnp.log1p(jnp.exp(-jnp.abs(x)))


def _ssd_kernel(*refs, d_inner, rows_valid, has_halo):
    if has_halo:
        (zx_ref, halo_ref, dt_ref, s0_ref, cw_ref, cb_ref, dtb_ref, alog_ref, dskip_ref, e_ref,
         y_ref, cs_ref, fs_ref, st_ref, xin_ref, dtin_ref, y_scr) = refs
        hist_ref = None
    else:
        (zx_ref, hist_ref, dt_ref, s0_ref, cw_ref, cb_ref, dtb_ref, alog_ref, dskip_ref, e_ref,
         y_ref, cs_ref, fs_ref, st_ref, xin_ref, dtin_ref, y_scr) = refs
        halo_ref = None
    c = pl.program_id(1)
    n_chunks = pl.num_programs(1)
    cl = SSM_CHUNK
    rows_blk = zx_ref.shape[1]
    conv_dim = cw_ref.shape[2]
    n_st = SSM_STATE
    hp = SSM_HEAD_DIM
    heads_per_group = d_inner // hp // SSM_GROUPS
    gw = heads_per_group * hp

    @pl.when(c == 0)
    def _():
        st_ref[...] = s0_ref[0].T

    if has_halo:
        xin_ref[0:8, :] = halo_ref[0, :, d_inner:d_inner + conv_dim] * (c != 0).astype(F32)
    else:
        xin_ref[0:8, :] = jnp.zeros((8, conv_dim), F32)
        xin_ref[5:8, :] = hist_ref[0]
    if rows_blk == cl:
        xin_ref[8:8 + cl, :] = zx_ref[0, :, d_inner:d_inner + conv_dim]
        dt_raw = dt_ref[0]
    else:
        xin_ref[8:8 + cl, :] = jnp.zeros((cl, conv_dim), F32)
        dtin_ref[...] = jnp.zeros(dtin_ref.shape, F32)
        xin_ref[8:8 + rows_blk, :] = zx_ref[0, :, d_inner:d_inner + conv_dim]
        dtin_ref[0:rows_blk, :] = dt_ref[0]
        dt_raw = dtin_ref[...]

    xbc = xin_ref[8:8 + cl, :]
    taps = [_rows8(xin_ref[5 + j:5 + j + cl, :]) * cw_ref[j][None] for j in range(3)]
    conv = taps[0] + taps[1] + taps[2] + _rows8(xbc) * cw_ref[3][None] + cb_ref[...][None]
    conv = conv.reshape(cl, conv_dim)
    act = conv * jax.nn.sigmoid(conv)

    @pl.when(c == n_chunks - 1)
    def _():
        cs_ref[0] = xbc[rows_valid - 3:rows_valid, :]

    dt = _softplus(dt_raw + dtb_ref[...])
    if rows_valid < cl:
        dt = jnp.where(lax.broadcasted_iota(jnp.int32, dt.shape, 0) < rows_valid, dt, 0.0)
    a = -jnp.exp(alog_ref[...])
    dta = dt * a
    r_i = lax.broadcasted_iota(jnp.int32, (cl, cl), 0)
    c_i = lax.broadcasted_iota(jnp.int32, (cl, cl), 1)
    causal = c_i <= r_i
    first_head_lanes = lax.broadcasted_iota(jnp.int32, (cl, 2 * hp), 1) < hp
    tril = jnp.where(causal, 1.0, 0.0)
    acum = jnp.dot(tril, dta, preferred_element_type=F32, precision=HIGHEST)
    acum_t = acum.T
    alast = acum[cl - 1:cl, :]
    narrow = jnp.concatenate([dt, jnp.exp(acum), jnp.exp(alast - acum),
                              jnp.broadcast_to(jnp.exp(alast), (8, alast.shape[1]))], axis=0)
    hi = narrow.astype(BF16)
    mid = (narrow - hi.astype(F32)).astype(BF16)
    wide = _dot(jnp.concatenate([hi, mid], axis=1), e_ref[...])
    dt_w = wide[0:cl]
    eac_w = wide[cl:2 * cl]
    dte_w = wide[2 * cl:3 * cl]
    cd_w = wide[3 * cl:3 * cl + 8]

    xs = act[:, 0:d_inner]
    xdt = xs * dt_w
    xdt_b = xdt.astype(BF16)
    xdtd_b = (xdt * dte_w).astype(BF16)
    for g in range(SSM_GROUPS):
        b_g = act[:, d_inner + g * n_st:d_inner + (g + 1) * n_st]
        c_g = act[:, d_inner + (SSM_GROUPS + g) * n_st:d_inner + (SSM_GROUPS + g + 1) * n_st]
        c_gb = c_g.astype(BF16)
        cb = _dot_nt(c_gb, b_g.astype(BF16))
        s_g = st_ref[:, g * gw:(g + 1) * gw]
        y_off = _dot(c_gb, s_g.astype(BF16)) * eac_w[:, g * gw:(g + 1) * gw]
        for r in range(0, heads_per_group, 2):
            h = g * heads_per_group + r
            stacked = []
            for hh in (h, h + 1):
                seg = acum[:, hh:hh + 1] - acum_t[hh:hh + 1, :]
                decay = jnp.exp(jnp.where(causal, seg, NEG))
                stacked.append((cb * decay).astype(BF16))
            both = _dot(jnp.concatenate(stacked, axis=0), xdt_b[:, h * hp:(h + 2) * hp])
            y_d = jnp.where(first_head_lanes, both[0:cl], both[cl:2 * cl])
            y_scr[:, h * hp:(h + 2) * hp] = y_d + y_off[:, r * hp:(r + 2) * hp]
        kept = (_rows8(s_g) * cd_w[:, g * gw:(g + 1) * gw][None]).reshape(n_st, gw)
        st_ref[:, g * gw:(g + 1) * gw] = kept + _dot(b_g.T.astype(BF16),
                                                     xdtd_b[:, g * gw:(g + 1) * gw])
    y = y_scr[...] + (_rows8(xs) * dskip_ref[...][None]).reshape(cl, d_inner)
    y_ref[0] = y[0:rows_blk, :]

    @pl.when(c == n_chunks - 1)
    def _():
        fs_ref[0] = st_ref[...].T


def ssd_mixer(zx3, dt3, hist, state0, conv_w, conv_b, dt_bias, a_log, d_skip, d_inner):
    nseq, t_len, _ = zx3.shape
    conv_dim = conv_w.shape[1]
    n_heads = d_inner // SSM_HEAD_DIM
    cl = SSM_CHUNK
    has_halo = hist is None
    if has_halo:
        rows_blk, rows_valid, n_chunks = cl, cl, t_len // cl
    else:
        rows_blk, rows_valid, n_chunks = t_len, t_len, 1
    pad = lambda v: jnp.pad(v.astype(F32), (0, 128 - n_heads)).reshape(1, 128)
    expand = (jnp.arange(256)[:, None] % 128
              == (jnp.arange(d_inner)[None, :] // SSM_HEAD_DIM)).astype(BF16)
    rows8 = lambda v: jnp.broadcast_to(v[..., None, :], v.shape[:-1] + (8, v.shape[-1]))
    d_wide = rows8(jnp.repeat(d_skip.astype(F32), SSM_HEAD_DIM))
    const2 = lambda s, c: (0, 0)
    if has_halo:
        second = pl.BlockSpec((1, 8, zx3.shape[2]),
                              lambda s, c: (s, jnp.maximum(c * (cl // 8) - 1, 0), 0))
        second_arg = zx3
    else:
        second = pl.BlockSpec((1,) + hist.shape[1:], lambda s, c: (s, 0, 0))
        second_arg = hist
    outs = pl.pallas_call(
        functools.partial(_ssd_kernel, d_inner=d_inner, rows_valid=rows_valid, has_halo=has_halo),
        grid=(nseq, n_chunks),
        in_specs=[pl.BlockSpec((1, rows_blk, zx3.shape[2]), lambda s, c: (s, c, 0)),
                  second,
                  pl.BlockSpec((1, rows_blk, 128), lambda s, c: (s, c, 0)),
                  pl.BlockSpec((1, d_inner, SSM_STATE), lambda s, c: (s, 0, 0)),
                  pl.BlockSpec((conv_w.shape[0], 8, conv_dim), lambda s, c: (0, 0, 0)),
                  pl.BlockSpec((8, conv_dim), const2),
                  pl.BlockSpec((1, 128), const2),
                  pl.BlockSpec((1, 128), const2),
                  pl.BlockSpec((8, d_inner), const2),
                  pl.BlockSpec((256, d_inner), const2)],
        out_specs=[pl.BlockSpec((1, rows_blk, d_inner), lambda s, c: (s, c, 0)),
                   pl.BlockSpec((1, 3, conv_dim), lambda s, c: (s, 0, 0)),
                   pl.BlockSpec((1, d_inner, SSM_STATE), lambda s, c: (s, 0, 0))],
        out_shape=[jax.ShapeDtypeStruct((nseq, t_len, d_inner), F32),
                   jax.ShapeDtypeStruct((nseq, 3, conv_dim), F32),
                   jax.ShapeDtypeStruct((nseq, d_inner, SSM_STATE), F32)],
        scratch_shapes=[pltpu.VMEM((SSM_STATE, d_inner), F32),
                        pltpu.VMEM((cl + 8, conv_dim), F32),
                        pltpu.VMEM((cl, 128), F32),
                        pltpu.VMEM((cl, d_inner), F32)],
        compiler_params=_params("parallel", "arbitrary"),
        name="ssd_mixer",
    )(zx3, second_arg, dt3, state0, rows8(conv_w), rows8(conv_b), pad(dt_bias), pad(a_log),
      d_wide, expand)
    return outs


def _l1_post_kernel(y_ref, z_ref, x_ref, gw_ref, wo_ref, pw_ref, o_ref):
    z = z_ref[...]
    g = y_ref[...] * (z * jax.nn.sigmoid(z))
    o = _dot(_rms(g, gw_ref[...]).astype(BF16), wo_ref[...])
    o_ref[...] = x_ref[...] + _rms(o, pw_ref[...])


def l1_post(y, zx, x, gate_w, wo, post_w):
    m, d = x.shape
    d_inner = y.shape[1]
    tm = _row_tile(m)
    const = lambda i: (0, 0)
    return pl.pallas_call(
        _l1_post_kernel,
        grid=(m // tm,),
        in_specs=[pl.BlockSpec((tm, d_inner), lambda i: (i, 0)),
                  pl.BlockSpec((tm, d_inner), lambda i: (i, 0)),
                  pl.BlockSpec((tm, d), lambda i: (i, 0)),
                  pl.BlockSpec((1, d_inner), const),
                  pl.BlockSpec(wo.shape, const),
                  pl.BlockSpec((1, d), const)],
        out_specs=pl.BlockSpec((tm, d), lambda i: (i, 0)),
        out_shape=jax.ShapeDtypeStruct((m, d), F32),
        compiler_params=_params("parallel"),
        name="l1_post",
    )(y, zx, x, gate_w.reshape(1, d_inner), wo, post_w.reshape(1, d))


def kernel(x_prompt, x_sample, cache_k, cache_v, page_table, state_conv_a, state_conv_ssm, state_ssm, l0_norm_mix_pre, l0_w_in, l0_conv_w, l0_w_out, l0_norm_mix_post, l0_norm_ffn_pre, l0_ffn_gate, l0_ffn_up, l0_ffn_down, l0_norm_ffn_post, l1_norm_mix_pre, l1_w_in, l1_conv_w, l1_conv_b, l1_dt_bias, l1_a_log, l1_d_skip, l1_norm_gate, l1_w_out, l1_norm_mix_post, l1_norm_ffn_pre, l1_ffn_gate, l1_ffn_up, l1_ffn_down, l1_norm_ffn_post):
    bp, s_len, d = x_prompt.shape
    bd, t_len, _ = x_sample.shape
    n_heads, page, hd = cache_k.shape[1:]
    width = n_heads * hd
    assert (page_table.shape[1] * page) % MOBA_BLOCK == 0
    conv_dim = l1_conv_w.shape[1]
    ssm_heads = l1_dt_bias.shape[0]
    d_inner = ssm_heads * SSM_HEAD_DIM

    bf = lambda w: w.astype(BF16)
    xp = x_prompt.reshape(bp * s_len, d)
    xs = x_sample.reshape(bd * t_len, d)

    w_in0 = bf(l0_w_in)
    w_out0 = bf(l0_w_out)
    hp_ = norm_matmul(xp, l0_norm_mix_pre, w_in0)
    hs_ = norm_matmul(xs, l0_norm_mix_pre, w_in0)

    yb_p, kp_t, vp_t = moba_prompt(hp_.reshape(bp, s_len, -1), n_heads, hd, page)
    k_prompt = jnp.swapaxes(kp_t, 3, 4)
    v_prompt = jnp.swapaxes(vp_t, 3, 4)
    xp, conv_a_prompt = l0_post_prompt(hp_, yb_p.reshape(bp * s_len, width), xp, l0_conv_w, w_out0,
                                       l0_norm_mix_post, s_len)

    qkv_t = hs_[:, 3 * width:].reshape(bd, t_len, 3, n_heads, hd).transpose(2, 0, 3, 4, 1)
    qt4, knt4, vnt4 = qkv_t[0], qkv_t[1], qkv_t[2]
    cache_kt = jnp.swapaxes(cache_k, 2, 3)
    cache_vt = jnp.swapaxes(cache_v, 2, 3)
    sel = moba_select(page_table, qt4, cache_kt)[:, :, :t_len * MOBA_TOPK]
    yb_s = moba_attend(page_table, sel, qt4, knt4, vnt4, cache_kt, cache_vt)
    yb_s = yb_s.transpose(0, 3, 1, 2).reshape(bd * t_len, width)
    hist_rows = jnp.pad(state_conv_a, ((0, 0), (0, t_len - state_conv_a.shape[1]), (0, 0)))
    xs, gcu_s = l0_post_sample(hs_, yb_s, xs, hist_rows.reshape(bd * t_len, width), l0_conv_w,
                               w_out0, l0_norm_mix_post, t_len)
    conv_a_sample = gcu_s.reshape(bd, t_len, width)[:, t_len - 2:]
    k_sample = knt4.transpose(0, 1, 3, 2)
    v_sample = vnt4.transpose(0, 1, 3, 2)

    wg0, wu0, wd0 = bf(l0_ffn_gate), bf(l0_ffn_up), bf(l0_ffn_down)
    xp = ffn(xp, l0_norm_ffn_pre, wg0, wu0, wd0, l0_norm_ffn_post)
    xs = ffn(xs, l0_norm_ffn_pre, wg0, wu0, wd0, l0_norm_ffn_post)

    w_zx = bf(l1_w_in[:, :d_inner + conv_dim])
    w_dt = jnp.pad(l1_w_in[:, d_inner + conv_dim:], ((0, 0), (0, 128 - ssm_heads)))
    w_out1 = bf(l1_w_out)
    zx_p, dt_p = norm_matmul(xp, l1_norm_mix_pre, w_zx, w_dt)
    zx_s, dt_s = norm_matmul(xs, l1_norm_mix_pre, w_zx, w_dt)

    zeros_state = jnp.zeros((bp, d_inner, SSM_STATE), F32)
    y_p, conv_ssm_prompt, fs_p = ssd_mixer(zx_p.reshape(bp, s_len, -1), dt_p.reshape(bp, s_len, 128),
                                           None, zeros_state, l1_conv_w, l1_conv_b, l1_dt_bias,
                                           l1_a_log, l1_d_skip, d_inner)
    y_s, conv_ssm_sample, fs_s = ssd_mixer(zx_s.reshape(bd, t_len, -1), dt_s.reshape(bd, t_len, 128),
                                           state_conv_ssm, state_ssm.reshape(bd, d_inner, SSM_STATE),
                                           l1_conv_w, l1_conv_b, l1_dt_bias, l1_a_log, l1_d_skip,
                                           d_inner)
    ssm_prompt = fs_p.reshape(bp, ssm_heads, SSM_HEAD_DIM, SSM_STATE)
    ssm_sample = fs_s.reshape(bd, ssm_heads, SSM_HEAD_DIM, SSM_STATE)

    xp = l1_post(y_p.reshape(bp * s_len, d_inner), zx_p, xp, l1_norm_gate, w_out1, l1_norm_mix_post)
    xs = l1_post(y_s.reshape(bd * t_len, d_inner), zx_s, xs, l1_norm_gate, w_out1, l1_norm_mix_post)

    wg1, wu1, wd1 = bf(l1_ffn_gate), bf(l1_ffn_up), bf(l1_ffn_down)
    xp = ffn(xp, l1_norm_ffn_pre, wg1, wu1, wd1, l1_norm_ffn_post)
    xs = ffn(xs, l1_norm_ffn_pre, wg1, wu1, wd1, l1_norm_ffn_post)

    return (xp.reshape(bp, s_len, d), xs.reshape(bd, t_len, d), k_prompt, v_prompt, k_sample,
            v_sample, conv_a_prompt, conv_a_sample, conv_ssm_prompt, conv_ssm_sample, ssm_prompt,
            ssm_sample)
```

```python
import functools

import jax
import jax.numpy as jnp
from jax import lax
from jax.experimental import pallas as pl
from jax.experimental.pallas import tpu as pltpu

F32 = jnp.float32
BF16 = jnp.bfloat16
HIGHEST = lax.Precision.HIGHEST

NORM_EPS = 1e-6
MOBA_BLOCK = 256
MOBA_TOPK = 3
SSM_CHUNK = 128
SSM_HEAD_DIM = 64
SSM_STATE = 128
SSM_GROUPS = 4
NEG = -1e30
LOG2E = 1.4426950408889634
ONES_ROWS = 16

VMEM_LIMIT_BYTES = 56 * 1024 * 1024
ROW_TILE = 512


def _params(*sem):
    return pltpu.CompilerParams(dimension_semantics=sem, vmem_limit_bytes=VMEM_LIMIT_BYTES)


def _rms(x, w):
    return x * lax.rsqrt(jnp.mean(x * x, axis=-1, keepdims=True) + NORM_EPS) * w


def _dot(a, b):
    return jnp.dot(a, b, preferred_element_type=F32)


def _dot_nt(a, b, precision=None):
    return lax.dot_general(a, b, (((1,), (1,)), ((), ())), preferred_element_type=F32,
                           precision=precision)


def _row_tile(m, tile=ROW_TILE):
    return tile if m % tile == 0 else m


PROJ_ROW_TILE = 512
PROJ_COL_CHUNK = 1024


def _norm_matmul_kernel(*refs, narrow):
    if narrow:
        x_ref, nw_ref, w_ref, whi_ref, wlo_ref, o_ref, o2_ref = refs
    else:
        x_ref, nw_ref, w_ref, o_ref = refs
    xn = _rms(x_ref[...], nw_ref[...])
    xh = xn.astype(BF16)
    n = w_ref.shape[1]
    for c0 in range(0, n, PROJ_COL_CHUNK):
        c1 = min(c0 + PROJ_COL_CHUNK, n)
        o_ref[:, c0:c1] = _dot(xh, w_ref[:, c0:c1])
    if narrow:
        xl = (xn - xh.astype(F32)).astype(BF16)
        o2_ref[...] = _dot(xh, whi_ref[...]) + _dot(xl, whi_ref[...]) + _dot(xh, wlo_ref[...])


def norm_matmul(x, nw, w_bf16, w_narrow=None):
    m, d = x.shape
    n = w_bf16.shape[1]
    tm = PROJ_ROW_TILE if m % PROJ_ROW_TILE == 0 else m
    const = lambda i: (0, 0)
    resident = lambda shape: pl.BlockSpec(shape, const, pipeline_mode=pl.Buffered(1))
    in_specs = [pl.BlockSpec((tm, d), lambda i: (i, 0)), pl.BlockSpec((1, d), const),
                resident((d, n))]
    out_specs = [pl.BlockSpec((tm, n), lambda i: (i, 0))]
    out_shape = [jax.ShapeDtypeStruct((m, n), F32)]
    args = [x, nw.reshape(1, d), w_bf16]
    if w_narrow is not None:
        n2 = w_narrow.shape[1]
        whi = w_narrow.astype(BF16)
        wlo = (w_narrow - whi.astype(F32)).astype(BF16)
        in_specs += [resident((d, n2)), resident((d, n2))]
        out_specs.append(pl.BlockSpec((tm, n2), lambda i: (i, 0)))
        out_shape.append(jax.ShapeDtypeStruct((m, n2), F32))
        args += [whi, wlo]
    outs = pl.pallas_call(
        functools.partial(_norm_matmul_kernel, narrow=w_narrow is not None),
        grid=(m // tm,),
        in_specs=in_specs,
        out_specs=out_specs,
        out_shape=out_shape,
        compiler_params=_params("parallel"),
        name="norm_matmul",
    )(*args)
    return outs if w_narrow is not None else outs[0]


FFN_CHUNK = 256
FFN_ROW_TILE = 1024


def _ffn_apply(x, pre_ref, wg_ref, wu_ref, wd_ref, post_ref):
    h = _rms(x, pre_ref[...]).astype(BF16)
    hidden = wg_ref.shape[1]
    acc = jnp.zeros(x.shape, F32)
    for c in range(hidden // FFN_CHUNK):
        sl = slice(c * FFN_CHUNK, (c + 1) * FFN_CHUNK)
        g = _dot(h, wg_ref[:, sl])
        u = _dot(h, wu_ref[:, sl])
        a = (g * jax.nn.sigmoid(g) * u).astype(BF16)
        acc = acc + _dot(a, wd_ref[sl, :])
    return x + _rms(acc, post_ref[...])


def _ffn_kernel(x_ref, pre_ref, wg_ref, wu_ref, wd_ref, post_ref, o_ref):
    o_ref[...] = _ffn_apply(x_ref[...], pre_ref, wg_ref, wu_ref, wd_ref, post_ref)


def _ffn_operands(ffn_w, d):
    pre_w, wg, wu, wd, post_w = ffn_w
    const = lambda i: (0, 0)
    resident = lambda w: pl.BlockSpec(w.shape, const, pipeline_mode=pl.Buffered(1))
    args = [pre_w.reshape(1, d), wg, wu, wd, post_w.reshape(1, d)]
    specs = [pl.BlockSpec((1, d), const), resident(wg), resident(wu), resident(wd),
             pl.BlockSpec((1, d), const)]
    return args, specs


def ffn(x, ffn_w):
    m, d = x.shape
    tm = _row_tile(m, FFN_ROW_TILE)
    ffn_args, ffn_specs = _ffn_operands(ffn_w, d)
    return pl.pallas_call(
        _ffn_kernel,
        grid=(m // tm,),
        in_specs=[pl.BlockSpec((tm, d), lambda i: (i, 0))] + ffn_specs,
        out_specs=pl.BlockSpec((tm, d), lambda i: (i, 0)),
        out_shape=jax.ShapeDtypeStruct((m, d), F32),
        compiler_params=_params("parallel"),
        name="ffn",
    )(x, *ffn_args)


def _moba_work_items(nb):
    items = [(qi, i, int(i == qi // 2)) for qi in range(nb) for i in range(qi // 2 + 1)]
    return [list(col) for col in zip(*items)]


def _moba_prompt_kernel(tab_ref, q_ref, k_ref, v_ref, o_ref, kp_ref, vp_ref, vt_ref, kb_ref,
                        qb_ref, sel_ref):
    s_len = k_ref.shape[1]
    nb = s_len // MOBA_BLOCK
    n_items = tab_ref.shape[0] // 3
    npages = kp_ref.shape[1]
    hd = kp_ref.shape[3]
    page = kp_ref.shape[4]
    ppb = MOBA_BLOCK // page
    scale = hd ** -0.5

    kb_ref[...] = k_ref[0].astype(BF16)
    qb_ref[...] = (q_ref[0] * (scale * LOG2E)).astype(BF16)
    km = jnp.mean(k_ref[0].reshape(nb, MOBA_BLOCK, 2 * hd), axis=1)
    blk = lax.broadcasted_iota(jnp.int32, (nb, s_len), 0)
    own = lax.broadcasted_iota(jnp.int32, (nb, s_len), 1) // MOBA_BLOCK
    for hh in range(2):
        lo = hh * hd
        gate = _dot_nt(km[:, lo:lo + hd], q_ref[0, :, lo:lo + hd], precision=HIGHEST)
        cnt = jnp.zeros((nb, s_len), F32)
        for jp in range(nb - 1):
            gj = gate[jp:jp + 1, :]
            beats = ((gj > gate) | ((gj == gate) & (jp < blk))) & (jp < own)
            cnt = cnt + jnp.where(beats, 1.0, 0.0)
        picked = ((cnt < MOBA_TOPK) & (blk < own)) | (blk == own)
        limit_shift = jnp.where(picked, 0, -4 * MOBA_BLOCK)
        for qb in range(nb):
            sel_ref[hh, qb] = limit_shift[:, qb * MOBA_BLOCK:(qb + 1) * MOBA_BLOCK]
    for pg in range(npages):
        rows = slice(pg * page, (pg + 1) * page)
        kt = k_ref[0, rows, :].T
        vt = v_ref[0, rows, :].T
        kp_ref[0, pg] = kt.reshape(2, hd, page)
        vp_ref[0, pg] = vt.reshape(2, hd, page)
        cols = slice((pg % ppb) * page, (pg % ppb + 1) * page)
        for hh in range(2):
            vt_ref[pg // ppb, hh, 0:hd, cols] = vt[hh * hd:(hh + 1) * hd].astype(BF16)
    ones_row = jnp.where(lax.broadcasted_iota(jnp.int32, (ONES_ROWS, MOBA_BLOCK), 0) == 0, 1.0, 0.0)
    for j in range(nb):
        for hh in range(2):
            vt_ref[j, hh, hd:hd + ONES_ROWS, :] = ones_row.astype(BF16)

    row = lax.broadcasted_iota(jnp.int32, (MOBA_BLOCK, MOBA_BLOCK), 0)
    col_row = lax.broadcasted_iota(jnp.int32, (1, MOBA_BLOCK), 1)

    def scores(qi, j, hh):
        jc = jnp.minimum(j, qi)
        q_tile = qb_ref[pl.ds(pl.multiple_of(qi * MOBA_BLOCK, MOBA_BLOCK), MOBA_BLOCK),
                        hh * hd:(hh + 1) * hd]
        k_tile = kb_ref[pl.ds(pl.multiple_of(jc * MOBA_BLOCK, MOBA_BLOCK), MOBA_BLOCK),
                        hh * hd:(hh + 1) * hd]
        shift = jnp.where(j < qi, MOBA_BLOCK, jnp.where(j == qi, 0, -4 * MOBA_BLOCK))
        limit = col_row + (1 + shift) + sel_ref[hh, qi, pl.ds(jc, 1), :]
        return jnp.where(row < limit, _dot_nt(k_tile, q_tile), -jnp.inf)

    def weighted_values(qi, i, state):
        ja = jnp.minimum(2 * i, qi)
        jb = jnp.minimum(2 * i + 1, qi)
        return [alpha * acc + _dot(vt_ref[ja, hh], pa) + _dot(vt_ref[jb, hh], pb)
                for hh, (_, acc, alpha, pa, pb) in enumerate(state)]

    def write_block(qi, accs):
        o_ref[0, pl.ds(pl.multiple_of(qi * MOBA_BLOCK, MOBA_BLOCK), MOBA_BLOCK), :] = (
            jnp.concatenate([acc[0:hd] / acc[hd:hd + 1] for acc in accs], axis=0).T)

    def body(t, state):
        qi, i = tab_ref[t], tab_ref[n_items + t]
        tiles = [scores(qi, 2 * i + b, hh) for hh in range(2) for b in range(2)]
        prev = jnp.maximum(t - 1, 0)
        qi_prev = tab_ref[prev]
        accs = weighted_values(qi_prev, tab_ref[n_items + prev], state)
        fresh = (i == 0).astype(F32)
        new = []
        for hh in range(2):
            m = state[hh][0]
            m = m + fresh * (NEG - m)
            sa, sb = tiles[2 * hh], tiles[2 * hh + 1]
            m_new = jnp.maximum(m, jnp.max(jnp.maximum(sa, sb), axis=0, keepdims=True))
            alpha = jnp.exp2(m - m_new) * (1.0 - fresh)
            new.append((m_new, accs[hh], alpha, jnp.exp2(sa - m_new).astype(BF16),
                        jnp.exp2(sb - m_new).astype(BF16)))

        @pl.when((tab_ref[2 * n_items + prev] == 1) & (t > 0))
        def _():
            write_block(qi_prev, accs)

        return tuple(new)

    no_p = jnp.zeros((MOBA_BLOCK, MOBA_BLOCK), BF16)
    init = (jnp.full((1, MOBA_BLOCK), NEG, F32), jnp.zeros((hd + ONES_ROWS, MOBA_BLOCK), F32),
            jnp.zeros((1, MOBA_BLOCK), F32), no_p, no_p)
    state = lax.fori_loop(0, n_items, body, (init, init))
    write_block(nb - 1, weighted_values(nb - 1, (nb - 1) // 2, state))


def moba_prompt(h3, n_heads, hd, page):
    bsz, s_len, _ = h3.shape
    width = n_heads * hd
    lanes = 2 * hd
    pairs = n_heads // 2
    q0 = 3 * width // lanes
    nb = s_len // MOBA_BLOCK
    pages_shape = jax.ShapeDtypeStruct((bsz, s_len // page, n_heads, hd, page), F32)
    page_spec = pl.BlockSpec((1, s_len // page, 2, hd, page), lambda b, p, tab: (b, 0, p, 0, 0))
    items = jnp.asarray(sum(_moba_work_items(nb), []), jnp.int32)
    grid_spec = pltpu.PrefetchScalarGridSpec(
        num_scalar_prefetch=1,
        grid=(bsz, pairs),
        in_specs=[pl.BlockSpec((1, s_len, lanes), lambda b, p, tab: (b, 0, q0 + p)),
                  pl.BlockSpec((1, s_len, lanes), lambda b, p, tab: (b, 0, q0 + pairs + p)),
                  pl.BlockSpec((1, s_len, lanes), lambda b, p, tab: (b, 0, q0 + 2 * pairs + p))],
        out_specs=[pl.BlockSpec((1, s_len, lanes), lambda b, p, tab: (b, 0, p)),
                   page_spec, page_spec],
        scratch_shapes=[pltpu.VMEM((nb, 2, hd + ONES_ROWS, MOBA_BLOCK), BF16),
                        pltpu.VMEM((s_len, lanes), BF16),
                        pltpu.VMEM((s_len, lanes), BF16),
                        pltpu.VMEM((2, nb, nb, MOBA_BLOCK), jnp.int32)],
    )
    return pl.pallas_call(
        _moba_prompt_kernel,
        grid_spec=grid_spec,
        out_shape=[jax.ShapeDtypeStruct((bsz, s_len, width), F32), pages_shape, pages_shape],
        compiler_params=_params("parallel", "parallel"),
        name="moba_prompt",
    )(items, h3, h3, h3)


def _conv3_mix(gb, gcu, prev2, prev1, li, cw):
    s1 = jnp.where(li == 0, prev1[0], pltpu.roll(gcu, 1, 0))
    s2 = pltpu.roll(gcu, 2, 0)
    s2 = jnp.where(li == 0, prev2[0], jnp.where(li == 1, prev2[1], s2))
    conv = cw[0:1] * s2 + cw[1:2] * s1 + cw[2:3] * gcu
    return gb * conv


def _out_proj_residual(ya, yb, x, wo_ref, pw):
    half = ya.shape[1]
    y = _dot(ya.astype(BF16), wo_ref[0:half, :]) + _dot(yb.astype(BF16), wo_ref[half:, :])
    return x + _rms(y, pw)


def _l0_post_prompt_kernel(gb_ref, gc_ref, u_ref, gch_ref, uh_ref, yb_ref, x_ref, cw_ref, wo_ref,
                           pw_ref, pre_ref, wg_ref, wu_ref, wd_ref, post_ref, xo_ref, st_ref, *,
                           tiles_per_seq):
    i = pl.program_id(0)
    tm = gb_ref.shape[0]
    gcu = gc_ref[...] * u_ref[...]
    halo = gch_ref[...] * uh_ref[...]
    halo = halo * (i % tiles_per_seq != 0).astype(F32)
    li = lax.broadcasted_iota(jnp.int32, gcu.shape, 0)
    ya = _conv3_mix(gb_ref[...], gcu, (halo[6:7], halo[7:8]), (halo[7:8],), li, cw_ref[...])
    x_mid = _out_proj_residual(ya, yb_ref[...], x_ref[...], wo_ref, pw_ref[...])
    xo_ref[...] = _ffn_apply(x_mid, pre_ref, wg_ref, wu_ref, wd_ref, post_ref)

    @pl.when(i % tiles_per_seq == tiles_per_seq - 1)
    def _():
        st_ref[0] = gcu[tm - 2:tm, :]


def l0_post_prompt(h2, yb, x, conv_w, wo, post_w, ffn_w, seq_len):
    m, d = x.shape
    width = yb.shape[1]
    tm = ROW_TILE
    tiles_per_seq = seq_len // tm
    halo_idx = lambda i: jnp.maximum(i * (tm // 8) - 1, 0)
    const = lambda i: (0, 0)
    ffn_args, ffn_specs = _ffn_operands(ffn_w, d)
    return pl.pallas_call(
        functools.partial(_l0_post_prompt_kernel, tiles_per_seq=tiles_per_seq),
        grid=(m // tm,),
        in_specs=[pl.BlockSpec((tm, width), lambda i: (i, 0)),
                  pl.BlockSpec((tm, width), lambda i: (i, 1)),
                  pl.BlockSpec((tm, width), lambda i: (i, 2)),
                  pl.BlockSpec((8, width), lambda i: (halo_idx(i), 1)),
                  pl.BlockSpec((8, width), lambda i: (halo_idx(i), 2)),
                  pl.BlockSpec((tm, width), lambda i: (i, 0)),
                  pl.BlockSpec((tm, d), lambda i: (i, 0)),
                  pl.BlockSpec(conv_w.shape, const),
                  pl.BlockSpec(wo.shape, const, pipeline_mode=pl.Buffered(1)),
                  pl.BlockSpec((1, d), const)] + ffn_specs,
        out_specs=[pl.BlockSpec((tm, d), lambda i: (i, 0)),
                   pl.BlockSpec((1, 2, width), lambda i: (i // tiles_per_seq, 0, 0))],
        out_shape=[jax.ShapeDtypeStruct((m, d), F32),
                   jax.ShapeDtypeStruct((m // seq_len, 2, width), F32)],
        compiler_params=_params("arbitrary"),
        name="l0_post_prompt",
    )(h2, h2, h2, h2, h2, yb, x, conv_w, wo, post_w.reshape(1, d), *ffn_args)


def _l0_post_sample_kernel(gb_ref, gc_ref, u_ref, hist_ref, yb_ref, x_ref, cw_ref, wo_ref, pw_ref,
                           xo_ref, gcu_ref, *, seq_len):
    gcu = gc_ref[...] * u_ref[...]
    rows = gcu.shape[0]
    t = lax.broadcasted_iota(jnp.int32, gcu.shape, 0) % seq_len
    hist = hist_ref[...]
    hist_next = pltpu.roll(hist, rows - 1, 0)
    ya = _conv3_mix(gb_ref[...], gcu, (hist, hist), (hist_next,), t, cw_ref[...])
    xo_ref[...] = _out_proj_residual(ya, yb_ref[...], x_ref[...], wo_ref, pw_ref[...])
    gcu_ref[...] = gcu


def l0_post_sample(h2, yb, x, hist_rows, conv_w, wo, post_w, seq_len):
    m, d = x.shape
    width = yb.shape[1]
    const = lambda i: (0, 0)
    return pl.pallas_call(
        functools.partial(_l0_post_sample_kernel, seq_len=seq_len),
        grid=(1,),
        in_specs=[pl.BlockSpec((m, width), lambda i: (0, 0)),
                  pl.BlockSpec((m, width), lambda i: (0, 1)),
                  pl.BlockSpec((m, width), lambda i: (0, 2)),
                  pl.BlockSpec((m, width), const),
                  pl.BlockSpec((m, width), const),
                  pl.BlockSpec((m, d), const),
                  pl.BlockSpec(conv_w.shape, const),
                  pl.BlockSpec(wo.shape, const),
                  pl.BlockSpec((1, d), const)],
        out_specs=[pl.BlockSpec((m, d), const), pl.BlockSpec((m, width), const)],
        out_shape=[jax.ShapeDtypeStruct((m, d), F32), jax.ShapeDtypeStruct((m, width), F32)],
        compiler_params=_params("arbitrary"),
        name="l0_post_sample",
    )(h2, h2, h2, hist_rows, yb, x, conv_w, wo, post_w.reshape(1, d))


MEAN_BUFFERS = 24
BLOCKS_PER_TRIP = 4
ATTEND_SLOTS = 4


def _moba_select_kernel(pt_ref, qt_ref, ck_ref, sel_ref, buf, sem, km_ref, *, pages_per_block,
                        n_blk):
    bd, n_h, hd, t_len = qt_ref.shape
    page = ck_ref.shape[3]
    n_pages = n_blk * pages_per_block
    total = bd * n_pages

    def copy(i):
        slot = i % MEAN_BUFFERS
        return pltpu.make_async_copy(ck_ref.at[pt_ref[i]], buf.at[slot], sem.at[slot])

    for s in range(MEAN_BUFFERS - 1):
        copy(s).start()

    lane3 = lax.broadcasted_iota(jnp.int32, (n_h, hd, 128), 2)
    lane = lax.broadcasted_iota(jnp.int32, (n_h, 128), 1)
    sub = lax.broadcasted_iota(jnp.int32, (n_h, 128), 0)

    def per_seq(b, _):
        km_ref[...] = jnp.zeros(km_ref.shape, F32)

        def block_sum(n):
            acc = None
            for jj in range(pages_per_block):
                i = b * n_pages + n * pages_per_block + jj
                nxt = i + MEAN_BUFFERS - 1

                @pl.when(nxt < total)
                def _():
                    copy(nxt).start()

                copy(i).wait()
                acc = buf[i % MEAN_BUFFERS] if acc is None else acc + buf[i % MEAN_BUFFERS]
            return acc

        def per_group(g, _):
            sums =jnp.concatenate([block_sum(g * BLOCKS_PER_TRIP + u)
                                    for u in range(BLOCKS_PER_TRIP)], axis=0)
            means = jnp.sum(sums, axis=-1, keepdims=True) * (1.0 / (pages_per_block * page))
            km = km_ref[...]
            for u in range(BLOCKS_PER_TRIP):
                km = jnp.where(lane3 == g * BLOCKS_PER_TRIP + u, means[u * n_h:(u + 1) * n_h], km)
            km_ref[...] = km
            return 0

        lax.fori_loop(0, n_blk // BLOCKS_PER_TRIP, per_group, 0)
        out = jnp.zeros((n_h, 128), jnp.int32)
        for t in range(t_len):
            gate = jnp.full((n_h, 128), -jnp.inf, F32)
            for h in range(n_h):
                g_h = jnp.sum(km_ref[h] * qt_ref[b, h][:, t:t + 1], axis=0, keepdims=True)
                gate = jnp.where(sub == h, g_h, gate)
            gate = jnp.where(lane < n_blk, gate, -jnp.inf)
            for r in range(MOBA_TOPK):
                best = jnp.max(gate, axis=-1, keepdims=True)
                idx = jnp.min(jnp.where(gate == best, lane, 128), axis=-1, keepdims=True)
                out = jnp.where(lane == t * MOBA_TOPK + r, idx, out)
                gate = jnp.where(lane == idx, -jnp.inf, gate)
        sel_ref[b] = out
        return 0

    lax.fori_loop(0, bd, per_seq, 0)


def moba_select(page_table, qt4, cache_kt):
    bd, n_pages = page_table.shape
    _, n_h, hd, t_len = qt4.shape
    page = cache_kt.shape[3]
    ppb = MOBA_BLOCK // page
    n_blk = n_pages // ppb
    assert n_blk <= 128 and t_len * MOBA_TOPK <= 128
    grid_spec = pltpu.PrefetchScalarGridSpec(
        num_scalar_prefetch=1,
        grid=(1,),
        in_specs=[pl.BlockSpec(qt4.shape, lambda i, pt: (0, 0, 0, 0)),
                  pl.BlockSpec(memory_space=pl.ANY)],
        out_specs=pl.BlockSpec((bd, n_h, 128), lambda i, pt: (0, 0, 0)),
        scratch_shapes=[pltpu.VMEM((MEAN_BUFFERS, n_h, hd, page), F32),
                        pltpu.SemaphoreType.DMA((MEAN_BUFFERS,)),
                        pltpu.VMEM((n_h, hd, 128), F32)],
    )
    return pl.pallas_call(
        functools.partial(_moba_select_kernel, pages_per_block=ppb, n_blk=n_blk),
        grid_spec=grid_spec,
        out_shape=jax.ShapeDtypeStruct((bd, n_h, 128), jnp.int32),
        compiler_params=_params("arbitrary"),
        name="moba_select",
    )(page_table.reshape(-1), qt4, cache_kt)


def _moba_attend_kernel(pt_ref, sel_ref, qt_ref, knt_ref, vnt_ref, ck_ref, cv_ref, o_ref,
                        kbuf, vbuf, sem, s_ref, *, pages_per_block, n_pages):
    bd, n_h, hd, t_len = qt_ref.shape
    page = ck_ref.shape[3]
    per_t = MOBA_TOPK * pages_per_block
    scale = hd ** -0.5
    step = lax.broadcasted_iota(jnp.int32, (1, t_len), 1)
    step_col = lax.broadcasted_iota(jnp.int32, (hd, t_len), 1)

    def copies(pair, slot):
        b = pair // n_h
        h = pair % n_h
        out = []
        for t in range(t_len):
            for r in range(MOBA_TOPK):
                blk = sel_ref[(b * n_h + h) * (t_len * MOBA_TOPK) + t * MOBA_TOPK + r]
                for jj in range(pages_per_block):
                    phys = pt_ref[b * n_pages + blk * pages_per_block + jj]
                    s = t * per_t + r * pages_per_block + jj
                    out.append(pltpu.make_async_copy(ck_ref.at[phys, h], kbuf.at[slot, s],
                                                     sem.at[slot]))
                    out.append(pltpu.make_async_copy(cv_ref.at[phys, h], vbuf.at[slot, s],
                                                     sem.at[slot]))
        return out

    for ahead in range(ATTEND_SLOTS - 1):
        for c in copies(ahead, ahead):
            c.start()

    def per_pair(pair, _):
        slot = pair % ATTEND_SLOTS
        b = pair // n_h
        h = pair % n_h
        nxt = pair + ATTEND_SLOTS - 1

        @pl.when(nxt < bd * n_h)
        def _():
            for c in copies(nxt, nxt % ATTEND_SLOTS):
                c.start()

        for c in copies(pair, slot):
            c.wait()

        q_all = qt_ref[b, h] * scale
        k_new = knt_ref[b, h]
        v_new = vnt_ref[b, h]
        for t in range(t_len):
            qc = q_all[:, t:t + 1]
            for s in range(per_t):
                s_ref[t * 8 + s:t * 8 + s + 1, :] = jnp.sum(kbuf[slot, t * per_t + s] * qc, axis=0,
                                                             keepdims=True)
        out = jnp.zeros((hd, t_len), F32)
        for t in range(t_len):
            qc = q_all[:, t:t + 1]
            s_sel = s_ref[t * 8:(t + 1) * 8, :]
            s_own = jnp.where(step <= t, jnp.sum(k_new * qc, axis=0, keepdims=True), NEG)
            m = jnp.maximum(jnp.max(s_own, axis=-1, keepdims=True),
                            jnp.max(jnp.max(s_sel, axis=-1, keepdims=True), axis=0, keepdims=True))
            p_own = jnp.exp(s_own - m)
            p_sel = jnp.exp(s_sel - m)
            l = (jnp.sum(p_own, axis=-1, keepdims=True)
                 + jnp.sum(jnp.sum(p_sel, axis=-1, keepdims=True), axis=0, keepdims=True))
            acc = vbuf[slot, t * per_t] * p_sel[0:1, :]
            for s in range(1, per_t):
                acc = acc + vbuf[slot, t * per_t + s] * p_sel[s:s + 1, :]
            o_t = (jnp.sum(v_new * p_own, axis=-1, keepdims=True)
                   + jnp.sum(acc, axis=-1, keepdims=True)) / l
            out = jnp.where(step_col == t, o_t, out)
        o_ref[b, h] = out
        return 0

    s_ref[...] = jnp.full(s_ref.shape, NEG, F32)
    lax.fori_loop(0, bd * n_h, per_pair, 0)


def moba_attend(page_table, sel, qt4, knt4, vnt4, cache_kt, cache_vt):
    bd, n_pages = page_table.shape
    _, n_h, hd, t_len = qt4.shape
    page = cache_kt.shape[3]
    ppb = MOBA_BLOCK // page
    n_slab = t_len * MOBA_TOPK * ppb
    full = pl.BlockSpec(qt4.shape, lambda i, pt, sl: (0, 0, 0, 0))
    grid_spec = pltpu.PrefetchScalarGridSpec(
        num_scalar_prefetch=2,
        grid=(1,),
        in_specs=[full, full, full, pl.BlockSpec(memory_space=pl.ANY),
                  pl.BlockSpec(memory_space=pl.ANY)],
        out_specs=full,
        scratch_shapes=[pltpu.VMEM((ATTEND_SLOTS, n_slab, hd, page), F32),
                        pltpu.VMEM((ATTEND_SLOTS, n_slab, hd, page), F32),
                        pltpu.SemaphoreType.DMA((ATTEND_SLOTS,)),
                        pltpu.VMEM((t_len * 8, page), F32)],
    )
    return pl.pallas_call(
        functools.partial(_moba_attend_kernel, pages_per_block=ppb, n_pages=n_pages),
        grid_spec=grid_spec,
        out_shape=jax.ShapeDtypeStruct(qt4.shape, F32),
        compiler_params=_params("arbitrary"),
        name="moba_attend",
    )(page_table.reshape(-1), sel.reshape(-1), qt4, knt4, vnt4, cache_kt, cache_vt)


def _rows8(x):
    return x.reshape(x.shape[0] // 8, 8, x.shape[1])


def _softplus(x):
    return jnp.maximum(x, 0.0) + jnp.log1p(jnp.exp(-jnp.abs(x)))


def _ssd_kernel(*refs, d_inner, rows_valid, has_halo):
    if has_halo:
        (zx_ref, halo_ref, dt_ref, s0_ref, cw_ref, cb_ref, dtb_ref, alog_ref, dskip_ref, e_ref,
         y_ref, cs_ref, fs_ref, st_ref, xin_ref, dtin_ref, y_scr) = refs
        hist_ref = None
    else:
        (zx_ref, hist_ref, dt_ref, s0_ref, cw_ref, cb_ref, dtb_ref, alog_ref, dskip_ref, e_ref,
         y_ref, cs_ref, fs_ref, st_ref, xin_ref, dtin_ref, y_scr) = refs
        halo_ref = None
    c = pl.program_id(1)
    n_chunks = pl.num_programs(1)
    cl = SSM_CHUNK
    rows_blk = zx_ref.shape[1]
    conv_dim = cw_ref.shape[2]
    n_st = SSM_STATE
    hp = SSM_HEAD_DIM
    heads_per_group = d_inner // hp // SSM_GROUPS
    gw = heads_per_group * hp

    @pl.when(c == 0)
    def _():
        st_ref[...] = s0_ref[0].T

    if has_halo:
        xin_ref[0:8, :] = halo_ref[0, :, d_inner:d_inner + conv_dim] * (c != 0).astype(F32)
    else:
        xin_ref[0:8, :] = jnp.zeros((8, conv_dim), F32)
        xin_ref[5:8, :] = hist_ref[0]
    if rows_blk == cl:
        xin_ref[8:8 + cl, :] = zx_ref[0, :, d_inner:d_inner + conv_dim]
        dt_raw = dt_ref[0]
    else:
        xin_ref[8:8 + cl, :] = jnp.zeros((cl, conv_dim), F32)
        dtin_ref[...] = jnp.zeros(dtin_ref.shape, F32)
        xin_ref[8:8 + rows_blk, :] = zx_ref[0, :, d_inner:d_inner + conv_dim]
        dtin_ref[0:rows_blk, :] = dt_ref[0]
        dt_raw = dtin_ref[...]

    xbc = xin_ref[8:8 + cl, :]
    taps = [_rows8(xin_ref[5 + j:5 + j + cl, :]) * cw_ref[j][None] for j in range(3)]
    conv = taps[0] + taps[1] + taps[2] + _rows8(xbc) * cw_ref[3][None] + cb_ref[...][None]
    conv = conv.reshape(cl, conv_dim)
    act = conv * jax.nn.sigmoid(conv)

    @pl.when(c == n_chunks - 1)
    def _():
        cs_ref[0] = xbc[rows_valid - 3:rows_valid, :]

    dt = _softplus(dt_raw + dtb_ref[...])
    if rows_valid < cl:
        dt = jnp.where(lax.broadcasted_iota(jnp.int32, dt.shape, 0) < rows_valid, dt, 0.0)
    a = -jnp.exp(alog_ref[...])
    dta = dt * a
    r_i = lax.broadcasted_iota(jnp.int32, (cl, cl), 0)
    c_i = lax.broadcasted_iota(jnp.int32, (cl, cl), 1)
    causal = c_i <= r_i
    first_head_lanes = lax.broadcasted_iota(jnp.int32, (cl, 2 * hp), 1) < hp
    tril = jnp.where(causal, 1.0, 0.0)
    acum = jnp.dot(tril, dta, preferred_element_type=F32, precision=HIGHEST)
    acum_t = acum.T
    alast = acum[cl - 1:cl, :]
    narrow = jnp.concatenate([dt, jnp.exp(acum), jnp.exp(alast - acum),
                              jnp.broadcast_to(jnp.exp(alast), (8, alast.shape[1]))], axis=0)
    hi = narrow.astype(BF16)
    mid = (narrow - hi.astype(F32)).astype(BF16)
    wide = _dot(jnp.concatenate([hi, mid], axis=1), e_ref[...])
    dt_w = wide[0:cl]
    eac_w = wide[cl:2 * cl]
    dte_w = wide[2 * cl:3 * cl]
    cd_w = wide[3 * cl:3 * cl + 8]

    xs = act[:, 0:d_inner]
    xdt = xs * dt_w
    xdt_b = xdt.astype(BF16)
    xdtd_b = (xdt * dte_w).astype(BF16)
    for g in range(SSM_GROUPS):
        b_g = act[:, d_inner + g * n_st:d_inner + (g + 1) * n_st]
        c_g = act[:, d_inner + (SSM_GROUPS + g) * n_st:d_inner + (SSM_GROUPS + g + 1) * n_st]
        c_gb = c_g.astype(BF16)
        cb = _dot_nt(c_gb, b_g.astype(BF16))
        s_g = st_ref[:, g * gw:(g + 1) * gw]
        y_off = _dot(c_gb, s_g.astype(BF16)) * eac_w[:, g * gw:(g + 1) * gw]
        for r in range(0, heads_per_group, 2):
            h = g * heads_per_group + r
            stacked = []
            for hh in (h, h + 1):
                seg = acum[:, hh:hh + 1] - acum_t[hh:hh + 1, :]
                decay = jnp.exp(jnp.where(causal, seg, NEG))
                stacked.append((cb * decay).astype(BF16))
            both = _dot(jnp.concatenate(stacked, axis=0), xdt_b[:, h * hp:(h + 2) * hp])
            y_d = jnp.where(first_head_lanes, both[0:cl], both[cl:2 * cl])
            y_scr[:, h * hp:(h + 2) * hp] = y_d + y_off[:, r * hp:(r + 2) * hp]
        kept = (_rows8(s_g) * cd_w[:, g * gw:(g + 1) * gw][None]).reshape(n_st, gw)
        st_ref[:, g * gw:(g + 1) * gw] = kept + _dot(b_g.T.astype(BF16),
                                                     xdtd_b[:, g * gw:(g + 1) * gw])
    y = y_scr[...] + (_rows8(xs) * dskip_ref[...][None]).reshape(cl, d_inner)
    y_ref[0] = y[0:rows_blk, :]

    @pl.when(c == n_chunks - 1)
    def _():
        fs_ref[0] = st_ref[...].T


def ssd_mixer(zx3, dt3, hist, state0, conv_w, conv_b, dt_bias, a_log, d_skip, d_inner):
    nseq, t_len, _ = zx3.shape
    conv_dim = conv_w.shape[1]
    n_heads = d_inner // SSM_HEAD_DIM
    cl = SSM_CHUNK
    has_halo = hist is None
    if has_halo:
        rows_blk, rows_valid, n_chunks = cl, cl, t_len // cl
    else:
        rows_blk, rows_valid, n_chunks = t_len, t_len, 1
    pad = lambda v: jnp.pad(v.astype(F32), (0, 128 - n_heads)).reshape(1, 128)
    expand = (jnp.arange(256)[:, None] % 128
              == (jnp.arange(d_inner)[None, :] // SSM_HEAD_DIM)).astype(BF16)
    rows8 = lambda v: jnp.broadcast_to(v[..., None, :], v.shape[:-1] + (8, v.shape[-1]))
    d_wide = rows8(jnp.repeat(d_skip.astype(F32), SSM_HEAD_DIM))
    const2 = lambda s, c: (0, 0)
    if has_halo:
        second = pl.BlockSpec((1, 8, zx3.shape[2]),
                              lambda s, c: (s, jnp.maximum(c * (cl // 8) - 1, 0), 0))
        second_arg = zx3
    else:
        second = pl.BlockSpec((1,) + hist.shape[1:], lambda s, c: (s, 0, 0))
        second_arg = hist
    outs = pl.pallas_call(
        functools.partial(_ssd_kernel, d_inner=d_inner, rows_valid=rows_valid, has_halo=has_halo),
        grid=(nseq, n_chunks),
        in_specs=[pl.BlockSpec((1, rows_blk, zx3.shape[2]), lambda s, c: (s, c, 0)),
                  second,
                  pl.BlockSpec((1, rows_blk, 128), lambda s, c: (s, c, 0)),
                  pl.BlockSpec((1, d_inner, SSM_STATE), lambda s, c: (s, 0, 0)),
                  pl.BlockSpec((conv_w.shape[0], 8, conv_dim), lambda s, c: (0, 0, 0)),
                  pl.BlockSpec((8, conv_dim), const2),
                  pl.BlockSpec((1, 128), const2),
                  pl.BlockSpec((1, 128), const2),
                  pl.BlockSpec((8, d_inner), const2),
                  pl.BlockSpec((256, d_inner), const2)],
        out_specs=[pl.BlockSpec((1, rows_blk, d_inner), lambda s, c: (s, c, 0)),
                   pl.BlockSpec((1, 3, conv_dim), lambda s, c: (s, 0, 0)),
                   pl.BlockSpec((1, d_inner, SSM_STATE), lambda s, c: (s, 0, 0))],
        out_shape=[jax.ShapeDtypeStruct((nseq, t_len, d_inner), F32),
                   jax.ShapeDtypeStruct((nseq, 3, conv_dim), F32),
                   jax.ShapeDtypeStruct((nseq, d_inner, SSM_STATE), F32)],
        scratch_shapes=[pltpu.VMEM((SSM_STATE, d_inner), F32),
                        pltpu.VMEM((cl + 8, conv_dim), F32),
                        pltpu.VMEM((cl, 128), F32),
                        pltpu.VMEM((cl, d_inner), F32)],
        compiler_params=_params("parallel", "arbitrary"),
        name="ssd_mixer",
    )(zx3, second_arg, dt3, state0, rows8(conv_w), rows8(conv_b), pad(dt_bias), pad(a_log),
      d_wide, expand)
    return outs


L1_POST_ROW_TILE = 512


def _l1_post_kernel(y_ref, z_ref, x_ref, gw_ref, wo_ref, pw_ref, pre_ref, wg_ref, wu_ref, wd_ref,
                    post_ref, o_ref):
    z = z_ref[...]
    g = y_ref[...] * (z * jax.nn.sigmoid(z))
    o = _dot(_rms(g, gw_ref[...]).astype(BF16), wo_ref[...])
    x_mid = x_ref[...] + _rms(o, pw_ref[...])
    o_ref[...] = _ffn_apply(x_mid, pre_ref, wg_ref, wu_ref, wd_ref, post_ref)


def l1_post(y, zx, x, gate_w, wo, post_w, ffn_w):
    m, d = x.shape
    d_inner = y.shape[1]
    tm = _row_tile(m, L1_POST_ROW_TILE)
    const = lambda i: (0, 0)
    ffn_args, ffn_specs = _ffn_operands(ffn_w, d)
    return pl.pallas_call(
        _l1_post_kernel,
        grid=(m // tm,),
        in_specs=[pl.BlockSpec((tm, d_inner), lambda i: (i, 0)),
                  pl.BlockSpec((tm, d_inner), lambda i: (i, 0)),
                  pl.BlockSpec((tm, d), lambda i: (i, 0)),
                  pl.BlockSpec((1, d_inner), const),
                  pl.BlockSpec(wo.shape, const, pipeline_mode=pl.Buffered(1)),
                  pl.BlockSpec((1, d), const)] + ffn_specs,
        out_specs=pl.BlockSpec((tm, d), lambda i: (i, 0)),
        out_shape=jax.ShapeDtypeStruct((m, d), F32),
        compiler_params=_params("parallel"),
        name="l1_post",
    )(y, zx, x, gate_w.reshape(1, d_inner), wo, post_w.reshape(1, d), *ffn_args)


def kernel(x_prompt, x_sample, cache_k, cache_v, page_table, state_conv_a, state_conv_ssm, state_ssm, l0_norm_mix_pre, l0_w_in, l0_conv_w, l0_w_out, l0_norm_mix_post, l0_norm_ffn_pre, l0_ffn_gate, l0_ffn_up, l0_ffn_down, l0_norm_ffn_post, l1_norm_mix_pre, l1_w_in, l1_conv_w, l1_conv_b, l1_dt_bias, l1_a_log, l1_d_skip, l1_norm_gate, l1_w_out, l1_norm_mix_post, l1_norm_ffn_pre, l1_ffn_gate, l1_ffn_up, l1_ffn_down, l1_norm_ffn_post):
    bp, s_len, d = x_prompt.shape
    bd, t_len, _ = x_sample.shape
    n_heads, page, hd = cache_k.shape[1:]
    width = n_heads * hd
    assert (page_table.shape[1] * page) % MOBA_BLOCK == 0
    conv_dim = l1_conv_w.shape[1]
    ssm_heads = l1_dt_bias.shape[0]
    d_inner = ssm_heads * SSM_HEAD_DIM

    bf = lambda w: w.astype(BF16)
    xp = x_prompt.reshape(bp * s_len, d)
    xs = x_sample.reshape(bd * t_len, d)

    w_in0 = bf(l0_w_in)
    w_out0 = bf(l0_w_out)
    hp_ = norm_matmul(xp, l0_norm_mix_pre, w_in0)
    hs_ = norm_matmul(xs, l0_norm_mix_pre, w_in0)

    yb_p, kp_t, vp_t = moba_prompt(hp_.reshape(bp, s_len, -1), n_heads, hd, page)
    k_prompt = jnp.swapaxes(kp_t, 3, 4)
    v_prompt = jnp.swapaxes(vp_t, 3, 4)
    ffn0 = (l0_norm_ffn_pre, bf(l0_ffn_gate), bf(l0_ffn_up), bf(l0_ffn_down), l0_norm_ffn_post)
    xp, conv_a_prompt = l0_post_prompt(hp_, yb_p.reshape(bp * s_len, width), xp, l0_conv_w, w_out0,
                                       l0_norm_mix_post, ffn0, s_len)

    qkv_t = hs_[:, 3 * width:].reshape(bd, t_len, 3, n_heads, hd).transpose(2, 0, 3, 4, 1)
    qt4, knt4, vnt4 = qkv_t[0], qkv_t[1], qkv_t[2]
    cache_kt = jnp.swapaxes(cache_k, 2, 3)
    cache_vt = jnp.swapaxes(cache_v, 2, 3)
    sel = moba_select(page_table, qt4, cache_kt)[:, :, :t_len * MOBA_TOPK]
    yb_s = moba_attend(page_table, sel, qt4, knt4, vnt4, cache_kt, cache_vt)
    yb_s = yb_s.transpose(0, 3, 1, 2).reshape(bd * t_len, width)
    hist_rows = jnp.pad(state_conv_a, ((0, 0), (0, t_len - state_conv_a.shape[1]), (0, 0)))
    xs, gcu_s = l0_post_sample(hs_, yb_s, xs, hist_rows.reshape(bd * t_len, width), l0_conv_w,
                               w_out0, l0_norm_mix_post, t_len)
    conv_a_sample = gcu_s.reshape(bd, t_len, width)[:, t_len - 2:]
    k_sample = knt4.transpose(0, 1, 3, 2)
    v_sample = vnt4.transpose(0, 1, 3, 2)

    xs = ffn(xs, ffn0)

    w_zx = bf(l1_w_in[:, :d_inner + conv_dim])
    w_dt = jnp.pad(l1_w_in[:, d_inner + conv_dim:], ((0, 0), (0, 128 - ssm_heads)))
    w_out1 = bf(l1_w_out)
    zx_p, dt_p = norm_matmul(xp, l1_norm_mix_pre, w_zx, w_dt)
    zx_s, dt_s = norm_matmul(xs, l1_norm_mix_pre, w_zx, w_dt)

    zeros_state = jnp.zeros((bp, d_inner, SSM_STATE), F32)
    y_p, conv_ssm_prompt, fs_p = ssd_mixer(zx_p.reshape(bp, s_len, -1), dt_p.reshape(bp, s_len, 128),
                                           None, zeros_state, l1_conv_w, l1_conv_b, l1_dt_bias,
                                           l1_a_log, l1_d_skip, d_inner)
    y_s, conv_ssm_sample, fs_s = ssd_mixer(zx_s.reshape(bd, t_len, -1), dt_s.reshape(bd, t_len, 128),
                                           state_conv_ssm, state_ssm.reshape(bd, d_inner, SSM_STATE),
                                           l1_conv_w, l1_conv_b, l1_dt_bias, l1_a_log, l1_d_skip,
                                           d_inner)
    ssm_prompt = fs_p.reshape(bp, ssm_heads, SSM_HEAD_DIM, SSM_STATE)
    ssm_sample = fs_s.reshape(bd, ssm_heads, SSM_HEAD_DIM, SSM_STATE)

    ffn1 = (l1_norm_ffn_pre, bf(l1_ffn_gate), bf(l1_ffn_up), bf(l1_ffn_down), l1_norm_ffn_post)
    xp = l1_post(y_p.reshape(bp * s_len, d_inner), zx_p, xp, l1_norm_gate, w_out1, l1_norm_mix_post,
                 ffn1)
    xs = l1_post(y_s.reshape(bd * t_len, d_inner), zx_s, xs, l1_norm_gate, w_out1, l1_norm_mix_post,
                 ffn1)

    return (xp.reshape(bp, s_len, d), xs.reshape(bd, t_len, d), k_prompt, v_prompt, k_sample,
            v_sample, conv_a_prompt, conv_a_sample, conv_ssm_prompt, conv_ssm_sample, ssm_prompt,
            ssm_sample)
```

```python
import functools

import jax
import jax.numpy as jnp
from jax import lax
from jax.experimental import pallas as pl
from jax.experimental.pallas import tpu as pltpu

F32 = jnp.float32
BF16 = jnp.bfloat16
HIGHEST = lax.Precision.HIGHEST

NORM_EPS = 1e-6
MOBA_BLOCK = 256
MOBA_TOPK = 3
SSM_CHUNK = 128
SSM_HEAD_DIM = 64
SSM_STATE = 128
SSM_GROUPS = 4
NEG = -1e30
LOG2E = 1.4426950408889634
ONES_ROWS = 16

VMEM_LIMIT_BYTES = 56 * 1024 * 1024
ROW_TILE = 512


def _params(*sem):
    return pltpu.CompilerParams(dimension_semantics=sem, vmem_limit_bytes=VMEM_LIMIT_BYTES)


def _rms(x, w):
    return x * lax.rsqrt(jnp.mean(x * x, axis=-1, keepdims=True) + NORM_EPS) * w


def _dot(a, b):
    return jnp.dot(a, b, preferred_element_type=F32)


def _dot_nt(a, b, precision=None):
    return lax.dot_general(a, b, (((1,), (1,)), ((), ())), preferred_element_type=F32,
                           precision=precision)


def _row_tile(m, tile=ROW_TILE):
    return tile if m % tile == 0 else m


PROJ_ROW_TILE = 512
PROJ_COL_CHUNK = 1024


def _norm_matmul_kernel(*refs, narrow):
    if narrow:
        x_ref, nw_ref, w_ref, whi_ref, wlo_ref, o_ref, o2_ref = refs
    else:
        x_ref, nw_ref, w_ref, o_ref = refs
    xn = _rms(x_ref[...], nw_ref[...])
    xh = xn.astype(BF16)
    n = w_ref.shape[1]
    for c0 in range(0, n, PROJ_COL_CHUNK):
        c1 = min(c0 + PROJ_COL_CHUNK, n)
        o_ref[:, c0:c1] = _dot(xh, w_ref[:, c0:c1])
    if narrow:
        xl = (xn - xh.astype(F32)).astype(BF16)
        o2_ref[...] = _dot(xh, whi_ref[...]) + _dot(xl, whi_ref[...]) + _dot(xh, wlo_ref[...])


def norm_matmul(x, nw, w_bf16, w_narrow=None):
    m, d = x.shape
    n = w_bf16.shape[1]
    tm = PROJ_ROW_TILE if m % PROJ_ROW_TILE == 0 else m
    const = lambda i: (0, 0)
    resident = lambda shape: pl.BlockSpec(shape, const, pipeline_mode=pl.Buffered(1))
    in_specs = [pl.BlockSpec((tm, d), lambda i: (i, 0)), pl.BlockSpec((1, d), const),
                resident((d, n))]
    out_specs = [pl.BlockSpec((tm, n), lambda i: (i, 0))]
    out_shape = [jax.ShapeDtypeStruct((m, n), F32)]
    args = [x, nw.reshape(1, d), w_bf16]
    if w_narrow is not None:
        n2 = w_narrow.shape[1]
        whi = w_narrow.astype(BF16)
        wlo = (w_narrow - whi.astype(F32)).astype(BF16)
        in_specs += [resident((d, n2)), resident((d, n2))]
        out_specs.append(pl.BlockSpec((tm, n2), lambda i: (i, 0)))
        out_shape.append(jax.ShapeDtypeStruct((m, n2), F32))
        args += [whi, wlo]
    outs = pl.pallas_call(
        functools.partial(_norm_matmul_kernel, narrow=w_narrow is not None),
        grid=(m // tm,),
        in_specs=in_specs,
        out_specs=out_specs,
        out_shape=out_shape,
        compiler_params=_params("parallel"),
        name="norm_matmul",
    )(*args)
    return outs if w_narrow is not None else outs[0]


FFN_CHUNK = 256
FFN_ROW_TILE = 1024


def _ffn_apply(x, pre_ref, wg_ref, wu_ref, wd_ref, post_ref, side_task=None):
    h = _rms(x, pre_ref[...]).astype(BF16)
    n_chunks = wg_ref.shape[1] // FFN_CHUNK
    acc = jnp.zeros(x.shape, F32)
    for c in range(n_chunks):
        sl = slice(c * FFN_CHUNK, (c + 1) * FFN_CHUNK)
        g = _dot(h, wg_ref[:, sl])
        u = _dot(h, wu_ref[:, sl])
        a = (g * jax.nn.sigmoid(g) * u).astype(BF16)
        acc = acc + _dot(a, wd_ref[sl, :])
        if side_task is not None:
            side_task(c, n_chunks)
    return x + _rms(acc, post_ref[...])


def _ffn_kernel(x_ref, pre_ref, wg_ref, wu_ref, wd_ref, post_ref, o_ref):
    o_ref[...] = _ffn_apply(x_ref[...], pre_ref, wg_ref, wu_ref, wd_ref, post_ref)


def _ffn_operands(ffn_w, d, const=lambda i: (0, 0)):
    pre_w, wg, wu, wd, post_w = ffn_w
    resident = lambda w: pl.BlockSpec(w.shape, const, pipeline_mode=pl.Buffered(1))
    args = [pre_w.reshape(1, d), wg, wu, wd, post_w.reshape(1, d)]
    specs = [pl.BlockSpec((1, d), const), resident(wg), resident(wu), resident(wd),
             pl.BlockSpec((1, d), const)]
    return args, specs


def ffn(x, ffn_w):
    m, d = x.shape
    tm = _row_tile(m, FFN_ROW_TILE)
    ffn_args, ffn_specs = _ffn_operands(ffn_w, d)
    return pl.pallas_call(
        _ffn_kernel,
        grid=(m // tm,),
        in_specs=[pl.BlockSpec((tm, d), lambda i: (i, 0))] + ffn_specs,
        out_specs=pl.BlockSpec((tm, d), lambda i: (i, 0)),
        out_shape=jax.ShapeDtypeStruct((m, d), F32),
        compiler_params=_params("parallel"),
        name="ffn",
    )(x, *ffn_args)


def _moba_work_items(nb):
    items = [(qi, i, int(i == qi // 2)) for qi in range(nb) for i in range(qi // 2 + 1)]
    return [list(col) for col in zip(*items)]


def _moba_prompt_kernel(tab_ref, q_ref, k_ref, v_ref, o_ref, kp_ref, vp_ref, vt_ref, kb_ref,
                        qb_ref, sel_ref):
    s_len = k_ref.shape[1]
    nb = s_len // MOBA_BLOCK
    n_items = tab_ref.shape[0] // 3
    npages = kp_ref.shape[1]
    hd = kp_ref.shape[3]
    page = kp_ref.shape[4]
    ppb = MOBA_BLOCK // page
    scale = hd ** -0.5

    kb_ref[...] = k_ref[0].astype(BF16)
    qb_ref[...] = (q_ref[0] * (scale * LOG2E)).astype(BF16)
    km = jnp.mean(k_ref[0].reshape(nb, MOBA_BLOCK, 2 * hd), axis=1)
    blk = lax.broadcasted_iota(jnp.int32, (nb, s_len), 0)
    own = lax.broadcasted_iota(jnp.int32, (nb, s_len), 1) // MOBA_BLOCK
    for hh in range(2):
        lo = hh * hd
        gate = _dot_nt(km[:, lo:lo + hd], q_ref[0, :, lo:lo + hd], precision=HIGHEST)
        cnt = jnp.zeros((nb, s_len), F32)
        for jp in range(nb - 1):
            gj = gate[jp:jp + 1, :]
            beats = ((gj > gate) | ((gj == gate) & (jp < blk))) & (jp < own)
            cnt = cnt + jnp.where(beats, 1.0, 0.0)
        picked = ((cnt < MOBA_TOPK) & (blk < own)) | (blk == own)
        limit_shift = jnp.where(picked, 0, -4 * MOBA_BLOCK)
        for qb in range(nb):
            sel_ref[hh, qb] = limit_shift[:, qb * MOBA_BLOCK:(qb + 1) * MOBA_BLOCK]
    for pg in range(npages):
        rows = slice(pg * page, (pg + 1) * page)
        kt = k_ref[0, rows, :].T
        vt = v_ref[0, rows, :].T
        kp_ref[0, pg] = kt.reshape(2, hd, page)
        vp_ref[0, pg] = vt.reshape(2, hd, page)
        cols = slice((pg % ppb) * page, (pg % ppb + 1) * page)
        for hh in range(2):
            vt_ref[pg // ppb, hh, 0:hd, cols] = vt[hh * hd:(hh + 1) * hd].astype(BF16)
    ones_row = jnp.where(lax.broadcasted_iota(jnp.int32, (ONES_ROWS, MOBA_BLOCK), 0) == 0, 1.0, 0.0)
    for j in range(nb):
        for hh in range(2):
            vt_ref[j, hh, hd:hd + ONES_ROWS, :] = ones_row.astype(BF16)

    row = lax.broadcasted_iota(jnp.int32, (MOBA_BLOCK, MOBA_BLOCK), 0)
    col_row = lax.broadcasted_iota(jnp.int32, (1, MOBA_BLOCK), 1)

    def scores(qi, j, hh):
        jc = jnp.minimum(j, qi)
        q_tile = qb_ref[pl.ds(pl.multiple_of(qi * MOBA_BLOCK, MOBA_BLOCK), MOBA_BLOCK),
                        hh * hd:(hh + 1) * hd]
        k_tile = kb_ref[pl.ds(pl.multiple_of(jc * MOBA_BLOCK, MOBA_BLOCK), MOBA_BLOCK),
                        hh * hd:(hh + 1) * hd]
        shift = jnp.where(j < qi, MOBA_BLOCK, jnp.where(j == qi, 0, -4 * MOBA_BLOCK))
        limit = col_row + (1 + shift) + sel_ref[hh, qi, pl.ds(jc, 1), :]
        return jnp.where(row < limit, _dot_nt(k_tile, q_tile), -jnp.inf)

    def weighted_values(qi, i, state):
        ja = jnp.minimum(2 * i, qi)
        jb = jnp.minimum(2 * i + 1, qi)
        return [alpha * acc + _dot(vt_ref[ja, hh], pa) + _dot(vt_ref[jb, hh], pb)
                for hh, (_, acc, alpha, pa, pb) in enumerate(state)]

    def write_block(qi, accs):
        o_ref[0, pl.ds(pl.multiple_of(qi * MOBA_BLOCK, MOBA_BLOCK), MOBA_BLOCK), :] = (
            jnp.concatenate([acc[0:hd] / acc[hd:hd + 1] for acc in accs], axis=0).T)

    def body(t, state):
        qi, i = tab_ref[t], tab_ref[n_items + t]
        tiles = [scores(qi, 2 * i + b, hh) for hh in range(2) for b in range(2)]
        prev = jnp.maximum(t - 1, 0)
        qi_prev = tab_ref[prev]
        accs = weighted_values(qi_prev, tab_ref[n_items + prev], state)
        fresh = (i == 0).astype(F32)
        new = []
        for hh in range(2):
            m = state[hh][0]
            m = m + fresh * (NEG - m)
            sa, sb = tiles[2 * hh], tiles[2 * hh + 1]
            m_new = jnp.maximum(m, jnp.max(jnp.maximum(sa, sb), axis=0, keepdims=True))
            alpha = jnp.exp2(m - m_new) * (1.0 - fresh)
            new.append((m_new, accs[hh], alpha, jnp.exp2(sa - m_new).astype(BF16),
                        jnp.exp2(sb - m_new).astype(BF16)))

        @pl.when((tab_ref[2 * n_items + prev] == 1) & (t > 0))
        def _():
            write_block(qi_prev, accs)

        return tuple(new)

    no_p = jnp.zeros((MOBA_BLOCK, MOBA_BLOCK), BF16)
    init = (jnp.full((1, MOBA_BLOCK), NEG, F32), jnp.zeros((hd + ONES_ROWS, MOBA_BLOCK), F32),
            jnp.zeros((1, MOBA_BLOCK), F32), no_p, no_p)
    state = lax.fori_loop(0, n_items, body, (init, init))
    write_block(nb - 1, weighted_values(nb - 1, (nb - 1) // 2, state))


def moba_prompt(h3, n_heads, hd, page):
    bsz, s_len, _ = h3.shape
    width = n_heads * hd
    lanes = 2 * hd
    pairs = n_heads // 2
    q0 = 3 * width // lanes
    nb = s_len // MOBA_BLOCK
    pages_shape = jax.ShapeDtypeStruct((bsz, s_len // page, n_heads, hd, page), F32)
    page_spec = pl.BlockSpec((1, s_len // page, 2, hd, page), lambda b, p, tab: (b, 0, p, 0, 0))
    items = jnp.asarray(sum(_moba_work_items(nb), []), jnp.int32)
    grid_spec = pltpu.PrefetchScalarGridSpec(
        num_scalar_prefetch=1,
        grid=(bsz, pairs),
        in_specs=[pl.BlockSpec((1, s_len, lanes), lambda b, p, tab: (b, 0, q0 + p)),
                  pl.BlockSpec((1, s_len, lanes), lambda b, p, tab: (b, 0, q0 + pairs + p)),
                  pl.BlockSpec((1, s_len, lanes), lambda b, p, tab: (b, 0, q0 + 2 * pairs + p))],
        out_specs=[pl.BlockSpec((1, s_len, lanes), lambda b, p, tab: (b, 0, p)),
                   page_spec, page_spec],
        scratch_shapes=[pltpu.VMEM((nb, 2, hd + ONES_ROWS, MOBA_BLOCK), BF16),
                        pltpu.VMEM((s_len, lanes), BF16),
                        pltpu.VMEM((s_len, lanes), BF16),
                        pltpu.VMEM((2, nb, nb, MOBA_BLOCK), jnp.int32)],
    )
    return pl.pallas_call(
        _moba_prompt_kernel,
        grid_spec=grid_spec,
        out_shape=[jax.ShapeDtypeStruct((bsz, s_len, width), F32), pages_shape, pages_shape],
        compiler_params=_params("parallel", "parallel"),
        name="moba_prompt",
    )(items, h3, h3, h3)


def _conv3_mix(gb, gcu, prev2, prev1, li, cw):
    s1 = jnp.where(li == 0, prev1[0], pltpu.roll(gcu, 1, 0))
    s2 = pltpu.roll(gcu, 2, 0)
    s2 = jnp.where(li == 0, prev2[0], jnp.where(li == 1, prev2[1], s2))
    conv = cw[0:1] * s2 + cw[1:2] * s1 + cw[2:3] * gcu
    return gb * conv


def _out_proj_residual(ya, yb, x, wo_ref, pw):
    half = ya.shape[1]
    y = _dot(ya.astype(BF16), wo_ref[0:half, :]) + _dot(yb.astype(BF16), wo_ref[half:, :])
    return x + _rms(y, pw)


def _cache_block_means(seq, n_pages, pt_ref, ck_ref, km_ref, buf, sem):
    _, n_h, hd, _ = km_ref.shape
    page = ck_ref.shape[3]
    ppc = buf.shape[0] // 2
    pages_per_block = MOBA_BLOCK // page
    lane3 = lax.broadcasted_iota(jnp.int32, (n_h, hd, 128), 2)

    def pages_of(c):
        return [p for p in range(c * ppc, (c + 1) * ppc) if p < n_pages]

    def copy(p):
        half = (p // ppc) % 2
        return pltpu.make_async_copy(ck_ref.at[pt_ref[seq * n_pages + p]],
                                     buf.at[half * ppc + p % ppc], sem.at[half])

    def start(c):
        for p in pages_of(c):
            copy(p).start()

    def run(c, n_chunks):
        if c + 1 < n_chunks:
            start(c + 1)
        pages = pages_of(c)
        for p in pages:
            copy(p).wait()
        half = (c % 2) * ppc
        blocks = len(pages) // pages_per_block
        if blocks == 0:
            return
        sums = []
        for u in range(blocks):
            acc = buf[half + u * pages_per_block]
            for jj in range(1, pages_per_block):
                acc = acc + buf[half + u * pages_per_block + jj]
            sums.append(acc)
        means = jnp.sum(jnp.concatenate(sums, axis=0), axis=-1, keepdims=True) * (1.0 / MOBA_BLOCK)
        km = km_ref[0]
        first = c * ppc // pages_per_block
        for u in range(blocks):
            km = jnp.where(lane3 == first + u, means[u * n_h:(u + 1) * n_h], km)
        km_ref[0] = km

    return start, run


def _l0_post_prompt_kernel(pt_ref, gb_ref, gc_ref, u_ref, gch_ref, uh_ref, yb_ref, x_ref, cw_ref,
                           wo_ref, pw_ref, pre_ref, wg_ref, wu_ref, wd_ref, post_ref, ck_ref,
                           xo_ref, st_ref, km_ref, buf, sem, *, tiles_per_seq, n_pages):
    i = pl.program_id(0)
    tm = gb_ref.shape[0]
    start, run = _cache_block_means(i, n_pages, pt_ref, ck_ref, km_ref, buf, sem)
    km_ref[...] = jnp.zeros(km_ref.shape, F32)
    start(0)

    gcu = gc_ref[...] * u_ref[...]
    halo = gch_ref[...] * uh_ref[...]
    halo = halo * (i % tiles_per_seq != 0).astype(F32)
    li = lax.broadcasted_iota(jnp.int32, gcu.shape, 0)
    ya = _conv3_mix(gb_ref[...], gcu, (halo[6:7], halo[7:8]), (halo[7:8],), li, cw_ref[...])
    x_mid = _out_proj_residual(ya, yb_ref[...], x_ref[...], wo_ref, pw_ref[...])
    xo_ref[...] = _ffn_apply(x_mid, pre_ref, wg_ref, wu_ref, wd_ref, post_ref, side_task=run)

    @pl.when(i % tiles_per_seq == tiles_per_seq - 1)
    def _():
        st_ref[0] = gcu[tm - 2:tm, :]


def l0_post_prompt(h2, yb, x, conv_w, wo, post_w, ffn_w, seq_len, page_table, cache_kt):
    m, d = x.shape
    width = yb.shape[1]
    tm = ROW_TILE
    tiles_per_seq = seq_len // tm
    bd, n_pages = page_table.shape
    _, n_h, hd, page = cache_kt.shape
    n_chunks = ffn_w[1].shape[1] // FFN_CHUNK
    ppb = MOBA_BLOCK // page
    pages_per_chunk = -(-n_pages // (n_chunks * ppb)) * ppb
    assert m // tm == bd and n_pages // ppb <= 128
    halo_idx = lambda i: jnp.maximum(i * (tm // 8) - 1, 0)
    const = lambda i, pt: (0, 0)
    ffn_args, ffn_specs = _ffn_operands(ffn_w, d, const)
    grid_spec = pltpu.PrefetchScalarGridSpec(
        num_scalar_prefetch=1,
        grid=(m // tm,),
        in_specs=[pl.BlockSpec((tm, width), lambda i, pt: (i, 0)),
                  pl.BlockSpec((tm, width), lambda i, pt: (i, 1)),
                  pl.BlockSpec((tm, width), lambda i, pt: (i, 2)),
                  pl.BlockSpec((8, width), lambda i, pt: (halo_idx(i), 1)),
                  pl.BlockSpec((8, width), lambda i, pt: (halo_idx(i), 2)),
                  pl.BlockSpec((tm, width), lambda i, pt: (i, 0)),
                  pl.BlockSpec((tm, d), lambda i, pt: (i, 0)),
                  pl.BlockSpec(conv_w.shape, const),
                  pl.BlockSpec(wo.shape, const, pipeline_mode=pl.Buffered(1)),
                  pl.BlockSpec((1, d), const)] + ffn_specs + [pl.BlockSpec(memory_space=pl.ANY)],
        out_specs=[pl.BlockSpec((tm, d), lambda i, pt: (i, 0)),
                   pl.BlockSpec((1, 2, width), lambda i, pt: (i // tiles_per_seq, 0, 0)),
                   pl.BlockSpec((1, n_h, hd, 128), lambda i, pt: (i, 0, 0, 0))],
        scratch_shapes=[pltpu.VMEM((2 * pages_per_chunk, n_h, hd, page), F32),
                        pltpu.SemaphoreType.DMA((2,))],
    )
    return pl.pallas_call(
        functools.partial(_l0_post_prompt_kernel, tiles_per_seq=tiles_per_seq, n_pages=n_pages),
        grid_spec=grid_spec,
        out_shape=[jax.ShapeDtypeStruct((m, d), F32),
                   jax.ShapeDtypeStruct((m // seq_len, 2, width), F32),
                   jax.ShapeDtypeStruct((bd, n_h, hd, 128), F32)],
        compiler_params=_params("arbitrary"),
        name="l0_post_prompt",
    )(page_table.reshape(-1), h2, h2, h2, h2, h2, yb, x, conv_w, wo, post_w.reshape(1, d), *ffn_args,
      cache_kt)


def _l0_post_sample_kernel(gb_ref, gc_ref, u_ref, hist_ref, yb_ref, x_ref, cw_ref, wo_ref, pw_ref,
                           xo_ref, gcu_ref, *, seq_len):
    gcu = gc_ref[...] * u_ref[...]
    rows = gcu.shape[0]
    t = lax.broadcasted_iota(jnp.int32, gcu.shape, 0) % seq_len
    hist = hist_ref[...]
    hist_next = pltpu.roll(hist, rows - 1, 0)
    ya = _conv3_mix(gb_ref[...], gcu, (hist, hist), (hist_next,), t, cw_ref[...])
    xo_ref[...] = _out_proj_residual(ya, yb_ref[...], x_ref[...], wo_ref, pw_ref[...])
    gcu_ref[...] = gcu


def l0_post_sample(h2, yb, x, hist_rows, conv_w, wo, post_w, seq_len):
    m, d = x.shape
    width = yb.shape[1]
    const = lambda i: (0, 0)
    return pl.pallas_call(
        functools.partial(_l0_post_sample_kernel, seq_len=seq_len),
        grid=(1,),
        in_specs=[pl.BlockSpec((m, width), lambda i: (0, 0)),
                  pl.BlockSpec((m, width), lambda i: (0, 1)),
                  pl.BlockSpec((m, width), lambda i: (0, 2)),
                  pl.BlockSpec((m, width), const),
                  pl.BlockSpec((m, width), const),
                  pl.BlockSpec((m, d), const),
                  pl.BlockSpec(conv_w.shape, const),
                  pl.BlockSpec(wo.shape, const),
                  pl.BlockSpec((1, d), const)],
        out_specs=[pl.BlockSpec((m, d), const), pl.BlockSpec((m, width), const)],
        out_shape=[jax.ShapeDtypeStruct((m, d), F32), jax.ShapeDtypeStruct((m, width), F32)],
        compiler_params=_params("arbitrary"),
        name="l0_post_sample",
    )(h2, h2, h2, hist_rows, yb, x, conv_w, wo, post_w.reshape(1, d))


ATTEND_SLOTS = 4


def _moba_select_kernel(km_ref, qt_ref, sel_ref, *, n_blk):
    _, n_h, hd, t_len = qt_ref.shape
    lane = lax.broadcasted_iota(jnp.int32, (n_h, 128), 1)
    sub = lax.broadcasted_iota(jnp.int32, (n_h, 128), 0)
    out = jnp.zeros((n_h, 128), jnp.int32)
    for t in range(t_len):
        gate = jnp.full((n_h, 128), -jnp.inf, F32)
        for h in range(n_h):
            g_h = jnp.sum(km_ref[0, h] * qt_ref[0, h][:, t:t + 1], axis=0, keepdims=True)
            gate = jnp.where(sub == h, g_h, gate)
        gate = jnp.where(lane < n_blk, gate, -jnp.inf)
        for r in range(MOBA_TOPK):
            best = jnp.max(gate, axis=-1, keepdims=True)
            idx = jnp.min(jnp.where(gate == best, lane, 128), axis=-1, keepdims=True)
            out = jnp.where(lane == t * MOBA_TOPK + r, idx, out)
            gate = jnp.where(lane == idx, -jnp.inf, gate)
    sel_ref[0] = out


def moba_select(block_means, qt4, n_blk):
    bd, n_h, hd, t_len = qt4.shape
    assert n_blk <= 128 and t_len * MOBA_TOPK <= 128
    return pl.pallas_call(
        functools.partial(_moba_select_kernel, n_blk=n_blk),
        grid=(bd,),
        in_specs=[pl.BlockSpec((1, n_h, hd, 128), lambda b: (b, 0, 0, 0)),
                  pl.BlockSpec((1, n_h, hd, t_len), lambda b: (b, 0, 0, 0))],
        out_specs=pl.BlockSpec((1, n_h, 128), lambda b: (b, 0, 0)),
        out_shape=jax.ShapeDtypeStruct((bd, n_h, 128), jnp.int32),
        compiler_params=_params("parallel"),
        name="moba_select",
    )(block_means, qt4)


def _moba_attend_kernel(pt_ref, sel_ref, qt_ref, knt_ref, vnt_ref, ck_ref, cv_ref, o_ref,
                        kbuf, vbuf, sem, s_ref, *, pages_per_block, n_pages):
    bd, n_h, hd, t_len = qt_ref.shape
    page = ck_ref.shape[3]
    per_t = MOBA_TOPK * pages_per_block
    scale = hd ** -0.5
    step = lax.broadcasted_iota(jnp.int32, (1, t_len), 1)
    step_col = lax.broadcasted_iota(jnp.int32, (hd, t_len), 1)

    def copies(pair, slot):
        b = pair // n_h
        h = pair % n_h
        out = []
        for t in range(t_len):
            for r in range(MOBA_TOPK):
                blk = sel_ref[(b * n_h + h) * (t_len * MOBA_TOPK) + t * MOBA_TOPK + r]
                for jj in range(pages_per_block):
                    phys = pt_ref[b * n_pages + blk * pages_per_block + jj]
                    s = t * per_t + r * pages_per_block + jj
                    out.append(pltpu.make_async_copy(ck_ref.at[phys, h], kbuf.at[slot, s],
                                                     sem.at[slot]))
                    out.append(pltpu.make_async_copy(cv_ref.at[phys, h], vbuf.at[slot, s],
                                                     sem.at[slot]))
        return out

    for ahead in range(ATTEND_SLOTS - 1):
        for c in copies(ahead, ahead):
            c.start()

    def per_pair(pair, _):
        slot = pair % ATTEND_SLOTS
        b = pair // n_h
        h = pair % n_h
        nxt = pair + ATTEND_SLOTS - 1

        @pl.when(nxt < bd * n_h)
        def _():
            for c in copies(nxt, nxt % ATTEND_SLOTS):
                c.start()

        for c in copies(pair, slot):
            c.wait()

        q_all = qt_ref[b, h] * scale
        k_new = knt_ref[b, h]
        v_new = vnt_ref[b, h]
        for t in range(t_len):
            qc = q_all[:, t:t + 1]
            for s in range(per_t):
                s_ref[t * 8 + s:t * 8 + s + 1, :] = jnp.sum(kbuf[slot, t * per_t + s] * qc, axis=0,
                                                             keepdims=True)
        out = jnp.zeros((hd, t_len), F32)
        for t in range(t_len):
            qc = q_all[:, t:t + 1]
            s_sel = s_ref[t * 8:(t + 1) * 8, :]
            s_own = jnp.where(step <= t, jnp.sum(k_new * qc, axis=0, keepdims=True), NEG)
            m = jnp.maximum(jnp.max(s_own, axis=-1, keepdims=True),
                            jnp.max(jnp.max(s_sel, axis=-1, keepdims=True), axis=0, keepdims=True))
            p_own = jnp.exp(s_own - m)
            p_sel = jnp.exp(s_sel - m)
            l = (jnp.sum(p_own, axis=-1, keepdims=True)
                 + jnp.sum(jnp.sum(p_sel, axis=-1, keepdims=True), axis=0, keepdims=True))
            acc = vbuf[slot, t * per_t] * p_sel[0:1, :]
            for s in range(1, per_t):
                acc = acc + vbuf[slot, t * per_t + s] * p_sel[s:s + 1, :]
            o_t = (jnp.sum(v_new * p_own, axis=-1, keepdims=True)
                   + jnp.sum(acc, axis=-1, keepdims=True)) / l
            out = jnp.where(step_col == t, o_t, out)
        o_ref[b, h] = out
        return 0

    s_ref[...] = jnp.full(s_ref.shape, NEG, F32)
    lax.fori_loop(0, bd * n_h, per_pair, 0)


def moba_attend(page_table, sel, qt4, knt4, vnt4, cache_kt, cache_vt):
    bd, n_pages = page_table.shape
    _, n_h, hd, t_len = qt4.shape
    page = cache_kt.shape[3]
    ppb = MOBA_BLOCK // page
    n_slab = t_len * MOBA_TOPK * ppb
    full = pl.BlockSpec(qt4.shape, lambda i, pt, sl: (0, 0, 0, 0))
    grid_spec = pltpu.PrefetchScalarGridSpec(
        num_scalar_prefetch=2,
        grid=(1,),
        in_specs=[full, full, full, pl.BlockSpec(memory_space=pl.ANY),
                  pl.BlockSpec(memory_space=pl.ANY)],
        out_specs=full,
        scratch_shapes=[pltpu.VMEM((ATTEND_SLOTS, n_slab, hd, page), F32),
                        pltpu.VMEM((ATTEND_SLOTS, n_slab, hd, page), F32),
                        pltpu.SemaphoreType.DMA((ATTEND_SLOTS,)),
                        pltpu.VMEM((t_len * 8, page), F32)],
    )
    return pl.pallas_call(
        functools.partial(_moba_attend_kernel, pages_per_block=ppb, n_pages=n_pages),
        grid_spec=grid_spec,
        out_shape=jax.ShapeDtypeStruct(qt4.shape, F32),
        compiler_params=_params("arbitrary"),
        name="moba_attend",
    )(page_table.reshape(-1), sel.reshape(-1), qt4, knt4, vnt4, cache_kt, cache_vt)


def _rows8(x):
    return x.reshape(x.shape[0] // 8, 8, x.shape[1])


def _softplus(x):
    return jnp.maximum(x, 0.0) + jnp.log1p(jnp.exp(-jnp.abs(x)))


def _ssd_kernel(*refs, d_inner, rows_valid, has_halo):
    if has_halo:
        (zx_ref, halo_ref, dt_ref, s0_ref, cw_ref, cb_ref, dtb_ref, alog_ref, dskip_ref, e_ref,
         y_ref, cs_ref, fs_ref, st_ref, xin_ref, dtin_ref, y_scr) = refs
        hist_ref = None
    else:
        (zx_ref, hist_ref, dt_ref, s0_ref, cw_ref, cb_ref, dtb_ref, alog_ref, dskip_ref, e_ref,
         y_ref, cs_ref, fs_ref, st_ref, xin_ref, dtin_ref, y_scr) = refs
        halo_ref = None
    c = pl.program_id(1)
    n_chunks = pl.num_programs(1)
    cl = SSM_CHUNK
    rows_blk = zx_ref.shape[1]
    conv_dim = cw_ref.shape[2]
    n_st = SSM_STATE
    hp = SSM_HEAD_DIM
    heads_per_group = d_inner // hp // SSM_GROUPS
    gw = heads_per_group * hp

    @pl.when(c == 0)
    def _():
        st_ref[...] = s0_ref[0].T

    if has_halo:
        xin_ref[0:8, :] = halo_ref[0, :, d_inner:d_inner + conv_dim] * (c != 0).astype(F32)
    else:
        xin_ref[0:8, :] = jnp.zeros((8, conv_dim), F32)
        xin_ref[5:8, :] = hist_ref[0]
    if rows_blk == cl:
        xin_ref[8:8 + cl, :] = zx_ref[0, :, d_inner:d_inner + conv_dim]
        dt_raw = dt_ref[0]
    else:
        xin_ref[8:8 + cl, :] = jnp.zeros((cl, conv_dim), F32)
        dtin_ref[...] = jnp.zeros(dtin_ref.shape, F32)
        xin_ref[8:8 + rows_blk, :] = zx_ref[0, :, d_inner:d_inner + conv_dim]
        dtin_ref[0:rows_blk, :] = dt_ref[0]
        dt_raw = dtin_ref[...]

    xbc = xin_ref[8:8 + cl, :]
    taps = [_rows8(xin_ref[5 + j:5 + j + cl, :]) * cw_ref[j][None] for j in range(3)]
    conv = taps[0] + taps[1] + taps[2] + _rows8(xbc) * cw_ref[3][None] + cb_ref[...][None]
    conv = conv.reshape(cl, conv_dim)
    act = conv * jax.nn.sigmoid(conv)

    @pl.when(c == n_chunks - 1)
    def _():
        cs_ref[0] = xbc[rows_valid - 3:rows_valid, :]

    dt = _softplus(dt_raw + dtb_ref[...])
    if rows_valid < cl:
        dt = jnp.where(lax.broadcasted_iota(jnp.int32, dt.shape, 0) < rows_valid, dt, 0.0)
    a = -jnp.exp(alog_ref[...])
    dta = dt * a
    r_i = lax.broadcasted_iota(jnp.int32, (cl, cl), 0)
    c_i = lax.broadcasted_iota(jnp.int32, (cl, cl), 1)
    causal = c_i <= r_i
    first_head_lanes = lax.broadcasted_iota(jnp.int32, (cl, 2 * hp), 1) < hp
    tril = jnp.where(causal, 1.0, 0.0)
    acum = jnp.dot(tril, dta, preferred_element_type=F32, precision=HIGHEST)
    acum_t = acum.T
    alast = acum[cl - 1:cl, :]
    narrow = jnp.concatenate([dt, jnp.exp(acum), jnp.exp(alast - acum),
                              jnp.broadcast_to(jnp.exp(alast), (8, alast.shape[1]))], axis=0)
    hi = narrow.astype(BF16)
    mid = (narrow - hi.astype(F32)).astype(BF16)
    wide = _dot(jnp.concatenate([hi, mid], axis=1), e_ref[...])
    dt_w = wide[0:cl]
    eac_w = wide[cl:2 * cl]
    dte_w = wide[2 * cl:3 * cl]
    cd_w = wide[3 * cl:3 * cl + 8]

    xs = act[:, 0:d_inner]
    xdt = xs * dt_w
    xdt_b = xdt.astype(BF16)
    xdtd_b = (xdt * dte_w).astype(BF16)
    for g in range(SSM_GROUPS):
        b_g = act[:, d_inner + g * n_st:d_inner + (g + 1) * n_st]
        c_g = act[:, d_inner + (SSM_GROUPS + g) * n_st:d_inner + (SSM_GROUPS + g + 1) * n_st]
        c_gb = c_g.astype(BF16)
        cb = _dot_nt(c_gb, b_g.astype(BF16))
        s_g = st_ref[:, g * gw:(g + 1) * gw]
        y_off = _dot(c_gb, s_g.astype(BF16)) * eac_w[:, g * gw:(g + 1) * gw]
        for r in range(0, heads_per_group, 2):
            h = g * heads_per_group + r
            stacked = []
            for hh in (h, h + 1):
                seg = acum[:, hh:hh + 1] - acum_t[hh:hh + 1, :]
                decay = jnp.exp(jnp.where(causal, seg, NEG))
                stacked.append((cb * decay).astype(BF16))
            both = _dot(jnp.concatenate(stacked, axis=0), xdt_b[:, h * hp:(h + 2) * hp])
            y_d = jnp.where(first_head_lanes, both[0:cl], both[cl:2 * cl])
            y_scr[:, h * hp:(h + 2) * hp] = y_d + y_off[:, r * hp:(r + 2) * hp]
        kept = (_rows8(s_g) * cd_w[:, g * gw:(g + 1) * gw][None]).reshape(n_st, gw)
        st_ref[:, g * gw:(g + 1) * gw] = kept + _dot(b_g.T.astype(BF16),
                                                     xdtd_b[:, g * gw:(g + 1) * gw])
    y = y_scr[...] + (_rows8(xs) * dskip_ref[...][None]).reshape(cl, d_inner)
    y_ref[0] = y[0:rows_blk, :]

    @pl.when(c == n_chunks - 1)
    def _():
        fs_ref[0] = st_ref[...].T


def ssd_mixer(zx3, dt3, hist, state0, conv_w, conv_b, dt_bias, a_log, d_skip, d_inner):
    nseq, t_len, _ = zx3.shape
    conv_dim = conv_w.shape[1]
    n_heads = d_inner // SSM_HEAD_DIM
    cl = SSM_CHUNK
    has_halo = hist is None
    if has_halo:
        rows_blk, rows_valid, n_chunks = cl, cl, t_len // cl
    else:
        rows_blk, rows_valid, n_chunks = t_len, t_len, 1
    pad = lambda v: jnp.pad(v.astype(F32), (0, 128 - n_heads)).reshape(1, 128)
    expand = (jnp.arange(256)[:, None] % 128
              == (jnp.arange(d_inner)[None, :] // SSM_HEAD_DIM)).astype(BF16)
    rows8 = lambda v: jnp.broadcast_to(v[..., None, :], v.shape[:-1] + (8, v.shape[-1]))
    d_wide = rows8(jnp.repeat(d_skip.astype(F32), SSM_HEAD_DIM))
    const2 = lambda s, c: (0, 0)
    if has_halo:
        second = pl.BlockSpec((1, 8, zx3.shape[2]),
                              lambda s, c: (s, jnp.maximum(c * (cl // 8) - 1, 0), 0))
        second_arg = zx3
    else:
        second = pl.BlockSpec((1,) + hist.shape[1:], lambda s, c: (s, 0, 0))
        second_arg = hist
    outs = pl.pallas_call(
        functools.partial(_ssd_kernel, d_inner=d_inner, rows_valid=rows_valid, has_halo=has_halo),
        grid=(nseq, n_chunks),
        in_specs=[pl.BlockSpec((1, rows_blk, zx3.shape[2]), lambda s, c: (s, c, 0)),
                  second,
                  pl.BlockSpec((1, rows_blk, 128), lambda s, c: (s, c, 0)),
                  pl.BlockSpec((1, d_inner, SSM_STATE), lambda s, c: (s, 0, 0)),
                  pl.BlockSpec((conv_w.shape[0], 8, conv_dim), lambda s, c: (0, 0, 0)),
                  pl.BlockSpec((8, conv_dim), const2),
                  pl.BlockSpec((1, 128), const2),
                  pl.BlockSpec((1, 128), const2),
                  pl.BlockSpec((8, d_inner), const2),
                  pl.BlockSpec((256, d_inner), const2)],
        out_specs=[pl.BlockSpec((1, rows_blk, d_inner), lambda s, c: (s, c, 0)),
                   pl.BlockSpec((1, 3, conv_dim), lambda s, c: (s, 0, 0)),
                   pl.BlockSpec((1, d_inner, SSM_STATE), lambda s, c: (s, 0, 0))],
        out_shape=[jax.ShapeDtypeStruct((nseq, t_len, d_inner), F32),
                   jax.ShapeDtypeStruct((nseq, 3, conv_dim), F32),
                   jax.ShapeDtypeStruct((nseq, d_inner, SSM_STATE), F32)],
        scratch_shapes=[pltpu.VMEM((SSM_STATE, d_inner), F32),
                        pltpu.VMEM((cl + 8, conv_dim), F32),
                        pltpu.VMEM((cl, 128), F32),
                        pltpu.VMEM((cl, d_inner), F32)],
        compiler_params=_params("parallel", "arbitrary"),
        name="ssd_mixer",
    )(zx3, second_arg, dt3, state0, rows8(conv_w), rows8(conv_b), pad(dt_bias), pad(a_log),
      d_wide, expand)
    return outs


L1_POST_ROW_TILE = 512


def _l1_post_kernel(y_ref, z_ref, x_ref, gw_ref, wo_ref, pw_ref, pre_ref, wg_ref, wu_ref, wd_ref,
                    post_ref, o_ref):
    z = z_ref[...]
    g = y_ref[...] * (z * jax.nn.sigmoid(z))
    o = _dot(_rms(g, gw_ref[...]).astype(BF16), wo_ref[...])
    x_mid = x_ref[...] + _rms(o, pw_ref[...])
    o_ref[...] = _ffn_apply(x_mid, pre_ref, wg_ref, wu_ref, wd_ref, post_ref)


def l1_post(y, zx, x, gate_w, wo, post_w, ffn_w):
    m, d = x.shape
    d_inner = y.shape[1]
    tm = _row_tile(m, L1_POST_ROW_TILE)
    const = lambda i: (0, 0)
    ffn_args, ffn_specs = _ffn_operands(ffn_w, d)
    return pl.pallas_call(
        _l1_post_kernel,
        grid=(m // tm,),
        in_specs=[pl.BlockSpec((tm, d_inner), lambda i: (i, 0)),
                  pl.BlockSpec((tm, d_inner), lambda i: (i, 0)),
                  pl.BlockSpec((tm, d), lambda i: (i, 0)),
                  pl.BlockSpec((1, d_inner), const),
                  pl.BlockSpec(wo.shape, const, pipeline_mode=pl.Buffered(1)),
                  pl.BlockSpec((1, d), const)] + ffn_specs,
        out_specs=pl.BlockSpec((tm, d), lambda i: (i, 0)),
        out_shape=jax.ShapeDtypeStruct((m, d), F32),
        compiler_params=_params("parallel"),
        name="l1_post",
    )(y, zx, x, gate_w.reshape(1, d_inner), wo, post_w.reshape(1, d), *ffn_args)


def kernel(x_prompt, x_sample, cache_k, cache_v, page_table, state_conv_a, state_conv_ssm, state_ssm, l0_norm_mix_pre, l0_w_in, l0_conv_w, l0_w_out, l0_norm_mix_post, l0_norm_ffn_pre, l0_ffn_gate, l0_ffn_up, l0_ffn_down, l0_norm_ffn_post, l1_norm_mix_pre, l1_w_in, l1_conv_w, l1_conv_b, l1_dt_bias, l1_a_log, l1_d_skip, l1_norm_gate, l1_w_out, l1_norm_mix_post, l1_norm_ffn_pre, l1_ffn_gate, l1_ffn_up, l1_ffn_down, l1_norm_ffn_post):
    bp, s_len, d = x_prompt.shape
    bd, t_len, _ = x_sample.shape
    n_heads, page, hd = cache_k.shape[1:]
    width = n_heads * hd
    assert (page_table.shape[1] * page) % MOBA_BLOCK == 0
    conv_dim = l1_conv_w.shape[1]
    ssm_heads = l1_dt_bias.shape[0]
    d_inner = ssm_heads * SSM_HEAD_DIM

    bf = lambda w: w.astype(BF16)
    xp = x_prompt.reshape(bp * s_len, d)
    xs = x_sample.reshape(bd * t_len, d)

    w_in0 = bf(l0_w_in)
    w_out0 = bf(l0_w_out)
    hp_ = norm_matmul(xp, l0_norm_mix_pre, w_in0)
    hs_ = norm_matmul(xs, l0_norm_mix_pre, w_in0)

    yb_p, kp_t, vp_t = moba_prompt(hp_.reshape(bp, s_len, -1), n_heads, hd, page)
    k_prompt = jnp.swapaxes(kp_t, 3, 4)
    v_prompt = jnp.swapaxes(vp_t, 3, 4)
    cache_kt = jnp.swapaxes(cache_k, 2, 3)
    cache_vt = jnp.swapaxes(cache_v, 2, 3)
    ffn0 = (l0_norm_ffn_pre, bf(l0_ffn_gate), bf(l0_ffn_up), bf(l0_ffn_down), l0_norm_ffn_post)
    xp, conv_a_prompt, block_means = l0_post_prompt(
        hp_, yb_p.reshape(bp * s_len, width), xp, l0_conv_w, w_out0, l0_norm_mix_post, ffn0, s_len,
        page_table, cache_kt)

    qkv_t = hs_[:, 3 * width:].reshape(bd, t_len, 3, n_heads, hd).transpose(2, 0, 3, 4, 1)
    qt4, knt4, vnt4 = qkv_t[0], qkv_t[1], qkv_t[2]
    n_blk = page_table.shape[1] * page // MOBA_BLOCK
    sel = moba_select(block_means, qt4, n_blk)[:, :, :t_len * MOBA_TOPK]
    yb_s = moba_attend(page_table, sel, qt4, knt4, vnt4, cache_kt, cache_vt)
    yb_s = yb_s.transpose(0, 3, 1, 2).reshape(bd * t_len, width)
    hist_rows = jnp.pad(state_conv_a, ((0, 0), (0, t_len - state_conv_a.shape[1]), (0, 0)))
    xs, gcu_s = l0_post_sample(hs_, yb_s, xs, hist_rows.reshape(bd * t_len, width), l0_conv_w,
                               w_out0, l0_norm_mix_post, t_len)
    conv_a_sample = gcu_s.reshape(bd, t_len, width)[:, t_len - 2:]
    k_sample = knt4.transpose(0, 1, 3, 2)
    v_sample = vnt4.transpose(0, 1, 3, 2)

    xs = ffn(xs, ffn0)

    w_zx = bf(l1_w_in[:, :d_inner + conv_dim])
    w_dt = jnp.pad(l1_w_in[:, d_inner + conv_dim:], ((0, 0), (0, 128 - ssm_heads)))
    w_out1 = bf(l1_w_out)
    zx_p, dt_p = norm_matmul(xp, l1_norm_mix_pre, w_zx, w_dt)
    zx_s, dt_s = norm_matmul(xs, l1_norm_mix_pre, w_zx, w_dt)

    zeros_state = jnp.zeros((bp, d_inner, SSM_STATE), F32)
    y_p, conv_ssm_prompt, fs_p = ssd_mixer(zx_p.reshape(bp, s_len, -1), dt_p.reshape(bp, s_len, 128),
                                           None, zeros_state, l1_conv_w, l1_conv_b, l1_dt_bias,
                                           l1_a_log, l1_d_skip, d_inner)
    y_s, conv_ssm_sample, fs_s = ssd_mixer(zx_s.reshape(bd, t_len, -1), dt_s.reshape(bd, t_len, 128),
                                           state_conv_ssm, state_ssm.reshape(bd, d_inner, SSM_STATE),
                                           l1_conv_w, l1_conv_b, l1_dt_bias, l1_a_log, l1_d_skip,
                                           d_inner)
    ssm_prompt = fs_p.reshape(bp, ssm_heads, SSM_HEAD_DIM, SSM_STATE)
    ssm_sample = fs_s.reshape(bd, ssm_heads, SSM_HEAD_DIM, SSM_STATE)

    ffn1 = (l1_norm_ffn_pre, bf(l1_ffn_gate), bf(l1_ffn_up), bf(l1_ffn_down), l1_norm_ffn_post)
    xp = l1_post(y_p.reshape(bp * s_len, d_inner), zx_p, xp, l1_norm_gate, w_out1, l1_norm_mix_post,
                 ffn1)
    xs = l1_post(y_s.reshape(bd * t_len, d_inner), zx_s, xs, l1_norm_gate, w_out1, l1_norm_mix_post,
                 ffn1)

    return (xp.reshape(bp, s_len, d), xs.reshape(bd, t_len, d), k_prompt, v_prompt, k_sample,
            v_sample, conv_a_prompt, conv_a_sample, conv_ssm_prompt, conv_ssm_sample, ssm_prompt,
            ssm_sample)
```

```python
import functools

import jax
import jax.numpy as jnp
from jax import lax
from jax.experimental import pallas as pl
from jax.experimental.pallas import tpu as pltpu

F32 = jnp.float32
BF16 = jnp.bfloat16
HIGHEST = lax.Precision.HIGHEST

NORM_EPS = 1e-6
MOBA_BLOCK = 256
MOBA_TOPK = 3
SSM_CHUNK = 128
SSM_HEAD_DIM = 64
SSM_STATE = 128
SSM_GROUPS = 4
NEG = -1e30
LOG2E = 1.4426950408889634
ONES_ROWS = 16

VMEM_LIMIT_BYTES = 56 * 1024 * 1024
ROW_TILE = 512


def _params(*sem):
    return pltpu.CompilerParams(dimension_semantics=sem, vmem_limit_bytes=VMEM_LIMIT_BYTES)


def _rms(x, w):
    return x * lax.rsqrt(jnp.mean(x * x, axis=-1, keepdims=True) + NORM_EPS) * w


def _dot(a, b):
    return jnp.dot(a, b, preferred_element_type=F32)


def _dot_nt(a, b, precision=None):
    return lax.dot_general(a, b, (((1,), (1,)), ((), ())), preferred_element_type=F32,
                           precision=precision)


def _row_tile(m, tile=ROW_TILE):
    return tile if m % tile == 0 else m


PROJ_ROW_TILE = 512
PROJ_COL_CHUNK = 1024


def _norm_matmul_kernel(*refs, narrow):
    if narrow:
        x_ref, nw_ref, w_ref, whi_ref, wlo_ref, o_ref, o2_ref = refs
    else:
        x_ref, nw_ref, w_ref, o_ref = refs
    xn = _rms(x_ref[...], nw_ref[...])
    xh = xn.astype(BF16)
    n = w_ref.shape[1]
    for c0 in range(0, n, PROJ_COL_CHUNK):
        c1 = min(c0 + PROJ_COL_CHUNK, n)
        o_ref[:, c0:c1] = _dot(xh, w_ref[:, c0:c1])
    if narrow:
        xl = (xn - xh.astype(F32)).astype(BF16)
        o2_ref[...] = _dot(xh, whi_ref[...]) + _dot(xl, whi_ref[...]) + _dot(xh, wlo_ref[...])


def norm_matmul(x, nw, w_bf16, w_narrow=None):
    m, d = x.shape
    n = w_bf16.shape[1]
    tm = PROJ_ROW_TILE if m % PROJ_ROW_TILE == 0 else m
    const = lambda i: (0, 0)
    resident = lambda shape: pl.BlockSpec(shape, const, pipeline_mode=pl.Buffered(1))
    in_specs = [pl.BlockSpec((tm, d), lambda i: (i, 0)), pl.BlockSpec((1, d), const),
                resident((d, n))]
    out_specs = [pl.BlockSpec((tm, n), lambda i: (i, 0))]
    out_shape = [jax.ShapeDtypeStruct((m, n), F32)]
    args = [x, nw.reshape(1, d), w_bf16]
    if w_narrow is not None:
        n2 = w_narrow.shape[1]
        whi = w_narrow.astype(BF16)
        wlo = (w_narrow - whi.astype(F32)).astype(BF16)
        in_specs += [resident((d, n2)), resident((d, n2))]
        out_specs.append(pl.BlockSpec((tm, n2), lambda i: (i, 0)))
        out_shape.append(jax.ShapeDtypeStruct((m, n2), F32))
        args += [whi, wlo]
    outs = pl.pallas_call(
        functools.partial(_norm_matmul_kernel, narrow=w_narrow is not None),
        grid=(m // tm,),
        in_specs=in_specs,
        out_specs=out_specs,
        out_shape=out_shape,
        compiler_params=_params("parallel"),
        name="norm_matmul",
    )(*args)
    return outs if w_narrow is not None else outs[0]


FFN_CHUNK = 256
FFN_ROW_TILE = 1024


FFN_SIDE_CALLS = 2


def _ffn_apply(x, pre_ref, wg_ref, wu_ref, wd_ref, post_ref, side_task=None):
    h = _rms(x, pre_ref[...]).astype(BF16)
    n_chunks = wg_ref.shape[1] // FFN_CHUNK
    acc = jnp.zeros(x.shape, F32)
    for c in range(n_chunks):
        sl = slice(c * FFN_CHUNK, (c + 1) * FFN_CHUNK)
        g = _dot(h, wg_ref[:, sl])
        u = _dot(h, wu_ref[:, sl])
        if side_task is not None:
            side_task(FFN_SIDE_CALLS * c, FFN_SIDE_CALLS * n_chunks)
        a = (g * jax.nn.sigmoid(g) * u).astype(BF16)
        acc = acc + _dot(a, wd_ref[sl, :])
        if side_task is not None:
            side_task(FFN_SIDE_CALLS * c + 1, FFN_SIDE_CALLS * n_chunks)
    return x + _rms(acc, post_ref[...])


def _ffn_kernel(x_ref, pre_ref, wg_ref, wu_ref, wd_ref, post_ref, o_ref):
    o_ref[...] = _ffn_apply(x_ref[...], pre_ref, wg_ref, wu_ref, wd_ref, post_ref)


def _ffn_operands(ffn_w, d, const=lambda i: (0, 0)):
    pre_w, wg, wu, wd, post_w = ffn_w
    resident = lambda w: pl.BlockSpec(w.shape, const, pipeline_mode=pl.Buffered(1))
    args = [pre_w.reshape(1, d), wg, wu, wd, post_w.reshape(1, d)]
    specs = [pl.BlockSpec((1, d), const), resident(wg), resident(wu), resident(wd),
             pl.BlockSpec((1, d), const)]
    return args, specs


def ffn(x, ffn_w):
    m, d = x.shape
    tm = _row_tile(m, FFN_ROW_TILE)
    ffn_args, ffn_specs = _ffn_operands(ffn_w, d)
    return pl.pallas_call(
        _ffn_kernel,
        grid=(m // tm,),
        in_specs=[pl.BlockSpec((tm, d), lambda i: (i, 0))] + ffn_specs,
        out_specs=pl.BlockSpec((tm, d), lambda i: (i, 0)),
        out_shape=jax.ShapeDtypeStruct((m, d), F32),
        compiler_params=_params("parallel"),
        name="ffn",
    )(x, *ffn_args)


def _moba_work_items(nb):
    items = [(qi, i, int(i == qi // 2)) for qi in range(nb) for i in range(qi // 2 + 1)]
    return [list(col) for col in zip(*items)]


def _moba_prompt_kernel(tab_ref, q_ref, k_ref, v_ref, o_ref, kp_ref, vp_ref, vt_ref, kb_ref,
                        qb_ref, sel_ref):
    s_len = k_ref.shape[1]
    nb = s_len // MOBA_BLOCK
    n_items = tab_ref.shape[0] // 3
    npages = kp_ref.shape[1]
    hd = kp_ref.shape[3]
    page = kp_ref.shape[4]
    ppb = MOBA_BLOCK // page
    scale = hd ** -0.5

    kb_ref[...] = k_ref[0].astype(BF16)
    qb_ref[...] = (q_ref[0] * (scale * LOG2E)).astype(BF16)
    km = jnp.mean(k_ref[0].reshape(nb, MOBA_BLOCK, 2 * hd), axis=1)
    blk = lax.broadcasted_iota(jnp.int32, (nb, s_len), 0)
    own = lax.broadcasted_iota(jnp.int32, (nb, s_len), 1) // MOBA_BLOCK
    for hh in range(2):
        lo = hh * hd
        gate = _dot_nt(km[:, lo:lo + hd], q_ref[0, :, lo:lo + hd], precision=HIGHEST)
        cnt = jnp.zeros((nb, s_len), F32)
        for jp in range(nb - 1):
            gj = gate[jp:jp + 1, :]
            beats = ((gj > gate) | ((gj == gate) & (jp < blk))) & (jp < own)
            cnt = cnt + jnp.where(beats, 1.0, 0.0)
        picked = ((cnt < MOBA_TOPK) & (blk < own)) | (blk == own)
        limit_shift = jnp.where(picked, 0, -4 * MOBA_BLOCK)
        for qb in range(nb):
            sel_ref[hh, qb] = limit_shift[:, qb * MOBA_BLOCK:(qb + 1) * MOBA_BLOCK]
    for pg in range(npages):
        rows = slice(pg * page, (pg + 1) * page)
        kt = k_ref[0, rows, :].T
        vt = v_ref[0, rows, :].T
        kp_ref[0, pg] = kt.reshape(2, hd, page)
        vp_ref[0, pg] = vt.reshape(2, hd, page)
        cols = slice((pg % ppb) * page, (pg % ppb + 1) * page)
        for hh in range(2):
            vt_ref[pg // ppb, hh, 0:hd, cols] = vt[hh * hd:(hh + 1) * hd].astype(BF16)
    ones_row = jnp.where(lax.broadcasted_iota(jnp.int32, (ONES_ROWS, MOBA_BLOCK), 0) == 0, 1.0, 0.0)
    for j in range(nb):
        for hh in range(2):
            vt_ref[j, hh, hd:hd + ONES_ROWS, :] = ones_row.astype(BF16)

    row = lax.broadcasted_iota(jnp.int32, (MOBA_BLOCK, MOBA_BLOCK), 0)
    col_row = lax.broadcasted_iota(jnp.int32, (1, MOBA_BLOCK), 1)

    def scores(qi, j, hh):
        jc = jnp.minimum(j, qi)
        q_tile = qb_ref[pl.ds(pl.multiple_of(qi * MOBA_BLOCK, MOBA_BLOCK), MOBA_BLOCK),
                        hh * hd:(hh + 1) * hd]
        k_tile = kb_ref[pl.ds(pl.multiple_of(jc * MOBA_BLOCK, MOBA_BLOCK), MOBA_BLOCK),
                        hh * hd:(hh + 1) * hd]
        shift = jnp.where(j < qi, MOBA_BLOCK, jnp.where(j == qi, 0, -4 * MOBA_BLOCK))
        limit = col_row + (1 + shift) + sel_ref[hh, qi, pl.ds(jc, 1), :]
        return jnp.where(row < limit, _dot_nt(k_tile, q_tile), -jnp.inf)

    def weighted_values(qi, i, state):
        ja = jnp.minimum(2 * i, qi)
        jb = jnp.minimum(2 * i + 1, qi)
        return [alpha * acc + _dot(vt_ref[ja, hh], pa) + _dot(vt_ref[jb, hh], pb)
                for hh, (_, acc, alpha, pa, pb) in enumerate(state)]

    def write_block(qi, accs):
        o_ref[0, pl.ds(pl.multiple_of(qi * MOBA_BLOCK, MOBA_BLOCK), MOBA_BLOCK), :] = (
            jnp.concatenate([acc[0:hd] / acc[hd:hd + 1] for acc in accs], axis=0).T)

    def body(t, state):
        qi, i = tab_ref[t], tab_ref[n_items + t]
        tiles = [scores(qi, 2 * i + b, hh) for hh in range(2) for b in range(2)]
        prev = jnp.maximum(t - 1, 0)
        qi_prev = tab_ref[prev]
        accs = weighted_values(qi_prev, tab_ref[n_items + prev], state)
        fresh = (i == 0).astype(F32)
        new = []
        for hh in range(2):
            m = state[hh][0]
            m = m + fresh * (NEG - m)
            sa, sb = tiles[2 * hh], tiles[2 * hh + 1]
            m_new = jnp.maximum(m, jnp.max(jnp.maximum(sa, sb), axis=0, keepdims=True))
            alpha = jnp.exp2(m - m_new) * (1.0 - fresh)
            new.append((m_new, accs[hh], alpha, jnp.exp2(sa - m_new).astype(BF16),
                        jnp.exp2(sb - m_new).astype(BF16)))

        @pl.when((tab_ref[2 * n_items + prev] == 1) & (t > 0))
        def _():
            write_block(qi_prev, accs)

        return tuple(new)

    no_p = jnp.zeros((MOBA_BLOCK, MOBA_BLOCK), BF16)
    init = (jnp.full((1, MOBA_BLOCK), NEG, F32), jnp.zeros((hd + ONES_ROWS, MOBA_BLOCK), F32),
            jnp.zeros((1, MOBA_BLOCK), F32), no_p, no_p)
    state = lax.fori_loop(0, n_items, body, (init, init))
    write_block(nb - 1, weighted_values(nb - 1, (nb - 1) // 2, state))


def moba_prompt(h3, n_heads, hd, page):
    bsz, s_len, _ = h3.shape
    width = n_heads * hd
    lanes = 2 * hd
    pairs = n_heads // 2
    q0 = 3 * width // lanes
    nb = s_len // MOBA_BLOCK
    pages_shape = jax.ShapeDtypeStruct((bsz, s_len // page, n_heads, hd, page), F32)
    page_spec = pl.BlockSpec((1, s_len // page, 2, hd, page), lambda b, p, tab: (b, 0, p, 0, 0))
    items = jnp.asarray(sum(_moba_work_items(nb), []), jnp.int32)
    grid_spec = pltpu.PrefetchScalarGridSpec(
        num_scalar_prefetch=1,
        grid=(bsz, pairs),
        in_specs=[pl.BlockSpec((1, s_len, lanes), lambda b, p, tab: (b, 0, q0 + p)),
                  pl.BlockSpec((1, s_len, lanes), lambda b, p, tab: (b, 0, q0 + pairs + p)),
                  pl.BlockSpec((1, s_len, lanes), lambda b, p, tab: (b, 0, q0 + 2 * pairs + p))],
        out_specs=[pl.BlockSpec((1, s_len, lanes), lambda b, p, tab: (b, 0, p)),
                   page_spec, page_spec],
        scratch_shapes=[pltpu.VMEM((nb, 2, hd + ONES_ROWS, MOBA_BLOCK), BF16),
                        pltpu.VMEM((s_len, lanes), BF16),
                        pltpu.VMEM((s_len, lanes), BF16),
                        pltpu.VMEM((2, nb, nb, MOBA_BLOCK), jnp.int32)],
    )
    return pl.pallas_call(
        _moba_prompt_kernel,
        grid_spec=grid_spec,
        out_shape=[jax.ShapeDtypeStruct((bsz, s_len, width), F32), pages_shape, pages_shape],
        compiler_params=_params("parallel", "parallel"),
        name="moba_prompt",
    )(items, h3, h3, h3)


def _conv3_mix(gb, gcu, prev2, prev1, li, cw):
    s1 = jnp.where(li == 0, prev1[0], pltpu.roll(gcu, 1, 0))
    s2 = pltpu.roll(gcu, 2, 0)
    s2 = jnp.where(li == 0, prev2[0], jnp.where(li == 1, prev2[1], s2))
    conv = cw[0:1] * s2 + cw[1:2] * s1 + cw[2:3] * gcu
    return gb * conv


def _out_proj_residual(ya, yb, x, wo_ref, pw):
    half = ya.shape[1]
    y = _dot(ya.astype(BF16), wo_ref[0:half, :]) + _dot(yb.astype(BF16), wo_ref[half:, :])
    return x + _rms(y, pw)


CHUNKS_IN_FLIGHT = 3


def _cache_block_means(seq, n_pages, pt_ref, ck_ref, km_ref, buf, sem):
    _, n_h, hd, _ = km_ref.shape
    page = ck_ref.shape[3]
    parts = CHUNKS_IN_FLIGHT + 1
    ppc = buf.shape[0] // parts
    pages_per_block = MOBA_BLOCK // page
    lane3 = lax.broadcasted_iota(jnp.int32, (n_h, hd, 128), 2)

    def pages_of(c):
        return [p for p in range(c * ppc, (c + 1) * ppc) if p < n_pages]

    def copy(p):
        part = (p // ppc) % parts
        return pltpu.make_async_copy(ck_ref.at[pt_ref[seq * n_pages + p]],
                                     buf.at[part * ppc + p % ppc], sem.at[part])

    def start(c):
        for p in pages_of(c):
            copy(p).start()

    def run(c, n_chunks):
        if c + CHUNKS_IN_FLIGHT < n_chunks:
            start(c + CHUNKS_IN_FLIGHT)
        pages = pages_of(c)
        for p in pages:
            copy(p).wait()
        half = (c % parts) * ppc
        blocks = len(pages) // pages_per_block
        if blocks == 0:
            return
        sums = []
        for u in range(blocks):
            acc = buf[half + u * pages_per_block]
            for jj in range(1, pages_per_block):
                acc = acc + buf[half + u * pages_per_block + jj]
            sums.append(acc)
        means = jnp.sum(jnp.concatenate(sums, axis=0), axis=-1, keepdims=True) * (1.0 / MOBA_BLOCK)
        km = km_ref[0]
        first = c * ppc // pages_per_block
        for u in range(blocks):
            km = jnp.where(lane3 == first + u, means[u * n_h:(u + 1) * n_h], km)
        km_ref[0] = km

    return start, run


def _l0_post_prompt_kernel(pt_ref, gb_ref, gc_ref, u_ref, gch_ref, uh_ref, yb_ref, x_ref, cw_ref,
                           wo_ref, pw_ref, pre_ref, wg_ref, wu_ref, wd_ref, post_ref, ck_ref,
                           xo_ref, st_ref, km_ref, buf, sem, *, tiles_per_seq, n_pages):
    i = pl.program_id(0)
    tm = gb_ref.shape[0]
    start, run = _cache_block_means(i, n_pages, pt_ref, ck_ref, km_ref, buf, sem)
    km_ref[...] = jnp.zeros(km_ref.shape, F32)
    for c in range(CHUNKS_IN_FLIGHT):
        start(c)

    gcu = gc_ref[...] * u_ref[...]
    halo = gch_ref[...] * uh_ref[...]
    halo = halo * (i % tiles_per_seq != 0).astype(F32)
    li = lax.broadcasted_iota(jnp.int32, gcu.shape, 0)
    ya = _conv3_mix(gb_ref[...], gcu, (halo[6:7], halo[7:8]), (halo[7:8],), li, cw_ref[...])
    x_mid = _out_proj_residual(ya, yb_ref[...], x_ref[...], wo_ref, pw_ref[...])
    xo_ref[...] = _ffn_apply(x_mid, pre_ref, wg_ref, wu_ref, wd_ref, post_ref, side_task=run)

    @pl.when(i % tiles_per_seq == tiles_per_seq - 1)
    def _():
        st_ref[0] = gcu[tm - 2:tm, :]


def l0_post_prompt(h2, yb, x, conv_w, wo, post_w, ffn_w, seq_len, page_table, cache_kt):
    m, d = x.shape
    width = yb.shape[1]
    tm = ROW_TILE
    tiles_per_seq = seq_len // tm
    bd, n_pages = page_table.shape
    _, n_h, hd, page = cache_kt.shape
    n_chunks = FFN_SIDE_CALLS * (ffn_w[1].shape[1] // FFN_CHUNK)
    ppb = MOBA_BLOCK // page
    pages_per_chunk = -(-n_pages // (n_chunks * ppb)) * ppb
    assert m // tm == bd and n_pages // ppb <= 128
    halo_idx = lambda i: jnp.maximum(i * (tm // 8) - 1, 0)
    const = lambda i, pt: (0, 0)
    ffn_args, ffn_specs = _ffn_operands(ffn_w, d, const)
    grid_spec = pltpu.PrefetchScalarGridSpec(
        num_scalar_prefetch=1,
        grid=(m // tm,),
        in_specs=[pl.BlockSpec((tm, width), lambda i, pt: (i, 0)),
                  pl.BlockSpec((tm, width), lambda i, pt: (i, 1)),
                  pl.BlockSpec((tm, width), lambda i, pt: (i, 2)),
                  pl.BlockSpec((8, width), lambda i, pt: (halo_idx(i), 1)),
                  pl.BlockSpec((8, width), lambda i, pt: (halo_idx(i), 2)),
                  pl.BlockSpec((tm, width), lambda i, pt: (i, 0)),
                  pl.BlockSpec((tm, d), lambda i, pt: (i, 0)),
                  pl.BlockSpec(conv_w.shape, const),
                  pl.BlockSpec(wo.shape, const, pipeline_mode=pl.Buffered(1)),
                  pl.BlockSpec((1, d), const)] + ffn_specs + [pl.BlockSpec(memory_space=pl.ANY)],
        out_specs=[pl.BlockSpec((tm, d), lambda i, pt: (i, 0)),
                   pl.BlockSpec((1, 2, width), lambda i, pt: (i // tiles_per_seq, 0, 0)),
                   pl.BlockSpec((1, n_h, hd, 128), lambda i, pt: (i, 0, 0, 0))],
        scratch_shapes=[pltpu.VMEM(((CHUNKS_IN_FLIGHT + 1) * pages_per_chunk, n_h, hd, page), F32),
                        pltpu.SemaphoreType.DMA((CHUNKS_IN_FLIGHT + 1,))],
    )
    return pl.pallas_call(
        functools.partial(_l0_post_prompt_kernel, tiles_per_seq=tiles_per_seq, n_pages=n_pages),
        grid_spec=grid_spec,
        out_shape=[jax.ShapeDtypeStruct((m, d), F32),
                   jax.ShapeDtypeStruct((m // seq_len, 2, width), F32),
                   jax.ShapeDtypeStruct((bd, n_h, hd, 128), F32)],
        compiler_params=_params("arbitrary"),
        name="l0_post_prompt",
    )(page_table.reshape(-1), h2, h2, h2, h2, h2, yb, x, conv_w, wo, post_w.reshape(1, d), *ffn_args,
      cache_kt)


def _l0_post_sample_kernel(gb_ref, gc_ref, u_ref, hist_ref, yb_ref, x_ref, cw_ref, wo_ref, pw_ref,
                           xo_ref, gcu_ref, *, seq_len):
    gcu = gc_ref[...] * u_ref[...]
    rows = gcu.shape[0]
    t = lax.broadcasted_iota(jnp.int32, gcu.shape, 0) % seq_len
    hist = hist_ref[...]
    hist_next = pltpu.roll(hist, rows - 1, 0)
    ya = _conv3_mix(gb_ref[...], gcu, (hist, hist), (hist_next,), t, cw_ref[...])
    xo_ref[...] = _out_proj_residual(ya, yb_ref[...], x_ref[...], wo_ref, pw_ref[...])
    gcu_ref[...] = gcu


def l0_post_sample(h2, yb, x, hist_rows, conv_w, wo, post_w, seq_len):
    m, d = x.shape
    width = yb.shape[1]
    const = lambda i: (0, 0)
    return pl.pallas_call(
        functools.partial(_l0_post_sample_kernel, seq_len=seq_len),
        grid=(1,),
        in_specs=[pl.BlockSpec((m, width), lambda i: (0, 0)),
                  pl.BlockSpec((m, width), lambda i: (0, 1)),
                  pl.BlockSpec((m, width), lambda i: (0, 2)),
                  pl.BlockSpec((m, width), const),
                  pl.BlockSpec((m, width), const),
                  pl.BlockSpec((m, d), const),
                  pl.BlockSpec(conv_w.shape, const),
                  pl.BlockSpec(wo.shape, const),
                  pl.BlockSpec((1, d), const)],
        out_specs=[pl.BlockSpec((m, d), const), pl.BlockSpec((m, width), const)],
        out_shape=[jax.ShapeDtypeStruct((m, d), F32), jax.ShapeDtypeStruct((m, width), F32)],
        compiler_params=_params("arbitrary"),
        name="l0_post_sample",
    )(h2, h2, h2, hist_rows, yb, x, conv_w, wo, post_w.reshape(1, d))


ATTEND_SLOTS = 4


def _moba_select_kernel(km_ref, qt_ref, sel_ref, *, n_blk):
    _, n_h, hd, t_len = qt_ref.shape
    lane = lax.broadcasted_iota(jnp.int32, (n_h, 128), 1)
    sub = lax.broadcasted_iota(jnp.int32, (n_h, 128), 0)
    out = jnp.zeros((n_h, 128), jnp.int32)
    for t in range(t_len):
        gate = jnp.full((n_h, 128), -jnp.inf, F32)
        for h in range(n_h):
            g_h = jnp.sum(km_ref[0, h] * qt_ref[0, h][:, t:t + 1], axis=0, keepdims=True)
            gate = jnp.where(sub == h, g_h, gate)
        gate = jnp.where(lane < n_blk, gate, -jnp.inf)
        for r in range(MOBA_TOPK):
            best = jnp.max(gate, axis=-1, keepdims=True)
            idx = jnp.min(jnp.where(gate == best, lane, 128), axis=-1, keepdims=True)
            out = jnp.where(lane == t * MOBA_TOPK + r, idx, out)
            gate = jnp.where(lane == idx, -jnp.inf, gate)
    sel_ref[0] = out


def moba_select(block_means, qt4, n_blk):
    bd, n_h, hd, t_len = qt4.shape
    assert n_blk <= 128 and t_len * MOBA_TOPK <= 128
    return pl.pallas_call(
        functools.partial(_moba_select_kernel, n_blk=n_blk),
        grid=(bd,),
        in_specs=[pl.BlockSpec((1, n_h, hd, 128), lambda b: (b, 0, 0, 0)),
                  pl.BlockSpec((1, n_h, hd, t_len), lambda b: (b, 0, 0, 0))],
        out_specs=pl.BlockSpec((1, n_h, 128), lambda b: (b, 0, 0)),
        out_shape=jax.ShapeDtypeStruct((bd, n_h, 128), jnp.int32),
        compiler_params=_params("parallel"),
        name="moba_select",
    )(block_means, qt4)


def _moba_attend_kernel(pt_ref, sel_ref, qt_ref, knt_ref, vnt_ref, ck_ref, cv_ref, o_ref,
                        kbuf, vbuf, sem, s_ref, *, pages_per_block, n_pages):
    bd, n_h, hd, t_len = qt_ref.shape
    page = ck_ref.shape[3]
    per_t = MOBA_TOPK * pages_per_block
    scale = hd ** -0.5
    step = lax.broadcasted_iota(jnp.int32, (1, t_len), 1)
    step_col = lax.broadcasted_iota(jnp.int32, (hd, t_len), 1)

    def copies(pair, slot):
        b = pair // n_h
        h = pair % n_h
        out = []
        for t in range(t_len):
            for r in range(MOBA_TOPK):
                blk = sel_ref[(b * n_h + h) * (t_len * MOBA_TOPK) + t * MOBA_TOPK + r]
                for jj in range(pages_per_block):
                    phys = pt_ref[b * n_pages + blk * pages_per_block + jj]
                    s = t * per_t + r * pages_per_block + jj
                    out.append(pltpu.make_async_copy(ck_ref.at[phys, h], kbuf.at[slot, s],
                                                     sem.at[slot]))
                    out.append(pltpu.make_async_copy(cv_ref.at[phys, h], vbuf.at[slot, s],
                                                     sem.at[slot]))
        return out

    for ahead in range(ATTEND_SLOTS - 1):
        for c in copies(ahead, ahead):
            c.start()

    def per_pair(pair, _):
        slot = pair % ATTEND_SLOTS
        b = pair // n_h
        h = pair % n_h
        nxt = pair + ATTEND_SLOTS - 1

        @pl.when(nxt < bd * n_h)
        def _():
            for c in copies(nxt, nxt % ATTEND_SLOTS):
                c.start()

        for c in copies(pair, slot):
            c.wait()

        q_all = qt_ref[b, h] * scale
        k_new = knt_ref[b, h]
        v_new = vnt_ref[b, h]
        for t in range(t_len):
            qc = q_all[:, t:t + 1]
            for s in range(per_t):
                s_ref[t * 8 + s:t * 8 + s + 1, :] = jnp.sum(kbuf[slot, t * per_t + s] * qc, axis=0,
                                                             keepdims=True)
        out = jnp.zeros((hd, t_len), F32)
        for t in range(t_len):
            qc = q_all[:, t:t + 1]
            s_sel = s_ref[t * 8:(t + 1) * 8, :]
            s_own = jnp.where(step <= t, jnp.sum(k_new * qc, axis=0, keepdims=True), NEG)
            m = jnp.maximum(jnp.max(s_own, axis=-1, keepdims=True),
                            jnp.max(jnp.max(s_sel, axis=-1, keepdims=True), axis=0, keepdims=True))
            p_own = jnp.exp(s_own - m)
            p_sel = jnp.exp(s_sel - m)
            l = (jnp.sum(p_own, axis=-1, keepdims=True)
                 + jnp.sum(jnp.sum(p_sel, axis=-1, keepdims=True), axis=0, keepdims=True))
            acc = vbuf[slot, t * per_t] * p_sel[0:1, :]
            for s in range(1, per_t):
                acc = acc + vbuf[slot, t * per_t + s] * p_sel[s:s + 1, :]
            o_t = (jnp.sum(v_new * p_own, axis=-1, keepdims=True)
                   + jnp.sum(acc, axis=-1, keepdims=True)) / l
            out = jnp.where(step_col == t, o_t, out)
        o_ref[b, h] = out
        return 0

    s_ref[...] = jnp.full(s_ref.shape, NEG, F32)
    lax.fori_loop(0, bd * n_h, per_pair, 0)


def moba_attend(page_table, sel, qt4, knt4, vnt4, cache_kt, cache_vt):
    bd, n_pages = page_table.shape
    _, n_h, hd, t_len = qt4.shape
    page = cache_kt.shape[3]
    ppb = MOBA_BLOCK // page
    n_slab = t_len * MOBA_TOPK * ppb
    full = pl.BlockSpec(qt4.shape, lambda i, pt, sl: (0, 0, 0, 0))
    grid_spec = pltpu.PrefetchScalarGridSpec(
        num_scalar_prefetch=2,
        grid=(1,),
        in_specs=[full, full, full, pl.BlockSpec(memory_space=pl.ANY),
                  pl.BlockSpec(memory_space=pl.ANY)],
        out_specs=full,
        scratch_shapes=[pltpu.VMEM((ATTEND_SLOTS, n_slab, hd, page), F32),
                        pltpu.VMEM((ATTEND_SLOTS, n_slab, hd, page), F32),
                        pltpu.SemaphoreType.DMA((ATTEND_SLOTS,)),
                        pltpu.VMEM((t_len * 8, page), F32)],
    )
    return pl.pallas_call(
        functools.partial(_moba_attend_kernel, pages_per_block=ppb, n_pages=n_pages),
        grid_spec=grid_spec,
        out_shape=jax.ShapeDtypeStruct(qt4.shape, F32),
        compiler_params=_params("arbitrary"),
        name="moba_attend",
    )(page_table.reshape(-1), sel.reshape(-1), qt4, knt4, vnt4, cache_kt, cache_vt)


def _rows8(x):
    return x.reshape(x.shape[0] // 8, 8, x.shape[1])


def _softplus(x):
    return jnp.maximum(x, 0.0) + jnp.log1p(jnp.exp(-jnp.abs(x)))


def _ssd_kernel(*refs, d_inner, rows_valid, has_halo):
    if has_halo:
        (zx_ref, halo_ref, dt_ref, s0_ref, cw_ref, cb_ref, dtb_ref, alog_ref, dskip_ref, e_ref,
         y_ref, cs_ref, fs_ref, st_ref, xin_ref, dtin_ref, y_scr) = refs
        hist_ref = None
    else:
        (zx_ref, hist_ref, dt_ref, s0_ref, cw_ref, cb_ref, dtb_ref, alog_ref, dskip_ref, e_ref,
         y_ref, cs_ref, fs_ref, st_ref, xin_ref, dtin_ref, y_scr) = refs
        halo_ref = None
    c = pl.program_id(1)
    n_chunks = pl.num_programs(1)
    cl = SSM_CHUNK
    rows_blk = zx_ref.shape[1]
    conv_dim = cw_ref.shape[2]
    n_st = SSM_STATE
    hp = SSM_HEAD_DIM
    heads_per_group = d_inner // hp // SSM_GROUPS
    gw = heads_per_group * hp

    @pl.when(c == 0)
    def _():
        st_ref[...] = s0_ref[0].T

    if has_halo:
        xin_ref[0:8, :] = halo_ref[0, :, d_inner:d_inner + conv_dim] * (c != 0).astype(F32)
    else:
        xin_ref[0:8, :] = jnp.zeros((8, conv_dim), F32)
        xin_ref[5:8, :] = hist_ref[0]
    if rows_blk == cl:
        xin_ref[8:8 + cl, :] = zx_ref[0, :, d_inner:d_inner + conv_dim]
        dt_raw = dt_ref[0]
    else:
        xin_ref[8:8 + cl, :] = jnp.zeros((cl, conv_dim), F32)
        dtin_ref[...] = jnp.zeros(dtin_ref.shape, F32)
        xin_ref[8:8 + rows_blk, :] = zx_ref[0, :, d_inner:d_inner + conv_dim]
        dtin_ref[0:rows_blk, :] = dt_ref[0]
        dt_raw = dtin_ref[...]

    xbc = xin_ref[8:8 + cl, :]
    taps = [_rows8(xin_ref[5 + j:5 + j + cl, :]) * cw_ref[j][None] for j in range(3)]
    conv = taps[0] + taps[1] + taps[2] + _rows8(xbc) * cw_ref[3][None] + cb_ref[...][None]
    conv = conv.reshape(cl, conv_dim)
    act = conv * jax.nn.sigmoid(conv)

    @pl.when(c == n_chunks - 1)
    def _():
        cs_ref[0] = xbc[rows_valid - 3:rows_valid, :]

    dt = _softplus(dt_raw + dtb_ref[...])
    if rows_valid < cl:
        dt = jnp.where(lax.broadcasted_iota(jnp.int32, dt.shape, 0) < rows_valid, dt, 0.0)
    a = -jnp.exp(alog_ref[...])
    dta = dt * a
    r_i = lax.broadcasted_iota(jnp.int32, (cl, cl), 0)
    c_i = lax.broadcasted_iota(jnp.int32, (cl, cl), 1)
    causal = c_i <= r_i
    first_head_lanes = lax.broadcasted_iota(jnp.int32, (cl, 2 * hp), 1) < hp
    tril = jnp.where(causal, 1.0, 0.0)
    acum = jnp.dot(tril, dta, preferred_element_type=F32, precision=HIGHEST)
    acum_t = acum.T
    alast = acum[cl - 1:cl, :]
    narrow = jnp.concatenate([dt, jnp.exp(acum), jnp.exp(alast - acum),
                              jnp.broadcast_to(jnp.exp(alast), (8, alast.shape[1]))], axis=0)
    hi = narrow.astype(BF16)
    mid = (narrow - hi.astype(F32)).astype(BF16)
    wide = _dot(jnp.concatenate([hi, mid], axis=1), e_ref[...])
    dt_w = wide[0:cl]
    eac_w = wide[cl:2 * cl]
    dte_w = wide[2 * cl:3 * cl]
    cd_w = wide[3 * cl:3 * cl + 8]

    xs = act[:, 0:d_inner]
    xdt = xs * dt_w
    xdt_b = xdt.astype(BF16)
    xdtd_b = (xdt * dte_w).astype(BF16)
    for g in range(SSM_GROUPS):
        b_g = act[:, d_inner + g * n_st:d_inner + (g + 1) * n_st]
        c_g = act[:, d_inner + (SSM_GROUPS + g) * n_st:d_inner + (SSM_GROUPS + g + 1) * n_st]
        c_gb = c_g.astype(BF16)
        cb = _dot_nt(c_gb, b_g.astype(BF16))
        s_g = st_ref[:, g * gw:(g + 1) * gw]
        y_off = _dot(c_gb, s_g.astype(BF16)) * eac_w[:, g * gw:(g + 1) * gw]
        for r in range(0, heads_per_group, 2):
            h = g * heads_per_group + r
            stacked = []
            for hh in (h, h + 1):
                seg = acum[:, hh:hh + 1] - acum_t[hh:hh + 1, :]
                decay = jnp.exp(jnp.where(causal, seg, NEG))
                stacked.append((cb * decay).astype(BF16))
            both = _dot(jnp.concatenate(stacked, axis=0), xdt_b[:, h * hp:(h + 2) * hp])
            y_d = jnp.where(first_head_lanes, both[0:cl], both[cl:2 * cl])
            y_scr[:, h * hp:(h + 2) * hp] = y_d + y_off[:, r * hp:(r + 2) * hp]
        kept = (_rows8(s_g) * cd_w[:, g * gw:(g + 1) * gw][None]).reshape(n_st, gw)
        st_ref[:, g * gw:(g + 1) * gw] = kept + _dot(b_g.T.astype(BF16),
                                                     xdtd_b[:, g * gw:(g + 1) * gw])
    y = y_scr[...] + (_rows8(xs) * dskip_ref[...][None]).reshape(cl, d_inner)
    y_ref[0] = y[0:rows_blk, :]

    @pl.when(c == n_chunks - 1)
    def _():
        fs_ref[0] = st_ref[...].T


def ssd_mixer(zx3, dt3, hist, state0, conv_w, conv_b, dt_bias, a_log, d_skip, d_inner):
    nseq, t_len, _ = zx3.shape
    conv_dim = conv_w.shape[1]
    n_heads = d_inner // SSM_HEAD_DIM
    cl = SSM_CHUNK
    has_halo = hist is None
    if has_halo:
        rows_blk, rows_valid, n_chunks = cl, cl, t_len // cl
    else:
        rows_blk, rows_valid, n_chunks = t_len, t_len, 1
    pad = lambda v: jnp.pad(v.astype(F32), (0, 128 - n_heads)).reshape(1, 128)
    expand = (jnp.arange(256)[:, None] % 128
              == (jnp.arange(d_inner)[None, :] // SSM_HEAD_DIM)).astype(BF16)
    rows8 = lambda v: jnp.broadcast_to(v[..., None, :], v.shape[:-1] + (8, v.shape[-1]))
    d_wide = rows8(jnp.repeat(d_skip.astype(F32), SSM_HEAD_DIM))
    const2 = lambda s, c: (0, 0)
    if has_halo:
        second = pl.BlockSpec((1, 8, zx3.shape[2]),
                              lambda s, c: (s, jnp.maximum(c * (cl // 8) - 1, 0), 0))
        second_arg = zx3
    else:
        second = pl.BlockSpec((1,) + hist.shape[1:], lambda s, c: (s, 0, 0))
        second_arg = hist
    outs = pl.pallas_call(
        functools.partial(_ssd_kernel, d_inner=d_inner, rows_valid=rows_valid, has_halo=has_halo),
        grid=(nseq, n_chunks),
        in_specs=[pl.BlockSpec((1, rows_blk, zx3.shape[2]), lambda s, c: (s, c, 0)),
                  second,
                  pl.BlockSpec((1, rows_blk, 128), lambda s, c: (s, c, 0)),
                  pl.BlockSpec((1, d_inner, SSM_STATE), lambda s, c: (s, 0, 0)),
                  pl.BlockSpec((conv_w.shape[0], 8, conv_dim), lambda s, c: (0, 0, 0)),
                  pl.BlockSpec((8, conv_dim), const2),
                  pl.BlockSpec((1, 128), const2),
                  pl.BlockSpec((1, 128), const2),
                  pl.BlockSpec((8, d_inner), const2),
                  pl.BlockSpec((256, d_inner), const2)],
        out_specs=[pl.BlockSpec((1, rows_blk, d_inner), lambda s, c: (s, c, 0)),
                   pl.BlockSpec((1, 3, conv_dim), lambda s, c: (s, 0, 0)),
                   pl.BlockSpec((1, d_inner, SSM_STATE), lambda s, c: (s, 0, 0))],
        out_shape=[jax.ShapeDtypeStruct((nseq, t_len, d_inner), F32),
                   jax.ShapeDtypeStruct((nseq, 3, conv_dim), F32),
                   jax.ShapeDtypeStruct((nseq, d_inner, SSM_STATE), F32)],
        scratch_shapes=[pltpu.VMEM((SSM_STATE, d_inner), F32),
                        pltpu.VMEM((cl + 8, conv_dim), F32),
                        pltpu.VMEM((cl, 128), F32),
                        pltpu.VMEM((cl, d_inner), F32)],
        compiler_params=_params("parallel", "arbitrary"),
        name="ssd_mixer",
    )(zx3, second_arg, dt3, state0, rows8(conv_w), rows8(conv_b), pad(dt_bias), pad(a_log),
      d_wide, expand)
    return outs


L1_POST_ROW_TILE = 512


def _l1_post_kernel(y_ref, z_ref, x_ref, gw_ref, wo_ref, pw_ref, pre_ref, wg_ref, wu_ref, wd_ref,
                    post_ref, o_ref):
    z = z_ref[...]
    g = y_ref[...] * (z * jax.nn.sigmoid(z))
    o = _dot(_rms(g, gw_ref[...]).astype(BF16), wo_ref[...])
    x_mid = x_ref[...] + _rms(o, pw_ref[...])
    o_ref[...] = _ffn_apply(x_mid, pre_ref, wg_ref, wu_ref, wd_ref, post_ref)


def l1_post(y, zx, x, gate_w, wo, post_w, ffn_w):
    m, d = x.shape
    d_inner = y.shape[1]
    tm = _row_tile(m, L1_POST_ROW_TILE)
    const = lambda i: (0, 0)
    ffn_args, ffn_specs = _ffn_operands(ffn_w, d)
    return pl.pallas_call(
        _l1_post_kernel,
        grid=(m // tm,),
        in_specs=[pl.BlockSpec((tm, d_inner), lambda i: (i, 0)),
                  pl.BlockSpec((tm, d_inner), lambda i: (i, 0)),
                  pl.BlockSpec((tm, d), lambda i: (i, 0)),
                  pl.BlockSpec((1, d_inner), const),
                  pl.BlockSpec(wo.shape, const, pipeline_mode=pl.Buffered(1)),
                  pl.BlockSpec((1, d), const)] + ffn_specs,
        out_specs=pl.BlockSpec((tm, d), lambda i: (i, 0)),
        out_shape=jax.ShapeDtypeStruct((m, d), F32),
        compiler_params=_params("parallel"),
        name="l1_post",
    )(y, zx, x, gate_w.reshape(1, d_inner), wo, post_w.reshape(1, d), *ffn_args)


def kernel(x_prompt, x_sample, cache_k, cache_v, page_table, state_conv_a, state_conv_ssm, state_ssm, l0_norm_mix_pre, l0_w_in, l0_conv_w, l0_w_out, l0_norm_mix_post, l0_norm_ffn_pre, l0_ffn_gate, l0_ffn_up, l0_ffn_down, l0_norm_ffn_post, l1_norm_mix_pre, l1_w_in, l1_conv_w, l1_conv_b, l1_dt_bias, l1_a_log, l1_d_skip, l1_norm_gate, l1_w_out, l1_norm_mix_post, l1_norm_ffn_pre, l1_ffn_gate, l1_ffn_up, l1_ffn_down, l1_norm_ffn_post):
    bp, s_len, d = x_prompt.shape
    bd, t_len, _ = x_sample.shape
    n_heads, page, hd = cache_k.shape[1:]
    width = n_heads * hd
    assert (page_table.shape[1] * page) % MOBA_BLOCK == 0
    conv_dim = l1_conv_w.shape[1]
    ssm_heads = l1_dt_bias.shape[0]
    d_inner = ssm_heads * SSM_HEAD_DIM

    bf = lambda w: w.astype(BF16)
    xp = x_prompt.reshape(bp * s_len, d)
    xs = x_sample.reshape(bd * t_len, d)

    w_in0 = bf(l0_w_in)
    w_out0 = bf(l0_w_out)
    hp_ = norm_matmul(xp, l0_norm_mix_pre, w_in0)
    hs_ = norm_matmul(xs, l0_norm_mix_pre, w_in0)

    yb_p, kp_t, vp_t = moba_prompt(hp_.reshape(bp, s_len, -1), n_heads, hd, page)
    k_prompt = jnp.swapaxes(kp_t, 3, 4)
    v_prompt = jnp.swapaxes(vp_t, 3, 4)
    cache_kt = jnp.swapaxes(cache_k, 2, 3)
    cache_vt = jnp.swapaxes(cache_v, 2, 3)
    ffn0 = (l0_norm_ffn_pre, bf(l0_ffn_gate), bf(l0_ffn_up), bf(l0_ffn_down), l0_norm_ffn_post)
    xp, conv_a_prompt, block_means = l0_post_prompt(
        hp_, yb_p.reshape(bp * s_len, width), xp, l0_conv_w, w_out0, l0_norm_mix_post, ffn0, s_len,
        page_table, cache_kt)

    qkv_t = hs_[:, 3 * width:].reshape(bd, t_len, 3, n_heads, hd).transpose(2, 0, 3, 4, 1)
    qt4, knt4, vnt4 = qkv_t[0], qkv_t[1], qkv_t[2]
    n_blk = page_table.shape[1] * page // MOBA_BLOCK
    sel = moba_select(block_means, qt4, n_blk)[:, :, :t_len * MOBA_TOPK]
    yb_s = moba_attend(page_table, sel, qt4, knt4, vnt4, cache_kt, cache_vt)
    yb_s = yb_s.transpose(0, 3, 1, 2).reshape(bd * t_len, width)
    hist_rows = jnp.pad(state_conv_a, ((0, 0), (0, t_len - state_conv_a.shape[1]), (0, 0)))
    xs, gcu_s = l0_post_sample(hs_, yb_s, xs, hist_rows.reshape(bd * t_len, width), l0_conv_w,
                               w_out0, l0_norm_mix_post, t_len)
    conv_a_sample = gcu_s.reshape(bd, t_len, width)[:, t_len - 2:]
    k_sample = knt4.transpose(0, 1, 3, 2)
    v_sample = vnt4.transpose(0, 1, 3, 2)

    xs = ffn(xs, ffn0)

    w_zx = bf(l1_w_in[:, :d_inner + conv_dim])
    w_dt = jnp.pad(l1_w_in[:, d_inner + conv_dim:], ((0, 0), (0, 128 - ssm_heads)))
    w_out1 = bf(l1_w_out)
    zx_p, dt_p = norm_matmul(xp, l1_norm_mix_pre, w_zx, w_dt)
    zx_s, dt_s = norm_matmul(xs, l1_norm_mix_pre, w_zx, w_dt)

    zeros_state = jnp.zeros((bp, d_inner, SSM_STATE), F32)
    y_p, conv_ssm_prompt, fs_p = ssd_mixer(zx_p.reshape(bp, s_len, -1), dt_p.reshape(bp, s_len, 128),
                                           None, zeros_state, l1_conv_w, l1_conv_b, l1_dt_bias,
                                           l1_a_log, l1_d_skip, d_inner)
    y_s, conv_ssm_sample, fs_s = ssd_mixer(zx_s.reshape(bd, t_len, -1), dt_s.reshape(bd, t_len, 128),
                                           state_conv_ssm, state_ssm.reshape(bd, d_inner, SSM_STATE),
                                           l1_conv_w, l1_conv_b, l1_dt_bias, l1_a_log, l1_d_skip,
                                           d_inner)
    ssm_prompt = fs_p.reshape(bp, ssm_heads, SSM_HEAD_DIM, SSM_STATE)
    ssm_sample = fs_s.reshape(bd, ssm_heads, SSM_HEAD_DIM, SSM_STATE)

    ffn1 = (l1_norm_ffn_pre, bf(l1_ffn_gate), bf(l1_ffn_up), bf(l1_ffn_down), l1_norm_ffn_post)
    xp = l1_post(y_p.reshape(bp * s_len, d_inner), zx_p, xp, l1_norm_gate, w_out1, l1_norm_mix_post,
                 ffn1)
    xs = l1_post(y_s.reshape(bd * t_len, d_inner), zx_s, xs, l1_norm_gate, w_out1, l1_norm_mix_post,
                 ffn1)

    return (xp.reshape(bp, s_len, d), xs.reshape(bd, t_len, d), k_prompt, v_prompt, k_sample,
            v_sample, conv_a_prompt, conv_a_sample, conv_ssm_prompt, conv_ssm_sample, ssm_prompt,
            ssm_sample)
```

```python
import functools

import jax
import jax.numpy as jnp
from jax import lax
from jax.experimental import pallas as pl
from jax.experimental.pallas import tpu as pltpu

F32 = jnp.float32
BF16 = jnp.bfloat16
HIGHEST = lax.Precision.HIGHEST

NORM_EPS = 1e-6
MOBA_BLOCK = 256
MOBA_TOPK = 3
SSM_CHUNK = 128
SSM_HEAD_DIM = 64
SSM_STATE = 128
SSM_GROUPS = 4
NEG = -1e30
LOG2E = 1.4426950408889634
ONES_ROWS = 16

VMEM_LIMIT_BYTES = 62 * 1024 * 1024
ROW_TILE = 512


def _params(*sem):
    return pltpu.CompilerParams(dimension_semantics=sem, vmem_limit_bytes=VMEM_LIMIT_BYTES)


def _rms(x, w):
    return x * lax.rsqrt(jnp.mean(x * x, axis=-1, keepdims=True) + NORM_EPS) * w


def _dot(a, b):
    return jnp.dot(a, b, preferred_element_type=F32)


def _dot_nt(a, b, precision=None):
    return lax.dot_general(a, b, (((1,), (1,)), ((), ())), preferred_element_type=F32,
                           precision=precision)


def _row_tile(m, tile=ROW_TILE):
    return tile if m % tile == 0 else m


PROJ_ROW_TILE = 512
PROJ_COL_CHUNK = 1024


def _norm_matmul_kernel(*refs, narrow):
    if narrow:
        x_ref, nw_ref, w_ref, whi_ref, wlo_ref, o_ref, o2_ref = refs
    else:
        x_ref, nw_ref, w_ref, o_ref = refs
    xn = _rms(x_ref[...], nw_ref[...])
    xh = xn.astype(BF16)
    n = w_ref.shape[1]
    for c0 in range(0, n, PROJ_COL_CHUNK):
        c1 = min(c0 + PROJ_COL_CHUNK, n)
        o_ref[:, c0:c1] = _dot(xh, w_ref[:, c0:c1])
    if narrow:
        xl = (xn - xh.astype(F32)).astype(BF16)
        o2_ref[...] = _dot(xh, whi_ref[...]) + _dot(xl, whi_ref[...]) + _dot(xh, wlo_ref[...])


def norm_matmul(x, nw, w_bf16, w_narrow=None):
    m, d = x.shape
    n = w_bf16.shape[1]
    tm = PROJ_ROW_TILE if m % PROJ_ROW_TILE == 0 else m
    const = lambda i: (0, 0)
    resident = lambda shape: pl.BlockSpec(shape, const, pipeline_mode=pl.Buffered(1))
    in_specs = [pl.BlockSpec((tm, d), lambda i: (i, 0)), pl.BlockSpec((1, d), const),
                resident((d, n))]
    out_specs = [pl.BlockSpec((tm, n), lambda i: (i, 0))]
    out_shape = [jax.ShapeDtypeStruct((m, n), F32)]
    args = [x, nw.reshape(1, d), w_bf16]
    if w_narrow is not None:
        n2 = w_narrow.shape[1]
        whi = w_narrow.astype(BF16)
        wlo = (w_narrow - whi.astype(F32)).astype(BF16)
        in_specs += [resident((d, n2)), resident((d, n2))]
        out_specs.append(pl.BlockSpec((tm, n2), lambda i: (i, 0)))
        out_shape.append(jax.ShapeDtypeStruct((m, n2), F32))
        args += [whi, wlo]
    outs = pl.pallas_call(
        functools.partial(_norm_matmul_kernel, narrow=w_narrow is not None),
        grid=(m // tm,),
        in_specs=in_specs,
        out_specs=out_specs,
        out_shape=out_shape,
        compiler_params=_params("parallel"),
        name="norm_matmul",
    )(*args)
    return outs if w_narrow is not None else outs[0]


FFN_CHUNK = 256
FFN_ROW_TILE = 1024


def _ffn_apply(x, pre_ref, wg_ref, wu_ref, wd_ref, post_ref, side_task=None):
    h = _rms(x, pre_ref[...]).astype(BF16)
    n_chunks = wg_ref.shape[1] // FFN_CHUNK
    acc = jnp.zeros(x.shape, F32)
    for c in range(n_chunks):
        sl = slice(c * FFN_CHUNK, (c + 1) * FFN_CHUNK)
        g = _dot(h, wg_ref[:, sl])
        u = _dot(h, wu_ref[:, sl])
        a = (g * jax.nn.sigmoid(g) * u).astype(BF16)
        acc = acc + _dot(a, wd_ref[sl, :])
        if side_task is not None:
            side_task(c, n_chunks)
    return x + _rms(acc, post_ref[...])


def _ffn_kernel(x_ref, pre_ref, wg_ref, wu_ref, wd_ref, post_ref, o_ref):
    o_ref[...] = _ffn_apply(x_ref[...], pre_ref, wg_ref, wu_ref, wd_ref, post_ref)


def _ffn_operands(ffn_w, d, const=lambda i: (0, 0)):
    pre_w, wg, wu, wd, post_w = ffn_w
    resident = lambda w: pl.BlockSpec(w.shape, const, pipeline_mode=pl.Buffered(1))
    args = [pre_w.reshape(1, d), wg, wu, wd, post_w.reshape(1, d)]
    specs = [pl.BlockSpec((1, d), const), resident(wg), resident(wu), resident(wd),
             pl.BlockSpec((1, d), const)]
    return args, specs


def ffn(x, ffn_w):
    m, d = x.shape
    tm = _row_tile(m, FFN_ROW_TILE)
    ffn_args, ffn_specs = _ffn_operands(ffn_w, d)
    return pl.pallas_call(
        _ffn_kernel,
        grid=(m // tm,),
        in_specs=[pl.BlockSpec((tm, d), lambda i: (i, 0))] + ffn_specs,
        out_specs=pl.BlockSpec((tm, d), lambda i: (i, 0)),
        out_shape=jax.ShapeDtypeStruct((m, d), F32),
        compiler_params=_params("parallel"),
        name="ffn",
    )(x, *ffn_args)


def _moba_work_items(nb):
    items = [(qi, i, int(i == qi // 2)) for qi in range(nb) for i in range(qi // 2 + 1)]
    return [list(col) for col in zip(*items)]


def _moba_prompt_kernel(tab_ref, q_ref, k_ref, v_ref, o_ref, kp_ref, vp_ref, vt_ref, kb_ref,
                        qb_ref, sel_ref):
    s_len = k_ref.shape[1]
    nb = s_len // MOBA_BLOCK
    n_items = tab_ref.shape[0] // 3
    npages = kp_ref.shape[1]
    hd = kp_ref.shape[3]
    page = kp_ref.shape[4]
    ppb = MOBA_BLOCK // page
    scale = hd ** -0.5

    kb_ref[...] = k_ref[0].astype(BF16)
    qb_ref[...] = (q_ref[0] * (scale * LOG2E)).astype(BF16)
    km = jnp.mean(k_ref[0].reshape(nb, MOBA_BLOCK, 2 * hd), axis=1)
    blk = lax.broadcasted_iota(jnp.int32, (nb, s_len), 0)
    own = lax.broadcasted_iota(jnp.int32, (nb, s_len), 1) // MOBA_BLOCK
    for hh in range(2):
        lo = hh * hd
        gate = _dot_nt(km[:, lo:lo + hd], q_ref[0, :, lo:lo + hd], precision=HIGHEST)
        cnt = jnp.zeros((nb, s_len), F32)
        for jp in range(nb - 1):
            gj = gate[jp:jp + 1, :]
            beats = ((gj > gate) | ((gj == gate) & (jp < blk))) & (jp < own)
            cnt = cnt + jnp.where(beats, 1.0, 0.0)
        picked = ((cnt < MOBA_TOPK) & (blk < own)) | (blk == own)
        limit_shift = jnp.where(picked, 0, -4 * MOBA_BLOCK)
        for qb in range(nb):
            sel_ref[hh, qb] = limit_shift[:, qb * MOBA_BLOCK:(qb + 1) * MOBA_BLOCK]
    for pg in range(npages):
        rows = slice(pg * page, (pg + 1) * page)
        kt = k_ref[0, rows, :].T
        vt = v_ref[0, rows, :].T
        kp_ref[0, pg] = kt.reshape(2, hd, page)
        vp_ref[0, pg] = vt.reshape(2, hd, page)
        cols = slice((pg % ppb) * page, (pg % ppb + 1) * page)
        for hh in range(2):
            vt_ref[pg // ppb, hh, 0:hd, cols] = vt[hh * hd:(hh + 1) * hd].astype(BF16)
    ones_row = jnp.where(lax.broadcasted_iota(jnp.int32, (ONES_ROWS, MOBA_BLOCK), 0) == 0, 1.0, 0.0)
    for j in range(nb):
        for hh in range(2):
            vt_ref[j, hh, hd:hd + ONES_ROWS, :] = ones_row.astype(BF16)

    row = lax.broadcasted_iota(jnp.int32, (MOBA_BLOCK, MOBA_BLOCK), 0)
    col_row = lax.broadcasted_iota(jnp.int32, (1, MOBA_BLOCK), 1)

    def scores(qi, j, hh):
        jc = jnp.minimum(j, qi)
        q_tile = qb_ref[pl.ds(pl.multiple_of(qi * MOBA_BLOCK, MOBA_BLOCK), MOBA_BLOCK),
                        hh * hd:(hh + 1) * hd]
        k_tile = kb_ref[pl.ds(pl.multiple_of(jc * MOBA_BLOCK, MOBA_BLOCK), MOBA_BLOCK),
                        hh * hd:(hh + 1) * hd]
        shift = jnp.where(j < qi, MOBA_BLOCK, jnp.where(j == qi, 0, -4 * MOBA_BLOCK))
        limit = col_row + (1 + shift) + sel_ref[hh, qi, pl.ds(jc, 1), :]
        return jnp.where(row < limit, _dot_nt(k_tile, q_tile), -jnp.inf)

    def weighted_values(qi, i, state):
        ja = jnp.minimum(2 * i, qi)
        jb = jnp.minimum(2 * i + 1, qi)
        return [alpha * acc + _dot(vt_ref[ja, hh], pa) + _dot(vt_ref[jb, hh], pb)
                for hh, (_, acc, alpha, pa, pb) in enumerate(state)]

    def write_block(qi, accs):
        o_ref[0, pl.ds(pl.multiple_of(qi * MOBA_BLOCK, MOBA_BLOCK), MOBA_BLOCK), :] = (
            jnp.concatenate([acc[0:hd] / acc[hd:hd + 1] for acc in accs], axis=0).T)

    def body(t, state):
        qi, i = tab_ref[t], tab_ref[n_items + t]
        tiles = [scores(qi, 2 * i + b, hh) for hh in range(2) for b in range(2)]
        prev = jnp.maximum(t - 1, 0)
        qi_prev = tab_ref[prev]
        accs = weighted_values(qi_prev, tab_ref[n_items + prev], state)
        fresh = (i == 0).astype(F32)
        new = []
        for hh in range(2):
            m = state[hh][0]
            m = m + fresh * (NEG - m)
            sa, sb = tiles[2 * hh], tiles[2 * hh + 1]
            m_new = jnp.maximum(m, jnp.max(jnp.maximum(sa, sb), axis=0, keepdims=True))
            alpha = jnp.exp2(m - m_new) * (1.0 - fresh)
            new.append((m_new, accs[hh], alpha, jnp.exp2(sa - m_new).astype(BF16),
                        jnp.exp2(sb - m_new).astype(BF16)))

        @pl.when((tab_ref[2 * n_items + prev] == 1) & (t > 0))
        def _():
            write_block(qi_prev, accs)

        return tuple(new)

    no_p = jnp.zeros((MOBA_BLOCK, MOBA_BLOCK), BF16)
    init = (jnp.full((1, MOBA_BLOCK), NEG, F32), jnp.zeros((hd + ONES_ROWS, MOBA_BLOCK), F32),
            jnp.zeros((1, MOBA_BLOCK), F32), no_p, no_p)
    state = lax.fori_loop(0, n_items, body, (init, init))
    write_block(nb - 1, weighted_values(nb - 1, (nb - 1) // 2, state))


def moba_prompt(h3, n_heads, hd, page):
    bsz, s_len, _ = h3.shape
    width = n_heads * hd
    lanes = 2 * hd
    pairs = n_heads // 2
    q0 = 3 * width // lanes
    nb = s_len // MOBA_BLOCK
    pages_shape = jax.ShapeDtypeStruct((bsz, s_len // page, n_heads, hd, page), F32)
    page_spec = pl.BlockSpec((1, s_len // page, 2, hd, page), lambda b, p, tab: (b, 0, p, 0, 0))
    items = jnp.asarray(sum(_moba_work_items(nb), []), jnp.int32)
    grid_spec = pltpu.PrefetchScalarGridSpec(
        num_scalar_prefetch=1,
        grid=(bsz, pairs),
        in_specs=[pl.BlockSpec((1, s_len, lanes), lambda b, p, tab: (b, 0, q0 + p)),
                  pl.BlockSpec((1, s_len, lanes), lambda b, p, tab: (b, 0, q0 + pairs + p)),
                  pl.BlockSpec((1, s_len, lanes), lambda b, p, tab: (b, 0, q0 + 2 * pairs + p))],
        out_specs=[pl.BlockSpec((1, s_len, lanes), lambda b, p, tab: (b, 0, p)),
                   page_spec, page_spec],
        scratch_shapes=[pltpu.VMEM((nb, 2, hd + ONES_ROWS, MOBA_BLOCK), BF16),
                        pltpu.VMEM((s_len, lanes), BF16),
                        pltpu.VMEM((s_len, lanes), BF16),
                        pltpu.VMEM((2, nb, nb, MOBA_BLOCK), jnp.int32)],
    )
    return pl.pallas_call(
        _moba_prompt_kernel,
        grid_spec=grid_spec,
        out_shape=[jax.ShapeDtypeStruct((bsz, s_len, width), F32), pages_shape, pages_shape],
        compiler_params=_params("parallel", "parallel"),
        name="moba_prompt",
    )(items, h3, h3, h3)


def _conv3_mix(gb, gcu, prev2, prev1, li, cw):
    s1 = jnp.where(li == 0, prev1[0], pltpu.roll(gcu, 1, 0))
    s2 = pltpu.roll(gcu, 2, 0)
    s2 = jnp.where(li == 0, prev2[0], jnp.where(li == 1, prev2[1], s2))
    conv = cw[0:1] * s2 + cw[1:2] * s1 + cw[2:3] * gcu
    return gb * conv


def _out_proj_residual(ya, yb, x, wo_ref, pw):
    half = ya.shape[1]
    y = _dot(ya.astype(BF16), wo_ref[0:half, :]) + _dot(yb.astype(BF16), wo_ref[half:, :])
    return x + _rms(y, pw)


def _cache_block_means(first_page, n_pages, first_lane, pt_ref, ck_ref, km_ref, buf, sem):
    _, n_h, hd, _ = km_ref.shape
    page = ck_ref.shape[3]
    ppc = buf.shape[0] // 2
    pages_per_block = MOBA_BLOCK // page
    lane3 = lax.broadcasted_iota(jnp.int32, (n_h, hd, 128), 2) - first_lane

    def pages_of(c):
        return [p for p in range(c * ppc, (c + 1) * ppc) if p < n_pages]

    def copy(p):
        half = (p // ppc) % 2
        return pltpu.make_async_copy(ck_ref.at[pt_ref[first_page + p]],
                                     buf.at[half * ppc + p % ppc], sem.at[half])

    def start(c):
        for p in pages_of(c):
            copy(p).start()

    def run(c, n_chunks):
        if c + 1 < n_chunks:
            start(c + 1)
        pages = pages_of(c)
        for p in pages:
            copy(p).wait()
        half = (c % 2) * ppc
        blocks = len(pages) // pages_per_block
        if blocks == 0:
            return
        sums = []
        for u in range(blocks):
            acc = buf[half + u * pages_per_block]
            for jj in range(1, pages_per_block):
                acc = acc + buf[half + u * pages_per_block + jj]
            sums.append(acc)
        means = jnp.sum(jnp.concatenate(sums, axis=0), axis=-1, keepdims=True) * (1.0 / MOBA_BLOCK)
        km = km_ref[0]
        first = c * ppc // pages_per_block
        for u in range(blocks):
            km = jnp.where(lane3 == first + u, means[u * n_h:(u + 1) * n_h], km)
        km_ref[0] = km

    return start, run


def _begin_cache_stream(i, pt_ref, ck_ref, km_ref, buf, sem, *, page_base, step_pages,
                        steps_per_seq):
    blocks_per_step = step_pages // (MOBA_BLOCK // ck_ref.shape[3])

    @pl.when(i % steps_per_seq == 0)
    def _():
        km_ref[...] = jnp.zeros(km_ref.shape, F32)

    start, run = _cache_block_means(page_base + i * step_pages, step_pages,
                                    (i % steps_per_seq) * blocks_per_step, pt_ref, ck_ref, km_ref,
                                    buf, sem)
    start(0)
    return run


def _cache_stream_plan(page_table, cache_kt, n_hosts, host, grid_steps, n_chunks):
    bd, n_pages = page_table.shape
    _, n_h, hd, page = cache_kt.shape
    ppb = MOBA_BLOCK // page
    step_pages = bd * n_pages // (n_hosts * grid_steps)
    assert step_pages % ppb == 0 and n_pages % step_pages == 0 and n_pages // ppb <= 128
    steps_per_seq = n_pages // step_pages
    pages_per_chunk = -(-step_pages // (n_chunks * ppb)) * ppb
    seqs = grid_steps // steps_per_seq
    kernel_kwargs = dict(page_base=host * grid_steps * step_pages, step_pages=step_pages,
                         steps_per_seq=steps_per_seq)
    out_spec = pl.BlockSpec((1, n_h, hd, 128), lambda i, pt: (i // steps_per_seq, 0, 0, 0))
    out_shape = jax.ShapeDtypeStruct((seqs, n_h, hd, 128), F32)
    scratch = [pltpu.VMEM((2 * pages_per_chunk, n_h, hd, page), F32), pltpu.SemaphoreType.DMA((2,))]
    return kernel_kwargs, out_spec, out_shape, scratch


def _l0_post_prompt_kernel(pt_ref, gb_ref, gc_ref, u_ref, gch_ref, uh_ref, yb_ref, x_ref, cw_ref,
                           wo_ref, pw_ref, pre_ref, wg_ref, wu_ref, wd_ref, post_ref, ck_ref,
                           xo_ref, st_ref, km_ref, buf, sem, *, tiles_per_seq, stream):
    i = pl.program_id(0)
    tm = gb_ref.shape[0]
    run = _begin_cache_stream(i, pt_ref, ck_ref, km_ref, buf, sem, **stream)

    gcu = gc_ref[...] * u_ref[...]
    halo = gch_ref[...] * uh_ref[...]
    halo = halo * (i % tiles_per_seq != 0).astype(F32)
    li = lax.broadcasted_iota(jnp.int32, gcu.shape, 0)
    ya = _conv3_mix(gb_ref[...], gcu, (halo[6:7], halo[7:8]), (halo[7:8],), li, cw_ref[...])
    x_mid = _out_proj_residual(ya, yb_ref[...], x_ref[...], wo_ref, pw_ref[...])
    xo_ref[...] = _ffn_apply(x_mid, pre_ref, wg_ref, wu_ref, wd_ref, post_ref, side_task=run)

    @pl.when(i % tiles_per_seq == tiles_per_seq - 1)
    def _():
        st_ref[0] = gcu[tm - 2:tm, :]


CACHE_STREAM_HOSTS = 2


def l0_post_prompt(h2, yb, x, conv_w, wo, post_w, ffn_w, seq_len, page_table, cache_kt):
    m, d = x.shape
    width = yb.shape[1]
    tm = ROW_TILE
    tiles_per_seq = seq_len // tm
    stream, km_spec, km_shape, stream_scratch = _cache_stream_plan(
        page_table, cache_kt, CACHE_STREAM_HOSTS, 0, m // tm, ffn_w[1].shape[1] // FFN_CHUNK)
    halo_idx = lambda i: jnp.maximum(i * (tm // 8) - 1, 0)
    const = lambda i, pt: (0, 0)
    ffn_args, ffn_specs = _ffn_operands(ffn_w, d, const)
    grid_spec = pltpu.PrefetchScalarGridSpec(
        num_scalar_prefetch=1,
        grid=(m // tm,),
        in_specs=[pl.BlockSpec((tm, width), lambda i, pt: (i, 0)),
                  pl.BlockSpec((tm, width), lambda i, pt: (i, 1)),
                  pl.BlockSpec((tm, width), lambda i, pt: (i, 2)),
                  pl.BlockSpec((8, width), lambda i, pt: (halo_idx(i), 1)),
                  pl.BlockSpec((8, width), lambda i, pt: (halo_idx(i), 2)),
                  pl.BlockSpec((tm, width), lambda i, pt: (i, 0)),
                  pl.BlockSpec((tm, d), lambda i, pt: (i, 0)),
                  pl.BlockSpec(conv_w.shape, const),
                  pl.BlockSpec(wo.shape, const, pipeline_mode=pl.Buffered(1)),
                  pl.BlockSpec((1, d), const)] + ffn_specs + [pl.BlockSpec(memory_space=pl.ANY)],
        out_specs=[pl.BlockSpec((tm, d), lambda i, pt: (i, 0)),
                   pl.BlockSpec((1, 2, width), lambda i, pt: (i // tiles_per_seq, 0, 0)),
                   km_spec],
        scratch_shapes=stream_scratch,
    )
    return pl.pallas_call(
        functools.partial(_l0_post_prompt_kernel, tiles_per_seq=tiles_per_seq, stream=stream),
        grid_spec=grid_spec,
        out_shape=[jax.ShapeDtypeStruct((m, d), F32),
                   jax.ShapeDtypeStruct((m // seq_len, 2, width), F32),
                   km_shape],
        compiler_params=_params("arbitrary"),
        name="l0_post_prompt",
    )(page_table.reshape(-1), h2, h2, h2, h2, h2, yb, x, conv_w, wo, post_w.reshape(1, d), *ffn_args,
      cache_kt)


def _l0_post_sample_kernel(gb_ref, gc_ref, u_ref, hist_ref, yb_ref, x_ref, cw_ref, wo_ref, pw_ref,
                           xo_ref, gcu_ref, *, seq_len):
    gcu = gc_ref[...] * u_ref[...]
    rows = gcu.shape[0]
    t = lax.broadcasted_iota(jnp.int32, gcu.shape, 0) % seq_len
    hist = hist_ref[...]
    hist_next = pltpu.roll(hist, rows - 1, 0)
    ya = _conv3_mix(gb_ref[...], gcu, (hist, hist), (hist_next,), t, cw_ref[...])
    xo_ref[...] = _out_proj_residual(ya, yb_ref[...], x_ref[...], wo_ref, pw_ref[...])
    gcu_ref[...] = gcu


def l0_post_sample(h2, yb, x, hist_rows, conv_w, wo, post_w, seq_len):
    m, d = x.shape
    width = yb.shape[1]
    const = lambda i: (0, 0)
    return pl.pallas_call(
        functools.partial(_l0_post_sample_kernel, seq_len=seq_len),
        grid=(1,),
        in_specs=[pl.BlockSpec((m, width), lambda i: (0, 0)),
                  pl.BlockSpec((m, width), lambda i: (0, 1)),
                  pl.BlockSpec((m, width), lambda i: (0, 2)),
                  pl.BlockSpec((m, width), const),
                  pl.BlockSpec((m, width), const),
                  pl.BlockSpec((m, d), const),
                  pl.BlockSpec(conv_w.shape, const),
                  pl.BlockSpec(wo.shape, const),
                  pl.BlockSpec((1, d), const)],
        out_specs=[pl.BlockSpec((m, d), const), pl.BlockSpec((m, width), const)],
        out_shape=[jax.ShapeDtypeStruct((m, d), F32), jax.ShapeDtypeStruct((m, width), F32)],
        compiler_params=_params("arbitrary"),
        name="l0_post_sample",
    )(h2, h2, h2, hist_rows, yb, x, conv_w, wo, post_w.reshape(1, d))


ATTEND_SLOTS = 4


def _moba_select_kernel(km_ref, qt_ref, sel_ref, *, n_blk):
    _, n_h, hd, t_len = qt_ref.shape
    lane = lax.broadcasted_iota(jnp.int32, (n_h, 128), 1)
    sub = lax.broadcasted_iota(jnp.int32, (n_h, 128), 0)
    out = jnp.zeros((n_h, 128), jnp.int32)
    for t in range(t_len):
        gate = jnp.full((n_h, 128), -jnp.inf, F32)
        for h in range(n_h):
            g_h = jnp.sum(km_ref[0, h] * qt_ref[0, h][:, t:t + 1], axis=0, keepdims=True)
            gate = jnp.where(sub == h, g_h, gate)
        gate = jnp.where(lane < n_blk, gate, -jnp.inf)
        for r in range(MOBA_TOPK):
            best = jnp.max(gate, axis=-1, keepdims=True)
            idx = jnp.min(jnp.where(gate == best, lane, 128), axis=-1, keepdims=True)
            out = jnp.where(lane == t * MOBA_TOPK + r, idx, out)
            gate = jnp.where(lane == idx, -jnp.inf, gate)
    sel_ref[0] = out


def moba_select(block_means, qt4, n_blk):
    bd, n_h, hd, t_len = qt4.shape
    assert n_blk <= 128 and t_len * MOBA_TOPK <= 128
    return pl.pallas_call(
        functools.partial(_moba_select_kernel, n_blk=n_blk),
        grid=(bd,),
        in_specs=[pl.BlockSpec((1, n_h, hd, 128), lambda b: (b, 0, 0, 0)),
                  pl.BlockSpec((1, n_h, hd, t_len), lambda b: (b, 0, 0, 0))],
        out_specs=pl.BlockSpec((1, n_h, 128), lambda b: (b, 0, 0)),
        out_shape=jax.ShapeDtypeStruct((bd, n_h, 128), jnp.int32),
        compiler_params=_params("parallel"),
        name="moba_select",
    )(block_means, qt4)


def _moba_attend_kernel(pt_ref, sel_ref, qt_ref, knt_ref, vnt_ref, ck_ref, cv_ref, o_ref,
                        kbuf, vbuf, sem, s_ref, *, pages_per_block, n_pages):
    bd, n_h, hd, t_len = qt_ref.shape
    page = ck_ref.shape[3]
    per_t = MOBA_TOPK * pages_per_block
    scale = hd ** -0.5
    step = lax.broadcasted_iota(jnp.int32, (1, t_len), 1)
    step_col = lax.broadcasted_iota(jnp.int32, (hd, t_len), 1)

    def copies(pair, slot):
        b = pair // n_h
        h = pair % n_h
        out = []
        for t in range(t_len):
            for r in range(MOBA_TOPK):
                blk = sel_ref[(b * n_h + h) * (t_len * MOBA_TOPK) + t * MOBA_TOPK + r]
                for jj in range(pages_per_block):
                    phys = pt_ref[b * n_pages + blk * pages_per_block + jj]
                    s = t * per_t + r * pages_per_block + jj
                    out.append(pltpu.make_async_copy(ck_ref.at[phys, h], kbuf.at[slot, s],
                                                     sem.at[slot]))
                    out.append(pltpu.make_async_copy(cv_ref.at[phys, h], vbuf.at[slot, s],
                                                     sem.at[slot]))
        return out

    for ahead in range(ATTEND_SLOTS - 1):
        for c in copies(ahead, ahead):
            c.start()

    def per_pair(pair, _):
        slot = pair % ATTEND_SLOTS
        b = pair // n_h
        h = pair % n_h
        nxt = pair + ATTEND_SLOTS - 1

        @pl.when(nxt < bd * n_h)
        def _():
            for c in copies(nxt, nxt % ATTEND_SLOTS):
                c.start()

        for c in copies(pair, slot):
            c.wait()

        q_all = qt_ref[b, h] * scale
        k_new = knt_ref[b, h]
        v_new = vnt_ref[b, h]
        for t in range(t_len):
            qc = q_all[:, t:t + 1]
            for s in range(per_t):
                s_ref[t * 8 + s:t * 8 + s + 1, :] = jnp.sum(kbuf[slot, t * per_t + s] * qc, axis=0,
                                                             keepdims=True)
        out = jnp.zeros((hd, t_len), F32)
        for t in range(t_len):
            qc = q_all[:, t:t + 1]
            s_sel = s_ref[t * 8:(t + 1) * 8, :]
            s_own = jnp.where(step <= t, jnp.sum(k_new * qc, axis=0, keepdims=True), NEG)
            m = jnp.maximum(jnp.max(s_own, axis=-1, keepdims=True),
                            jnp.max(jnp.max(s_sel, axis=-1, keepdims=True), axis=0, keepdims=True))
            p_own = jnp.exp(s_own - m)
            p_sel = jnp.exp(s_sel - m)
            l = (jnp.sum(p_own, axis=-1, keepdims=True)
                 + jnp.sum(jnp.sum(p_sel, axis=-1, keepdims=True), axis=0, keepdims=True))
            acc = vbuf[slot, t * per_t] * p_sel[0:1, :]
            for s in range(1, per_t):
                acc = acc + vbuf[slot, t * per_t + s] * p_sel[s:s + 1, :]
            o_t = (jnp.sum(v_new * p_own, axis=-1, keepdims=True)
                   + jnp.sum(acc, axis=-1, keepdims=True)) / l
            out = jnp.where(step_col == t, o_t, out)
        o_ref[b, h] = out
        return 0

    s_ref[...] = jnp.full(s_ref.shape, NEG, F32)
    lax.fori_loop(0, bd * n_h, per_pair, 0)


def moba_attend(page_table, sel, qt4, knt4, vnt4, cache_kt, cache_vt):
    bd, n_pages = page_table.shape
    _, n_h, hd, t_len = qt4.shape
    page = cache_kt.shape[3]
    ppb = MOBA_BLOCK // page
    n_slab = t_len * MOBA_TOPK * ppb
    full = pl.BlockSpec(qt4.shape, lambda i, pt, sl: (0, 0, 0, 0))
    grid_spec = pltpu.PrefetchScalarGridSpec(
        num_scalar_prefetch=2,
        grid=(1,),
        in_specs=[full, full, full, pl.BlockSpec(memory_space=pl.ANY),
                  pl.BlockSpec(memory_space=pl.ANY)],
        out_specs=full,
        scratch_shapes=[pltpu.VMEM((ATTEND_SLOTS, n_slab, hd, page), F32),
                        pltpu.VMEM((ATTEND_SLOTS, n_slab, hd, page), F32),
                        pltpu.SemaphoreType.DMA((ATTEND_SLOTS,)),
                        pltpu.VMEM((t_len * 8, page), F32)],
    )
    return pl.pallas_call(
        functools.partial(_moba_attend_kernel, pages_per_block=ppb, n_pages=n_pages),
        grid_spec=grid_spec,
        out_shape=jax.ShapeDtypeStruct(qt4.shape, F32),
        compiler_params=_params("arbitrary"),
        name="moba_attend",
    )(page_table.reshape(-1), sel.reshape(-1), qt4, knt4, vnt4, cache_kt, cache_vt)


def _rows8(x):
    return x.reshape(x.shape[0] // 8, 8, x.shape[1])


def _softplus(x):
    return jnp.maximum(x, 0.0) + jnp.log1p(jnp.exp(-jnp.abs(x)))


def _ssd_kernel(*refs, d_inner, rows_valid, has_halo):
    if has_halo:
        (zx_ref, halo_ref, dt_ref, s0_ref, cw_ref, cb_ref, dtb_ref, alog_ref, dskip_ref, e_ref,
         y_ref, cs_ref, fs_ref, st_ref, xin_ref, dtin_ref, y_scr) = refs
        hist_ref = None
    else:
        (zx_ref, hist_ref, dt_ref, s0_ref, cw_ref, cb_ref, dtb_ref, alog_ref, dskip_ref, e_ref,
         y_ref, cs_ref, fs_ref, st_ref, xin_ref, dtin_ref, y_scr) = refs
        halo_ref = None
    c = pl.program_id(1)
    n_chunks = pl.num_programs(1)
    cl = SSM_CHUNK
    rows_blk = zx_ref.shape[1]
    conv_dim = cw_ref.shape[2]
    n_st = SSM_STATE
    hp = SSM_HEAD_DIM
    heads_per_group = d_inner // hp // SSM_GROUPS
    gw = heads_per_group * hp

    @pl.when(c == 0)
    def _():
        st_ref[...] = s0_ref[0].T

    if has_halo:
        xin_ref[0:8, :] = halo_ref[0, :, d_inner:d_inner + conv_dim] * (c != 0).astype(F32)
    else:
        xin_ref[0:8, :] = jnp.zeros((8, conv_dim), F32)
        xin_ref[5:8, :] = hist_ref[0]
    if rows_blk == cl:
        xin_ref[8:8 + cl, :] = zx_ref[0, :, d_inner:d_inner + conv_dim]
        dt_raw = dt_ref[0]
    else:
        xin_ref[8:8 + cl, :] = jnp.zeros((cl, conv_dim), F32)
        dtin_ref[...] = jnp.zeros(dtin_ref.shape, F32)
        xin_ref[8:8 + rows_blk, :] = zx_ref[0, :, d_inner:d_inner + conv_dim]
        dtin_ref[0:rows_blk, :] = dt_ref[0]
        dt_raw = dtin_ref[...]

    xbc = xin_ref[8:8 + cl, :]
    taps = [_rows8(xin_ref[5 + j:5 + j + cl, :]) * cw_ref[j][None] for j in range(3)]
    conv = taps[0] + taps[1] + taps[2] + _rows8(xbc) * cw_ref[3][None] + cb_ref[...][None]
    conv = conv.reshape(cl, conv_dim)
    act = conv * jax.nn.sigmoid(conv)

    @pl.when(c == n_chunks - 1)
    def _():
        cs_ref[0] = xbc[rows_valid - 3:rows_valid, :]

    dt = _softplus(dt_raw + dtb_ref[...])
    if rows_valid < cl:
        dt = jnp.where(lax.broadcasted_iota(jnp.int32, dt.shape, 0) < rows_valid, dt, 0.0)
    a = -jnp.exp(alog_ref[...])
    dta = dt * a
    r_i = lax.broadcasted_iota(jnp.int32, (cl, cl), 0)
    c_i = lax.broadcasted_iota(jnp.int32, (cl, cl), 1)
    causal = c_i <= r_i
    first_head_lanes = lax.broadcasted_iota(jnp.int32, (cl, 2 * hp), 1) < hp
    tril = jnp.where(causal, 1.0, 0.0)
    acum = jnp.dot(tril, dta, preferred_element_type=F32, precision=HIGHEST)
    acum_t = acum.T
    alast = acum[cl - 1:cl, :]
    narrow = jnp.concatenate([dt, jnp.exp(acum), jnp.exp(alast - acum),
                              jnp.broadcast_to(jnp.exp(alast), (8, alast.shape[1]))], axis=0)
    hi = narrow.astype(BF16)
    mid = (narrow - hi.astype(F32)).astype(BF16)
    wide = _dot(jnp.concatenate([hi, mid], axis=1), e_ref[...])
    dt_w = wide[0:cl]
    eac_w = wide[cl:2 * cl]
    dte_w = wide[2 * cl:3 * cl]
    cd_w = wide[3 * cl:3 * cl + 8]

    xs = act[:, 0:d_inner]
    xdt = xs * dt_w
    xdt_b = xdt.astype(BF16)
    xdtd_b = (xdt * dte_w).astype(BF16)
    for g in range(SSM_GROUPS):
        b_g = act[:, d_inner + g * n_st:d_inner + (g + 1) * n_st]
        c_g = act[:, d_inner + (SSM_GROUPS + g) * n_st:d_inner + (SSM_GROUPS + g + 1) * n_st]
        c_gb = c_g.astype(BF16)
        cb = _dot_nt(c_gb, b_g.astype(BF16))
        s_g = st_ref[:, g * gw:(g + 1) * gw]
        y_off = _dot(c_gb, s_g.astype(BF16)) * eac_w[:, g * gw:(g + 1) * gw]
        for r in range(0, heads_per_group, 2):
            h = g * heads_per_group + r
            stacked = []
            for hh in (h, h + 1):
                seg = acum[:, hh:hh + 1] - acum_t[hh:hh + 1, :]
                decay = jnp.exp(jnp.where(causal, seg, NEG))
                stacked.append((cb * decay).astype(BF16))
            both = _dot(jnp.concatenate(stacked, axis=0), xdt_b[:, h * hp:(h + 2) * hp])
            y_d = jnp.where(first_head_lanes, both[0:cl], both[cl:2 * cl])
            y_scr[:, h * hp:(h + 2) * hp] = y_d + y_off[:, r * hp:(r + 2) * hp]
        kept = (_rows8(s_g) * cd_w[:, g * gw:(g + 1) * gw][None]).reshape(n_st, gw)
        st_ref[:, g * gw:(g + 1) * gw] = kept + _dot(b_g.T.astype(BF16),
                                                     xdtd_b[:, g * gw:(g + 1) * gw])
    y = y_scr[...] + (_rows8(xs) * dskip_ref[...][None]).reshape(cl, d_inner)
    y_ref[0] = y[0:rows_blk, :]

    @pl.when(c == n_chunks - 1)
    def _():
        fs_ref[0] = st_ref[...].T


def ssd_mixer(zx3, dt3, hist, state0, conv_w, conv_b, dt_bias, a_log, d_skip, d_inner):
    nseq, t_len, _ = zx3.shape
    conv_dim = conv_w.shape[1]
    n_heads = d_inner // SSM_HEAD_DIM
    cl = SSM_CHUNK
    has_halo = hist is None
    if has_halo:
        rows_blk, rows_valid, n_chunks = cl, cl, t_len // cl
    else:
        rows_blk, rows_valid, n_chunks = t_len, t_len, 1
    pad = lambda v: jnp.pad(v.astype(F32), (0, 128 - n_heads)).reshape(1, 128)
    expand = (jnp.arange(256)[:, None] % 128
              == (jnp.arange(d_inner)[None, :] // SSM_HEAD_DIM)).astype(BF16)
    rows8 = lambda v: jnp.broadcast_to(v[..., None, :], v.shape[:-1] + (8, v.shape[-1]))
    d_wide = rows8(jnp.repeat(d_skip.astype(F32), SSM_HEAD_DIM))
    const2 = lambda s, c: (0, 0)
    if has_halo:
        second = pl.BlockSpec((1, 8, zx3.shape[2]),
                              lambda s, c: (s, jnp.maximum(c * (cl // 8) - 1, 0), 0))
        second_arg = zx3
    else:
        second = pl.BlockSpec((1,) + hist.shape[1:], lambda s, c: (s, 0, 0))
        second_arg = hist
    outs = pl.pallas_call(
        functools.partial(_ssd_kernel, d_inner=d_inner, rows_valid=rows_valid, has_halo=has_halo),
        grid=(nseq, n_chunks),
        in_specs=[pl.BlockSpec((1, rows_blk, zx3.shape[2]), lambda s, c: (s, c, 0)),
                  second,
                  pl.BlockSpec((1, rows_blk, 128), lambda s, c: (s, c, 0)),
                  pl.BlockSpec((1, d_inner, SSM_STATE), lambda s, c: (s, 0, 0)),
                  pl.BlockSpec((conv_w.shape[0], 8, conv_dim), lambda s, c: (0, 0, 0)),
                  pl.BlockSpec((8, conv_dim), const2),
                  pl.BlockSpec((1, 128), const2),
                  pl.BlockSpec((1, 128), const2),
                  pl.BlockSpec((8, d_inner), const2),
                  pl.BlockSpec((256, d_inner), const2)],
        out_specs=[pl.BlockSpec((1, rows_blk, d_inner), lambda s, c: (s, c, 0)),
                   pl.BlockSpec((1, 3, conv_dim), lambda s, c: (s, 0, 0)),
                   pl.BlockSpec((1, d_inner, SSM_STATE), lambda s, c: (s, 0, 0))],
        out_shape=[jax.ShapeDtypeStruct((nseq, t_len, d_inner), F32),
                   jax.ShapeDtypeStruct((nseq, 3, conv_dim), F32),
                   jax.ShapeDtypeStruct((nseq, d_inner, SSM_STATE), F32)],
        scratch_shapes=[pltpu.VMEM((SSM_STATE, d_inner), F32),
                        pltpu.VMEM((cl + 8, conv_dim), F32),
                        pltpu.VMEM((cl, 128), F32),
                        pltpu.VMEM((cl, d_inner), F32)],
        compiler_params=_params("parallel", "arbitrary"),
        name="ssd_mixer",
    )(zx3, second_arg, dt3, state0, rows8(conv_w), rows8(conv_b), pad(dt_bias), pad(a_log),
      d_wide, expand)
    return outs


L1_POST_ROW_TILE = 512


def _l1_post_kernel(*refs, stream):
    if stream is None:
        (y_ref, z_ref, x_ref, gw_ref, wo_ref, pw_ref, pre_ref, wg_ref, wu_ref, wd_ref, post_ref,
         o_ref) = refs
        side_task = None
    else:
        (pt_ref, y_ref, z_ref, x_ref, gw_ref, wo_ref, pw_ref, pre_ref, wg_ref, wu_ref, wd_ref,
         post_ref, ck_ref, o_ref, km_ref, buf, sem) = refs
        side_task = _begin_cache_stream(pl.program_id(0), pt_ref, ck_ref, km_ref, buf, sem, **stream)
    z = z_ref[...]
    g = y_ref[...] * (z * jax.nn.sigmoid(z))
    o = _dot(_rms(g, gw_ref[...]).astype(BF16), wo_ref[...])
    x_mid = x_ref[...] + _rms(o, pw_ref[...])
    o_ref[...] = _ffn_apply(x_mid, pre_ref, wg_ref, wu_ref, wd_ref, post_ref, side_task=side_task)


def l1_post(y, zx, x, gate_w, wo, post_w, ffn_w, page_table=None, cache_kt=None):
    m, d = x.shape
    d_inner = y.shape[1]
    tm = _row_tile(m, L1_POST_ROW_TILE)
    streaming = page_table is not None
    const = (lambda i, pt: (0, 0)) if streaming else (lambda i: (0, 0))
    rows = (lambda i, pt: (i, 0)) if streaming else (lambda i: (i, 0))
    ffn_args, ffn_specs = _ffn_operands(ffn_w, d, const)
    in_specs = [pl.BlockSpec((tm, d_inner), rows),
                pl.BlockSpec((tm, d_inner), rows),
                pl.BlockSpec((tm, d), rows),
                pl.BlockSpec((1, d_inner), const),
                pl.BlockSpec(wo.shape, const, pipeline_mode=pl.Buffered(1)),
                pl.BlockSpec((1, d), const)] + ffn_specs
    out_specs = [pl.BlockSpec((tm, d), rows)]
    out_shape = [jax.ShapeDtypeStruct((m, d), F32)]
    args = [y, zx, x, gate_w.reshape(1, d_inner), wo, post_w.reshape(1, d), *ffn_args]
    if not streaming:
        return pl.pallas_call(
            functools.partial(_l1_post_kernel, stream=None),
            grid=(m // tm,),
            in_specs=in_specs,
            out_specs=out_specs,
            out_shape=out_shape,
            compiler_params=_params("parallel"),
            name="l1_post",
        )(*args)[0]
    stream, km_spec, km_shape, stream_scratch = _cache_stream_plan(
        page_table, cache_kt, CACHE_STREAM_HOSTS, 1, m // tm, ffn_w[1].shape[1] // FFN_CHUNK)
    grid_spec = pltpu.PrefetchScalarGridSpec(
        num_scalar_prefetch=1,
        grid=(m // tm,),
        in_specs=in_specs + [pl.BlockSpec(memory_space=pl.ANY)],
        out_specs=out_specs + [km_spec],
        scratch_shapes=stream_scratch,
    )
    return pl.pallas_call(
        functools.partial(_l1_post_kernel, stream=stream),
        grid_spec=grid_spec,
        out_shape=out_shape + [km_shape],
        compiler_params=_params("arbitrary"),
        name="l1_post",
    )(page_table.reshape(-1), *args, cache_kt)


def kernel(x_prompt, x_sample, cache_k, cache_v, page_table, state_conv_a, state_conv_ssm, state_ssm, l0_norm_mix_pre, l0_w_in, l0_conv_w, l0_w_out, l0_norm_mix_post, l0_norm_ffn_pre, l0_ffn_gate, l0_ffn_up, l0_ffn_down, l0_norm_ffn_post, l1_norm_mix_pre, l1_w_in, l1_conv_w, l1_conv_b, l1_dt_bias, l1_a_log, l1_d_skip, l1_norm_gate, l1_w_out, l1_norm_mix_post, l1_norm_ffn_pre, l1_ffn_gate, l1_ffn_up, l1_ffn_down, l1_norm_ffn_post):
    bp, s_len, d = x_prompt.shape
    bd, t_len, _ = x_sample.shape
    n_heads, page, hd = cache_k.shape[1:]
    width = n_heads * hd
    assert (page_table.shape[1] * page) % MOBA_BLOCK == 0
    conv_dim = l1_conv_w.shape[1]
    ssm_heads = l1_dt_bias.shape[0]
    d_inner = ssm_heads * SSM_HEAD_DIM

    bf = lambda w: w.astype(BF16)
    xp = x_prompt.reshape(bp * s_len, d)
    xs = x_sample.reshape(bd * t_len, d)

    w_in0 = bf(l0_w_in)
    w_out0 = bf(l0_w_out)
    ffn0 = (l0_norm_ffn_pre, bf(l0_ffn_gate), bf(l0_ffn_up), bf(l0_ffn_down), l0_norm_ffn_post)
    w_zx = bf(l1_w_in[:, :d_inner + conv_dim])
    w_dt = jnp.pad(l1_w_in[:, d_inner + conv_dim:], ((0, 0), (0, 128 - ssm_heads)))
    w_out1 = bf(l1_w_out)
    ffn1 = (l1_norm_ffn_pre, bf(l1_ffn_gate), bf(l1_ffn_up), bf(l1_ffn_down), l1_norm_ffn_post)
    cache_kt = jnp.swapaxes(cache_k, 2, 3)
    cache_vt = jnp.swapaxes(cache_v, 2, 3)

    hp_ = norm_matmul(xp, l0_norm_mix_pre, w_in0)
    yb_p, kp_t, vp_t = moba_prompt(hp_.reshape(bp, s_len, -1), n_heads, hd, page)
    k_prompt = jnp.swapaxes(kp_t, 3, 4)
    v_prompt = jnp.swapaxes(vp_t, 3, 4)
    xp, conv_a_prompt, means_a = l0_post_prompt(
        hp_, yb_p.reshape(bp * s_len, width), xp, l0_conv_w, w_out0, l0_norm_mix_post, ffn0, s_len,
        page_table, cache_kt)
    zx_p, dt_p = norm_matmul(xp, l1_norm_mix_pre, w_zx, w_dt)
    zeros_state = jnp.zeros((bp, d_inner, SSM_STATE), F32)
    y_p, conv_ssm_prompt, fs_p = ssd_mixer(zx_p.reshape(bp, s_len, -1), dt_p.reshape(bp, s_len, 128),
                                           None, zeros_state, l1_conv_w, l1_conv_b, l1_dt_bias,
                                           l1_a_log, l1_d_skip, d_inner)
    ssm_prompt = fs_p.reshape(bp, ssm_heads, SSM_HEAD_DIM, SSM_STATE)
    xp, means_b = l1_post(y_p.reshape(bp * s_len, d_inner), zx_p, xp, l1_norm_gate, w_out1,
                          l1_norm_mix_post, ffn1, page_table, cache_kt)
    block_means = jnp.concatenate([means_a, means_b], axis=0)

    hs_ = norm_matmul(xs, l0_norm_mix_pre, w_in0)
    qkv_t = hs_[:, 3 * width:].reshape(bd, t_len, 3, n_heads, hd).transpose(2, 0, 3, 4, 1)
    qt4, knt4, vnt4 = qkv_t[0], qkv_t[1], qkv_t[2]
    n_blk = page_table.shape[1] * page // MOBA_BLOCK
    sel = moba_select(block_means, qt4, n_blk)[:, :, :t_len * MOBA_TOPK]
    yb_s = moba_attend(page_table, sel, qt4, knt4, vnt4, cache_kt, cache_vt)
    yb_s = yb_s.transpose(0, 3, 1, 2).reshape(bd * t_len, width)
    hist_rows = jnp.pad(state_conv_a, ((0, 0), (0, t_len - state_conv_a.shape[1]), (0, 0)))
    xs, gcu_s = l0_post_sample(hs_, yb_s, xs, hist_rows.reshape(bd * t_len, width), l0_conv_w,
                               w_out0, l0_norm_mix_post, t_len)
    conv_a_sample = gcu_s.reshape(bd, t_len, width)[:, t_len - 2:]
    k_sample = knt4.transpose(0, 1, 3, 2)
    v_sample = vnt4.transpose(0, 1, 3, 2)
    xs = ffn(xs, ffn0)
    zx_s, dt_s = norm_matmul(xs, l1_norm_mix_pre, w_zx, w_dt)
    y_s, conv_ssm_sample, fs_s = ssd_mixer(zx_s.reshape(bd, t_len, -1), dt_s.reshape(bd, t_len, 128),
                                           state_conv_ssm, state_ssm.reshape(bd, d_inner, SSM_STATE),
                                           l1_conv_w, l1_conv_b, l1_dt_bias, l1_a_log, l1_d_skip,
                                           d_inner)
    ssm_sample = fs_s.reshape(bd, ssm_heads, SSM_HEAD_DIM, SSM_STATE)
    xs = l1_post(y_s.reshape(bd * t_len, d_inner), zx_s, xs, l1_norm_gate, w_out1, l1_norm_mix_post,
                 ffn1)

    return (xp.reshape(bp, s_len, d), xs.reshape(bd, t_len, d), k_prompt, v_prompt, k_sample,
            v_sample, conv_a_prompt, conv_a_sample, conv_ssm_prompt, conv_ssm_sample, ssm_prompt,
            ssm_sample)
```

```python
import functools

import jax
import jax.numpy as jnp
from jax import lax
from jax.experimental import pallas as pl
from jax.experimental.pallas import tpu as pltpu

F32 = jnp.float32
BF16 = jnp.bfloat16
HIGHEST = lax.Precision.HIGHEST

NORM_EPS = 1e-6
MOBA_BLOCK = 256
MOBA_TOPK = 3
SSM_CHUNK = 128
SSM_HEAD_DIM = 64
SSM_STATE = 128
SSM_GROUPS = 4
NEG = -1e30
LOG2E = 1.4426950408889634
ONES_ROWS = 16

VMEM_LIMIT_BYTES = 56 * 1024 * 1024
ROW_TILE = 512


def _params(*sem):
    return pltpu.CompilerParams(dimension_semantics=sem, vmem_limit_bytes=VMEM_LIMIT_BYTES)


def _rms(x, w):
    return x * lax.rsqrt(jnp.mean(x * x, axis=-1, keepdims=True) + NORM_EPS) * w


def _dot(a, b):
    return jnp.dot(a, b, preferred_element_type=F32)


def _dot_nt(a, b, precision=None):
    return lax.dot_general(a, b, (((1,), (1,)), ((), ())), preferred_element_type=F32,
                           precision=precision)


def _row_tile(m, tile=ROW_TILE):
    return tile if m % tile == 0 else m


PROJ_ROW_TILE = 512
PROJ_COL_CHUNK = 1024


def _norm_matmul_kernel(*refs, narrow):
    if narrow:
        x_ref, nw_ref, w_ref, whi_ref, wlo_ref, o_ref, o2_ref = refs
    else:
        x_ref, nw_ref, w_ref, o_ref = refs
    xn = _rms(x_ref[...], nw_ref[...])
    xh = xn.astype(BF16)
    n = w_ref.shape[1]
    for c0 in range(0, n, PROJ_COL_CHUNK):
        c1 = min(c0 + PROJ_COL_CHUNK, n)
        o_ref[:, c0:c1] = _dot(xh, w_ref[:, c0:c1])
    if narrow:
        xl = (xn - xh.astype(F32)).astype(BF16)
        o2_ref[...] = _dot(xh, whi_ref[...]) + _dot(xl, whi_ref[...]) + _dot(xh, wlo_ref[...])


def norm_matmul(x, nw, w_bf16, w_narrow=None):
    m, d = x.shape
    n = w_bf16.shape[1]
    tm = PROJ_ROW_TILE if m % PROJ_ROW_TILE == 0 else m
    const = lambda i: (0, 0)
    resident = lambda shape: pl.BlockSpec(shape, const, pipeline_mode=pl.Buffered(1))
    in_specs = [pl.BlockSpec((tm, d), lambda i: (i, 0)), pl.BlockSpec((1, d), const),
                resident((d, n))]
    out_specs = [pl.BlockSpec((tm, n), lambda i: (i, 0))]
    out_shape = [jax.ShapeDtypeStruct((m, n), F32)]
    args = [x, nw.reshape(1, d), w_bf16]
    if w_narrow is not None:
        n2 = w_narrow.shape[1]
        whi = w_narrow.astype(BF16)
        wlo = (w_narrow - whi.astype(F32)).astype(BF16)
        in_specs += [resident((d, n2)), resident((d, n2))]
        out_specs.append(pl.BlockSpec((tm, n2), lambda i: (i, 0)))
        out_shape.append(jax.ShapeDtypeStruct((m, n2), F32))
        args += [whi, wlo]
    outs = pl.pallas_call(
        functools.partial(_norm_matmul_kernel, narrow=w_narrow is not None),
        grid=(m // tm,),
        in_specs=in_specs,
        out_specs=out_specs,
        out_shape=out_shape,
        compiler_params=_params("parallel"),
        name="norm_matmul",
    )(*args)
    return outs if w_narrow is not None else outs[0]


FFN_CHUNK = 256
FFN_ROW_TILE = 1024


def _ffn_apply(x, pre_ref, wg_ref, wu_ref, wd_ref, post_ref, side_task=None):
    h = _rms(x, pre_ref[...]).astype(BF16)
    n_chunks = wg_ref.shape[1] // FFN_CHUNK
    acc = jnp.zeros(x.shape, F32)
    for c in range(n_chunks):
        sl = slice(c * FFN_CHUNK, (c + 1) * FFN_CHUNK)
        g = _dot(h, wg_ref[:, sl])
        u = _dot(h, wu_ref[:, sl])
        a = (g * jax.nn.sigmoid(g) * u).astype(BF16)
        acc = acc + _dot(a, wd_ref[sl, :])
        if side_task is not None:
            side_task(c, n_chunks)
    return x + _rms(acc, post_ref[...])


def _ffn_kernel(x_ref, pre_ref, wg_ref, wu_ref, wd_ref, post_ref, o_ref):
    o_ref[...] = _ffn_apply(x_ref[...], pre_ref, wg_ref, wu_ref, wd_ref, post_ref)


def _ffn_operands(ffn_w, d, const=lambda i: (0, 0)):
    pre_w, wg, wu, wd, post_w = ffn_w
    resident = lambda w: pl.BlockSpec(w.shape, const, pipeline_mode=pl.Buffered(1))
    args = [pre_w.reshape(1, d), wg, wu, wd, post_w.reshape(1, d)]
    specs = [pl.BlockSpec((1, d), const), resident(wg), resident(wu), resident(wd),
             pl.BlockSpec((1, d), const)]
    return args, specs


def ffn(x, ffn_w):
    m, d = x.shape
    tm = _row_tile(m, FFN_ROW_TILE)
    ffn_args, ffn_specs = _ffn_operands(ffn_w, d)
    return pl.pallas_call(
        _ffn_kernel,
        grid=(m // tm,),
        in_specs=[pl.BlockSpec((tm, d), lambda i: (i, 0))] + ffn_specs,
        out_specs=pl.BlockSpec((tm, d), lambda i: (i, 0)),
        out_shape=jax.ShapeDtypeStruct((m, d), F32),
        compiler_params=_params("parallel"),
        name="ffn",
    )(x, *ffn_args)


def _moba_work_items(nb):
    items = [(qi, i, int(i == qi // 2)) for qi in range(nb) for i in range(qi // 2 + 1)]
    return [list(col) for col in zip(*items)]


def _moba_prompt_kernel(tab_ref, q_ref, k_ref, v_ref, o_ref, kp_ref, vp_ref, vt_ref, kb_ref,
                        qb_ref, sel_ref):
    s_len = k_ref.shape[1]
    nb = s_len // MOBA_BLOCK
    n_items = tab_ref.shape[0] // 3
    npages = kp_ref.shape[1]
    hd = kp_ref.shape[3]
    page = kp_ref.shape[4]
    ppb = MOBA_BLOCK // page
    scale = hd ** -0.5

    kb_ref[...] = k_ref[0].astype(BF16)
    qb_ref[...] = (q_ref[0] * (scale * LOG2E)).astype(BF16)
    km = jnp.mean(k_ref[0].reshape(nb, MOBA_BLOCK, 2 * hd), axis=1)
    blk = lax.broadcasted_iota(jnp.int32, (nb, s_len), 0)
    own = lax.broadcasted_iota(jnp.int32, (nb, s_len), 1) // MOBA_BLOCK
    for hh in range(2):
        lo = hh * hd
        gate = _dot_nt(km[:, lo:lo + hd], q_ref[0, :, lo:lo + hd], precision=HIGHEST)
        cnt = jnp.zeros((nb, s_len), F32)
        for jp in range(nb - 1):
            gj = gate[jp:jp + 1, :]
            beats = ((gj > gate) | ((gj == gate) & (jp < blk))) & (jp < own)
            cnt = cnt + jnp.where(beats, 1.0, 0.0)
        picked = ((cnt < MOBA_TOPK) & (blk < own)) | (blk == own)
        limit_shift = jnp.where(picked, 0, -4 * MOBA_BLOCK)
        for qb in range(nb):
            sel_ref[hh, qb] = limit_shift[:, qb * MOBA_BLOCK:(qb + 1) * MOBA_BLOCK]
    for pg in range(npages):
        rows = slice(pg * page, (pg + 1) * page)
        kt = k_ref[0, rows, :].T
        vt = v_ref[0, rows, :].T
        kp_ref[0, pg] = kt.reshape(2, hd, page)
        vp_ref[0, pg] = vt.reshape(2, hd, page)
        cols = slice((pg % ppb) * page, (pg % ppb + 1) * page)
        for hh in range(2):
            vt_ref[pg // ppb, hh, 0:hd, cols] = vt[hh * hd:(hh + 1) * hd].astype(BF16)
    ones_row = jnp.where(lax.broadcasted_iota(jnp.int32, (ONES_ROWS, MOBA_BLOCK), 0) == 0, 1.0, 0.0)
    for j in range(nb):
        for hh in range(2):
            vt_ref[j, hh, hd:hd + ONES_ROWS, :] = ones_row.astype(BF16)

    row = lax.broadcasted_iota(jnp.int32, (MOBA_BLOCK, MOBA_BLOCK), 0)
    col_row = lax.broadcasted_iota(jnp.int32, (1, MOBA_BLOCK), 1)

    def scores(qi, j, hh):
        jc = jnp.minimum(j, qi)
        q_tile = qb_ref[pl.ds(pl.multiple_of(qi * MOBA_BLOCK, MOBA_BLOCK), MOBA_BLOCK),
                        hh * hd:(hh + 1) * hd]
        k_tile = kb_ref[pl.ds(pl.multiple_of(jc * MOBA_BLOCK, MOBA_BLOCK), MOBA_BLOCK),
                        hh * hd:(hh + 1) * hd]
        shift = jnp.where(j < qi, MOBA_BLOCK, jnp.where(j == qi, 0, -4 * MOBA_BLOCK))
        limit = col_row + (1 + shift) + sel_ref[hh, qi, pl.ds(jc, 1), :]
        return jnp.where(row < limit, _dot_nt(k_tile, q_tile), -jnp.inf)

    def weighted_values(qi, i, state):
        ja = jnp.minimum(2 * i, qi)
        jb = jnp.minimum(2 * i + 1, qi)
        return [alpha * acc + _dot(vt_ref[ja, hh], pa) + _dot(vt_ref[jb, hh], pb)
                for hh, (_, acc, alpha, pa, pb) in enumerate(state)]

    def write_block(qi, accs):
        o_ref[0, pl.ds(pl.multiple_of(qi * MOBA_BLOCK, MOBA_BLOCK), MOBA_BLOCK), :] = (
            jnp.concatenate([acc[0:hd] / acc[hd:hd + 1] for acc in accs], axis=0).T)

    def body(t, state):
        qi, i = tab_ref[t], tab_ref[n_items + t]
        tiles = [scores(qi, 2 * i + b, hh) for hh in range(2) for b in range(2)]
        prev = jnp.maximum(t - 1, 0)
        qi_prev = tab_ref[prev]
        accs = weighted_values(qi_prev, tab_ref[n_items + prev], state)
        fresh = (i == 0).astype(F32)
        new = []
        for hh in range(2):
            m = state[hh][0]
            m = m + fresh * (NEG - m)
            sa, sb = tiles[2 * hh], tiles[2 * hh + 1]
            m_new = jnp.maximum(m, jnp.max(jnp.maximum(sa, sb), axis=0, keepdims=True))
            alpha = jnp.exp2(m - m_new) * (1.0 - fresh)
            new.append((m_new, accs[hh], alpha, jnp.exp2(sa - m_new).astype(BF16),
                        jnp.exp2(sb - m_new).astype(BF16)))

        @pl.when((tab_ref[2 * n_items + prev] == 1) & (t > 0))
        def _():
            write_block(qi_prev, accs)

        return tuple(new)

    no_p = jnp.zeros((MOBA_BLOCK, MOBA_BLOCK), BF16)
    init = (jnp.full((1, MOBA_BLOCK), NEG, F32), jnp.zeros((hd + ONES_ROWS, MOBA_BLOCK), F32),
            jnp.zeros((1, MOBA_BLOCK), F32), no_p, no_p)
    state = lax.fori_loop(0, n_items, body, (init, init))
    write_block(nb - 1, weighted_values(nb - 1, (nb - 1) // 2, state))


def moba_prompt(h3, n_heads, hd, page):
    bsz, s_len, _ = h3.shape
    width = n_heads * hd
    lanes = 2 * hd
    pairs = n_heads // 2
    q0 = 3 * width // lanes
    nb = s_len // MOBA_BLOCK
    pages_shape = jax.ShapeDtypeStruct((bsz, s_len // page, n_heads, hd, page), F32)
    page_spec = pl.BlockSpec((1, s_len // page, 2, hd, page), lambda b, p, tab: (b, 0, p, 0, 0))
    items = jnp.asarray(sum(_moba_work_items(nb), []), jnp.int32)
    grid_spec = pltpu.PrefetchScalarGridSpec(
        num_scalar_prefetch=1,
        grid=(bsz, pairs),
        in_specs=[pl.BlockSpec((1, s_len, lanes), lambda b, p, tab: (b, 0, q0 + p)),
                  pl.BlockSpec((1, s_len, lanes), lambda b, p, tab: (b, 0, q0 + pairs + p)),
                  pl.BlockSpec((1, s_len, lanes), lambda b, p, tab: (b, 0, q0 + 2 * pairs + p))],
        out_specs=[pl.BlockSpec((1, s_len, lanes), lambda b, p, tab: (b, 0, p)),
                   page_spec, page_spec],
        scratch_shapes=[pltpu.VMEM((nb, 2, hd + ONES_ROWS, MOBA_BLOCK), BF16),
                        pltpu.VMEM((s_len, lanes), BF16),
                        pltpu.VMEM((s_len, lanes), BF16),
                        pltpu.VMEM((2, nb, nb, MOBA_BLOCK), jnp.int32)],
    )
    return pl.pallas_call(
        _moba_prompt_kernel,
        grid_spec=grid_spec,
        out_shape=[jax.ShapeDtypeStruct((bsz, s_len, width), F32), pages_shape, pages_shape],
        compiler_params=_params("parallel", "parallel"),
        name="moba_prompt",
    )(items, h3, h3, h3)


def _conv3_mix(gb, gcu, prev2, prev1, li, cw):
    s1 = jnp.where(li == 0, prev1[0], pltpu.roll(gcu, 1, 0))
    s2 = pltpu.roll(gcu, 2, 0)
    s2 = jnp.where(li == 0, prev2[0], jnp.where(li == 1, prev2[1], s2))
    conv = cw[0:1] * s2 + cw[1:2] * s1 + cw[2:3] * gcu
    return gb * conv


def _out_proj_residual(ya, yb, x, wo_ref, pw):
    half = ya.shape[1]
    y = _dot(ya.astype(BF16), wo_ref[0:half, :]) + _dot(yb.astype(BF16), wo_ref[half:, :])
    return x + _rms(y, pw)


def _cache_block_means(seq, n_pages, pt_ref, ck_ref, km_ref, buf, sem):
    _, n_h, hd, _ = km_ref.shape
    page = ck_ref.shape[3]
    ppc = buf.shape[0] // 2
    pages_per_block = MOBA_BLOCK // page
    lane3 = lax.broadcasted_iota(jnp.int32, (n_h, hd, 128), 2)

    def pages_of(c):
        return [p for p in range(c * ppc, (c + 1) * ppc) if p < n_pages]

    def copy(p):
        half = (p // ppc) % 2
        return pltpu.make_async_copy(ck_ref.at[pt_ref[seq * n_pages + p]],
                                     buf.at[half * ppc + p % ppc], sem.at[half])

    def start(c):
        for p in pages_of(c):
            copy(p).start()

    def run(c, n_chunks):
        if c + 1 < n_chunks:
            start(c + 1)
        pages = pages_of(c)
        for p in pages:
            copy(p).wait()
        half = (c % 2) * ppc
        blocks = len(pages) // pages_per_block
        if blocks == 0:
            return
        sums = []
        for u in range(blocks):
            acc = buf[half + u * pages_per_block]
            for jj in range(1, pages_per_block):
                acc = acc + buf[half + u * pages_per_block + jj]
            sums.append(acc)
        means = jnp.sum(jnp.concatenate(sums, axis=0), axis=-1, keepdims=True) * (1.0 / MOBA_BLOCK)
        km = km_ref[0]
        first = c * ppc // pages_per_block
        for u in range(blocks):
            km = jnp.where(lane3 == first + u, means[u * n_h:(u + 1) * n_h], km)
        km_ref[0] = km

    return start, run


def _l0_post_prompt_kernel(pt_ref, gb_ref, gc_ref, u_ref, gch_ref, uh_ref, yb_ref, x_ref, cw_ref,
                           wo_ref, pw_ref, pre_ref, wg_ref, wu_ref, wd_ref, post_ref, ck_ref,
                           xo_ref, st_ref, km_ref, buf, sem, *, tiles_per_seq, n_pages):
    i = pl.program_id(0)
    tm = gb_ref.shape[0]
    start, run = _cache_block_means(i, n_pages, pt_ref, ck_ref, km_ref, buf, sem)
    km_ref[...] = jnp.zeros(km_ref.shape, F32)
    start(0)

    gcu = gc_ref[...] * u_ref[...]
    halo = gch_ref[...] * uh_ref[...]
    halo = halo * (i % tiles_per_seq != 0).astype(F32)
    li = lax.broadcasted_iota(jnp.int32, gcu.shape, 0)
    ya = _conv3_mix(gb_ref[...], gcu, (halo[6:7], halo[7:8]), (halo[7:8],), li, cw_ref[...])
    x_mid = _out_proj_residual(ya, yb_ref[...], x_ref[...], wo_ref, pw_ref[...])
    xo_ref[...] = _ffn_apply(x_mid, pre_ref, wg_ref, wu_ref, wd_ref, post_ref, side_task=run)

    @pl.when(i % tiles_per_seq == tiles_per_seq - 1)
    def _():
        st_ref[0] = gcu[tm - 2:tm, :]


def l0_post_prompt(h2, yb, x, conv_w, wo, post_w, ffn_w, seq_len, page_table, cache_kt):
    m, d = x.shape
    width = yb.shape[1]
    tm = ROW_TILE
    tiles_per_seq = seq_len // tm
    bd, n_pages = page_table.shape
    _, n_h, hd, page = cache_kt.shape
    n_chunks = ffn_w[1].shape[1] // FFN_CHUNK
    ppb = MOBA_BLOCK // page
    pages_per_chunk = -(-n_pages // (n_chunks * ppb)) * ppb
    assert m // tm == bd and n_pages // ppb <= 128
    halo_idx = lambda i: jnp.maximum(i * (tm // 8) - 1, 0)
    const = lambda i, pt: (0, 0)
    ffn_args, ffn_specs = _ffn_operands(ffn_w, d, const)
    grid_spec = pltpu.PrefetchScalarGridSpec(
        num_scalar_prefetch=1,
        grid=(m // tm,),
        in_specs=[pl.BlockSpec((tm, width), lambda i, pt: (i, 0)),
                  pl.BlockSpec((tm, width), lambda i, pt: (i, 1)),
                  pl.BlockSpec((tm, width), lambda i, pt: (i, 2)),
                  pl.BlockSpec((8, width), lambda i, pt: (halo_idx(i), 1)),
                  pl.BlockSpec((8, width), lambda i, pt: (halo_idx(i), 2)),
                  pl.BlockSpec((tm, width), lambda i, pt: (i, 0)),
                  pl.BlockSpec((tm, d), lambda i, pt: (i, 0)),
                  pl.BlockSpec(conv_w.shape, const),
                  pl.BlockSpec(wo.shape, const, pipeline_mode=pl.Buffered(1)),
                  pl.BlockSpec((1, d), const)] + ffn_specs + [pl.BlockSpec(memory_space=pl.ANY)],
        out_specs=[pl.BlockSpec((tm, d), lambda i, pt: (i, 0)),
                   pl.BlockSpec((1, 2, width), lambda i, pt: (i // tiles_per_seq, 0, 0)),
                   pl.BlockSpec((1, n_h, hd, 128), lambda i, pt: (i, 0, 0, 0))],
        scratch_shapes=[pltpu.VMEM((2 * pages_per_chunk, n_h, hd, page), F32),
                        pltpu.SemaphoreType.DMA((2,))],
    )
    return pl.pallas_call(
        functools.partial(_l0_post_prompt_kernel, tiles_per_seq=tiles_per_seq, n_pages=n_pages),
        grid_spec=grid_spec,
        out_shape=[jax.ShapeDtypeStruct((m, d), F32),
                   jax.ShapeDtypeStruct((m // seq_len, 2, width), F32),
                   jax.ShapeDtypeStruct((bd, n_h, hd, 128), F32)],
        compiler_params=_params("arbitrary"),
        name="l0_post_prompt",
    )(page_table.reshape(-1), h2, h2, h2, h2, h2, yb, x, conv_w, wo, post_w.reshape(1, d), *ffn_args,
      cache_kt)


def _l0_post_sample_kernel(gb_ref, gc_ref, u_ref, hist_ref, yb_ref, x_ref, cw_ref, wo_ref, pw_ref,
                           xo_ref, gcu_ref, *, seq_len):
    gcu = gc_ref[...] * u_ref[...]
    rows = gcu.shape[0]
    t = lax.broadcasted_iota(jnp.int32, gcu.shape, 0) % seq_len
    hist = hist_ref[...]
    hist_next = pltpu.roll(hist, rows - 1, 0)
    ya = _conv3_mix(gb_ref[...], gcu, (hist, hist), (hist_next,), t, cw_ref[...])
    xo_ref[...] = _out_proj_residual(ya, yb_ref[...], x_ref[...], wo_ref, pw_ref[...])
    gcu_ref[...] = gcu


def l0_post_sample(h2, yb, x, hist_rows, conv_w, wo, post_w, seq_len):
    m, d = x.shape
    width = yb.shape[1]
    const = lambda i: (0, 0)
    return pl.pallas_call(
        functools.partial(_l0_post_sample_kernel, seq_len=seq_len),
        grid=(1,),
        in_specs=[pl.BlockSpec((m, width), lambda i: (0, 0)),
                  pl.BlockSpec((m, width), lambda i: (0, 1)),
                  pl.BlockSpec((m, width), lambda i: (0, 2)),
                  pl.BlockSpec((m, width), const),
                  pl.BlockSpec((m, width), const),
                  pl.BlockSpec((m, d), const),
                  pl.BlockSpec(conv_w.shape, const),
                  pl.BlockSpec(wo.shape, const),
                  pl.BlockSpec((1, d), const)],
        out_specs=[pl.BlockSpec((m, d), const), pl.BlockSpec((m, width), const)],
        out_shape=[jax.ShapeDtypeStruct((m, d), F32), jax.ShapeDtypeStruct((m, width), F32)],
        compiler_params=_params("arbitrary"),
        name="l0_post_sample",
    )(h2, h2, h2, hist_rows, yb, x, conv_w, wo, post_w.reshape(1, d))


ATTEND_SLOTS = 4


def _moba_select_kernel(km_ref, qt_ref, sel_ref, *, n_blk):
    _, n_h, hd, t_len = qt_ref.shape
    lane = lax.broadcasted_iota(jnp.int32, (n_h, 128), 1)
    sub = lax.broadcasted_iota(jnp.int32, (n_h, 128), 0)
    out = jnp.zeros((n_h, 128), jnp.int32)
    for t in range(t_len):
        gate = jnp.full((n_h, 128), -jnp.inf, F32)
        for h in range(n_h):
            g_h = jnp.sum(km_ref[0, h] * qt_ref[0, h][:, t:t + 1], axis=0, keepdims=True)
            gate = jnp.where(sub == h, g_h, gate)
        gate = jnp.where(lane < n_blk, gate, -jnp.inf)
        for r in range(MOBA_TOPK):
            best = jnp.max(gate, axis=-1, keepdims=True)
            idx = jnp.min(jnp.where(gate == best, lane, 128), axis=-1, keepdims=True)
            out = jnp.where(lane == t * MOBA_TOPK + r, idx, out)
            gate = jnp.where(lane == idx, -jnp.inf, gate)
    sel_ref[0] = out


def moba_select(block_means, qt4, n_blk):
    bd, n_h, hd, t_len = qt4.shape
    assert n_blk <= 128 and t_len * MOBA_TOPK <= 128
    return pl.pallas_call(
        functools.partial(_moba_select_kernel, n_blk=n_blk),
        grid=(bd,),
        in_specs=[pl.BlockSpec((1, n_h, hd, 128), lambda b: (b, 0, 0, 0)),
                  pl.BlockSpec((1, n_h, hd, t_len), lambda b: (b, 0, 0, 0))],
        out_specs=pl.BlockSpec((1, n_h, 128), lambda b: (b, 0, 0)),
        out_shape=jax.ShapeDtypeStruct((bd, n_h, 128), jnp.int32),
        compiler_params=_params("parallel"),
        name="moba_select",
    )(block_means, qt4)


def _moba_attend_kernel(pt_ref, sel_ref, qt_ref, knt_ref, vnt_ref, ck_ref, cv_ref, o_ref,
                        kbuf, vbuf, sem, s_ref, *, pages_per_block, n_pages):
    bd, n_h, hd, t_len = qt_ref.shape
    page = ck_ref.shape[3]
    per_t = MOBA_TOPK * pages_per_block
    scale = hd ** -0.5
    step = lax.broadcasted_iota(jnp.int32, (1, t_len), 1)
    step_col = lax.broadcasted_iota(jnp.int32, (hd, t_len), 1)

    def copies(pair, slot):
        b = pair // n_h
        h = pair % n_h
        out = []
        for t in range(t_len):
            for r in range(MOBA_TOPK):
                blk = sel_ref[(b * n_h + h) * (t_len * MOBA_TOPK) + t * MOBA_TOPK + r]
                for jj in range(pages_per_block):
                    phys = pt_ref[b * n_pages + blk * pages_per_block + jj]
                    s = t * per_t + r * pages_per_block + jj
                    out.append(pltpu.make_async_copy(ck_ref.at[phys, h], kbuf.at[slot, s],
                                                     sem.at[slot]))
                    out.append(pltpu.make_async_copy(cv_ref.at[phys, h], vbuf.at[slot, s],
                                                     sem.at[slot]))
        return out

    for ahead in range(ATTEND_SLOTS - 1):
        for c in copies(ahead, ahead):
            c.start()

    def per_pair(pair, _):
        slot = pair % ATTEND_SLOTS
        b = pair // n_h
        h = pair % n_h
        nxt = pair + ATTEND_SLOTS - 1

        @pl.when(nxt < bd * n_h)
        def _():
            for c in copies(nxt, nxt % ATTEND_SLOTS):
                c.start()

        for c in copies(pair, slot):
            c.wait()

        q_all = qt_ref[b, h] * scale
        k_new = knt_ref[b, h]
        v_new = vnt_ref[b, h]
        for t in range(t_len):
            qc = q_all[:, t:t + 1]
            for s in range(per_t):
                s_ref[t * 8 + s:t * 8 + s + 1, :] = jnp.sum(kbuf[slot, t * per_t + s] * qc, axis=0,
                                                             keepdims=True)
        out = jnp.zeros((hd, t_len), F32)
        for t in range(t_len):
            qc = q_all[:, t:t + 1]
            s_sel = s_ref[t * 8:(t + 1) * 8, :]
            s_own = jnp.where(step <= t, jnp.sum(k_new * qc, axis=0, keepdims=True), NEG)
            m = jnp.maximum(jnp.max(s_own, axis=-1, keepdims=True),
                            jnp.max(jnp.max(s_sel, axis=-1, keepdims=True), axis=0, keepdims=True))
            p_own = jnp.exp(s_own - m)
            p_sel = jnp.exp(s_sel - m)
            l = (jnp.sum(p_own, axis=-1, keepdims=True)
                 + jnp.sum(jnp.sum(p_sel, axis=-1, keepdims=True), axis=0, keepdims=True))
            acc = vbuf[slot, t * per_t] * p_sel[0:1, :]
            for s in range(1, per_t):
                acc = acc + vbuf[slot, t * per_t + s] * p_sel[s:s + 1, :]
            o_t = (jnp.sum(v_new * p_own, axis=-1, keepdims=True)
                   + jnp.sum(acc, axis=-1, keepdims=True)) / l
            out = jnp.where(step_col == t, o_t, out)
        o_ref[b, h] = out
        return 0

    s_ref[...] = jnp.full(s_ref.shape, NEG, F32)
    lax.fori_loop(0, bd * n_h, per_pair, 0)


def moba_attend(page_table, sel, qt4, knt4, vnt4, cache_kt, cache_vt):
    bd, n_pages = page_table.shape
    _, n_h, hd, t_len = qt4.shape
    page = cache_kt.shape[3]
    ppb = MOBA_BLOCK // page
    n_slab = t_len * MOBA_TOPK * ppb
    full = pl.BlockSpec(qt4.shape, lambda i, pt, sl: (0, 0, 0, 0))
    grid_spec = pltpu.PrefetchScalarGridSpec(
        num_scalar_prefetch=2,
        grid=(1,),
        in_specs=[full, full, full, pl.BlockSpec(memory_space=pl.ANY),
                  pl.BlockSpec(memory_space=pl.ANY)],
        out_specs=full,
        scratch_shapes=[pltpu.VMEM((ATTEND_SLOTS, n_slab, hd, page), F32),
                        pltpu.VMEM((ATTEND_SLOTS, n_slab, hd, page), F32),
                        pltpu.SemaphoreType.DMA((ATTEND_SLOTS,)),
                        pltpu.VMEM((t_len * 8, page), F32)],
    )
    return pl.pallas_call(
        functools.partial(_moba_attend_kernel, pages_per_block=ppb, n_pages=n_pages),
        grid_spec=grid_spec,
        out_shape=jax.ShapeDtypeStruct(qt4.shape, F32),
        compiler_params=_params("arbitrary"),
        name="moba_attend",
    )(page_table.reshape(-1), sel.reshape(-1), qt4, knt4, vnt4, cache_kt, cache_vt)


def _rows8(x):
    return x.reshape(x.shape[0] // 8, 8, x.shape[1])


def _softplus(x):
    return jnp.maximum(x, 0.0) + jnp.log1p(jnp.exp(-jnp.abs(x)))


def _ssd_kernel(*refs, d_inner, rows_valid, has_halo):
    if has_halo:
        (zx_ref, halo_ref, dt_ref, s0_ref, cw_ref, cb_ref, dtb_ref, alog_ref, dskip_ref, e_ref,
         y_ref, cs_ref, fs_ref, st_ref, xin_ref, dtin_ref, y_scr) = refs
        hist_ref = None
    else:
        (zx_ref, hist_ref, dt_ref, s0_ref, cw_ref, cb_ref, dtb_ref, alog_ref, dskip_ref, e_ref,
         y_ref, cs_ref, fs_ref, st_ref, xin_ref, dtin_ref, y_scr) = refs
        halo_ref = None
    c = pl.program_id(1)
    n_chunks = pl.num_programs(1)
    cl = SSM_CHUNK
    rows_blk = zx_ref.shape[1]
    conv_dim = cw_ref.shape[2]
    n_st = SSM_STATE
    hp = SSM_HEAD_DIM
    heads_per_group = d_inner // hp // SSM_GROUPS
    gw = heads_per_group * hp

    @pl.when(c == 0)
    def _():
        st_ref[...] = s0_ref[0].T

    if has_halo:
        xin_ref[0:8, :] = halo_ref[0, :, d_inner:d_inner + conv_dim] * (c != 0).astype(F32)
    else:
        xin_ref[0:8, :] = jnp.zeros((8, conv_dim), F32)
        xin_ref[5:8, :] = hist_ref[0]
    if rows_blk == cl:
        xin_ref[8:8 + cl, :] = zx_ref[0, :, d_inner:d_inner + conv_dim]
        dt_raw = dt_ref[0]
    else:
        xin_ref[8:16, :] = jnp.zeros((8, conv_dim), F32)
        dtin_ref[...] = jnp.zeros(dtin_ref.shape, F32)
        xin_ref[8:8 + rows_blk, :] = zx_ref[0, :, d_inner:d_inner + conv_dim]
        dtin_ref[0:rows_blk, :] = dt_ref[0]
        dt_raw = dtin_ref[...]

    conv_rows = cl if rows_blk == cl else 8
    assert rows_blk <= conv_rows
    xbc = xin_ref[8:8 + conv_rows, :]
    taps = [_rows8(xin_ref[5 + j:5 + j + conv_rows, :]) * cw_ref[j][None] for j in range(3)]
    conv = taps[0] + taps[1] + taps[2] + _rows8(xbc) * cw_ref[3][None] + cb_ref[...][None]
    conv = conv.reshape(conv_rows, conv_dim)
    act = conv * jax.nn.sigmoid(conv)
    if conv_rows < cl:
        act = jnp.concatenate([act, jnp.zeros((cl - conv_rows, conv_dim), F32)], axis=0)

    @pl.when(c == n_chunks - 1)
    def _():
        cs_ref[0] = xbc[rows_valid - 3:rows_valid, :]

    dt = _softplus(dt_raw + dtb_ref[...])
    if rows_valid < cl:
        dt = jnp.where(lax.broadcasted_iota(jnp.int32, dt.shape, 0) < rows_valid, dt, 0.0)
    a = -jnp.exp(alog_ref[...])
    dta = dt * a
    r_i = lax.broadcasted_iota(jnp.int32, (cl, cl), 0)
    c_i = lax.broadcasted_iota(jnp.int32, (cl, cl), 1)
    causal = c_i <= r_i
    first_head_lanes = lax.broadcasted_iota(jnp.int32, (cl, 2 * hp), 1) < hp
    tril = jnp.where(causal, 1.0, 0.0)
    acum = jnp.dot(tril, dta, preferred_element_type=F32, precision=HIGHEST)
    acum_t = acum.T
    alast = acum[cl - 1:cl, :]
    narrow = jnp.concatenate([dt, jnp.exp(acum), jnp.exp(alast - acum),
                              jnp.broadcast_to(jnp.exp(alast), (8, alast.shape[1]))], axis=0)
    hi = narrow.astype(BF16)
    mid = (narrow - hi.astype(F32)).astype(BF16)
    wide = _dot(jnp.concatenate([hi, mid], axis=1), e_ref[...])
    dt_w = wide[0:cl]
    eac_w = wide[cl:2 * cl]
    dte_w = wide[2 * cl:3 * cl]
    cd_w = wide[3 * cl:3 * cl + 8]

    xs = act[:, 0:d_inner]
    xdt = xs * dt_w
    xdt_b = xdt.astype(BF16)
    xdtd_b = (xdt * dte_w).astype(BF16)
    for g in range(SSM_GROUPS):
        b_g = act[:, d_inner + g * n_st:d_inner + (g + 1) * n_st]
        c_g = act[:, d_inner + (SSM_GROUPS + g) * n_st:d_inner + (SSM_GROUPS + g + 1) * n_st]
        c_gb = c_g.astype(BF16)
        cb = _dot_nt(c_gb, b_g.astype(BF16))
        s_g = st_ref[:, g * gw:(g + 1) * gw]
        y_off = _dot(c_gb, s_g.astype(BF16)) * eac_w[:, g * gw:(g + 1) * gw]
        for r in range(0, heads_per_group, 2):
            h = g * heads_per_group + r
            stacked = []
            for hh in (h, h + 1):
                seg = acum[:, hh:hh + 1] - acum_t[hh:hh + 1, :]
                decay = jnp.exp(jnp.where(causal, seg, NEG))
                stacked.append((cb * decay).astype(BF16))
            both = _dot(jnp.concatenate(stacked, axis=0), xdt_b[:, h * hp:(h + 2) * hp])
            y_d = jnp.where(first_head_lanes, both[0:cl], both[cl:2 * cl])
            y_scr[:, h * hp:(h + 2) * hp] = y_d + y_off[:, r * hp:(r + 2) * hp]
        kept = (_rows8(s_g) * cd_w[:, g * gw:(g + 1) * gw][None]).reshape(n_st, gw)
        st_ref[:, g * gw:(g + 1) * gw] = kept + _dot(b_g.T.astype(BF16),
                                                     xdtd_b[:, g * gw:(g + 1) * gw])
    y = y_scr[...] + (_rows8(xs) * dskip_ref[...][None]).reshape(cl, d_inner)
    y_ref[0] = y[0:rows_blk, :]

    @pl.when(c == n_chunks - 1)
    def _():
        fs_ref[0] = st_ref[...].T


def ssd_mixer(zx3, dt3, hist, state0, conv_w, conv_b, dt_bias, a_log, d_skip, d_inner):
    nseq, t_len, _ = zx3.shape
    conv_dim = conv_w.shape[1]
    n_heads = d_inner // SSM_HEAD_DIM
    cl = SSM_CHUNK
    has_halo = hist is None
    if has_halo:
        rows_blk, rows_valid, n_chunks = cl, cl, t_len // cl
    else:
        rows_blk, rows_valid, n_chunks = t_len, t_len, 1
    pad = lambda v: jnp.pad(v.astype(F32), (0, 128 - n_heads)).reshape(1, 128)
    expand = (jnp.arange(256)[:, None] % 128
              == (jnp.arange(d_inner)[None, :] // SSM_HEAD_DIM)).astype(BF16)
    rows8 = lambda v: jnp.broadcast_to(v[..., None, :], v.shape[:-1] + (8, v.shape[-1]))
    d_wide = rows8(jnp.repeat(d_skip.astype(F32), SSM_HEAD_DIM))
    const2 = lambda s, c: (0, 0)
    if has_halo:
        second = pl.BlockSpec((1, 8, zx3.shape[2]),
                              lambda s, c: (s, jnp.maximum(c * (cl // 8) - 1, 0), 0))
        second_arg = zx3
    else:
        second = pl.BlockSpec((1,) + hist.shape[1:], lambda s, c: (s, 0, 0))
        second_arg = hist
    outs = pl.pallas_call(
        functools.partial(_ssd_kernel, d_inner=d_inner, rows_valid=rows_valid, has_halo=has_halo),
        grid=(nseq, n_chunks),
        in_specs=[pl.BlockSpec((1, rows_blk, zx3.shape[2]), lambda s, c: (s, c, 0)),
                  second,
                  pl.BlockSpec((1, rows_blk, 128), lambda s, c: (s, c, 0)),
                  pl.BlockSpec((1, d_inner, SSM_STATE), lambda s, c: (s, 0, 0)),
                  pl.BlockSpec((conv_w.shape[0], 8, conv_dim), lambda s, c: (0, 0, 0)),
                  pl.BlockSpec((8, conv_dim), const2),
                  pl.BlockSpec((1, 128), const2),
                  pl.BlockSpec((1, 128), const2),
                  pl.BlockSpec((8, d_inner), const2),
                  pl.BlockSpec((256, d_inner), const2)],
        out_specs=[pl.BlockSpec((1, rows_blk, d_inner), lambda s, c: (s, c, 0)),
                   pl.BlockSpec((1, 3, conv_dim), lambda s, c: (s, 0, 0)),
                   pl.BlockSpec((1, d_inner, SSM_STATE), lambda s, c: (s, 0, 0))],
        out_shape=[jax.ShapeDtypeStruct((nseq, t_len, d_inner), F32),
                   jax.ShapeDtypeStruct((nseq, 3, conv_dim), F32),
                   jax.ShapeDtypeStruct((nseq, d_inner, SSM_STATE), F32)],
        scratch_shapes=[pltpu.VMEM((SSM_STATE, d_inner), F32),
                        pltpu.VMEM((cl + 8, conv_dim), F32),
                        pltpu.VMEM((cl, 128), F32),
                        pltpu.VMEM((cl, d_inner), F32)],
        compiler_params=_params("parallel", "arbitrary"),
        name="ssd_mixer",
    )(zx3, second_arg, dt3, state0, rows8(conv_w), rows8(conv_b), pad(dt_bias), pad(a_log),
      d_wide, expand)
    return outs


L1_POST_ROW_TILE = 512


def _l1_post_kernel(y_ref, z_ref, x_ref, gw_ref, wo_ref, pw_ref, pre_ref, wg_ref, wu_ref, wd_ref,
                    post_ref, o_ref):
    z = z_ref[...]
    g = y_ref[...] * (z * jax.nn.sigmoid(z))
    o = _dot(_rms(g, gw_ref[...]).astype(BF16), wo_ref[...])
    x_mid = x_ref[...] + _rms(o, pw_ref[...])
    o_ref[...] = _ffn_apply(x_mid, pre_ref, wg_ref, wu_ref, wd_ref, post_ref)


def l1_post(y, zx, x, gate_w, wo, post_w, ffn_w):
    m, d = x.shape
    d_inner = y.shape[1]
    tm = _row_tile(m, L1_POST_ROW_TILE)
    const = lambda i: (0, 0)
    ffn_args, ffn_specs = _ffn_operands(ffn_w, d)
    return pl.pallas_call(
        _l1_post_kernel,
        grid=(m // tm,),
        in_specs=[pl.BlockSpec((tm, d_inner), lambda i: (i, 0)),
                  pl.BlockSpec((tm, d_inner), lambda i: (i, 0)),
                  pl.BlockSpec((tm, d), lambda i: (i, 0)),
                  pl.BlockSpec((1, d_inner), const),
                  pl.BlockSpec(wo.shape, const, pipeline_mode=pl.Buffered(1)),
                  pl.BlockSpec((1, d), const)] + ffn_specs,
        out_specs=pl.BlockSpec((tm, d), lambda i: (i, 0)),
        out_shape=jax.ShapeDtypeStruct((m, d), F32),
        compiler_params=_params("parallel"),
        name="l1_post",
    )(y, zx, x, gate_w.reshape(1, d_inner), wo, post_w.reshape(1, d), *ffn_args)


def kernel(x_prompt, x_sample, cache_k, cache_v, page_table, state_conv_a, state_conv_ssm, state_ssm, l0_norm_mix_pre, l0_w_in, l0_conv_w, l0_w_out, l0_norm_mix_post, l0_norm_ffn_pre, l0_ffn_gate, l0_ffn_up, l0_ffn_down, l0_norm_ffn_post, l1_norm_mix_pre, l1_w_in, l1_conv_w, l1_conv_b, l1_dt_bias, l1_a_log, l1_d_skip, l1_norm_gate, l1_w_out, l1_norm_mix_post, l1_norm_ffn_pre, l1_ffn_gate, l1_ffn_up, l1_ffn_down, l1_norm_ffn_post):
    bp, s_len, d = x_prompt.shape
    bd, t_len, _ = x_sample.shape
    n_heads, page, hd = cache_k.shape[1:]
    width = n_heads * hd
    assert (page_table.shape[1] * page) % MOBA_BLOCK == 0
    conv_dim = l1_conv_w.shape[1]
    ssm_heads = l1_dt_bias.shape[0]
    d_inner = ssm_heads * SSM_HEAD_DIM

    bf = lambda w: w.astype(BF16)
    xp = x_prompt.reshape(bp * s_len, d)
    xs = x_sample.reshape(bd * t_len, d)

    w_in0 = bf(l0_w_in)
    w_out0 = bf(l0_w_out)
    hp_ = norm_matmul(xp, l0_norm_mix_pre, w_in0)
    hs_ = norm_matmul(xs, l0_norm_mix_pre, w_in0)

    yb_p, kp_t, vp_t = moba_prompt(hp_.reshape(bp, s_len, -1), n_heads, hd, page)
    k_prompt = jnp.swapaxes(kp_t, 3, 4)
    v_prompt = jnp.swapaxes(vp_t, 3, 4)
    cache_kt = jnp.swapaxes(cache_k, 2, 3)
    cache_vt = jnp.swapaxes(cache_v, 2, 3)
    ffn0 = (l0_norm_ffn_pre, bf(l0_ffn_gate), bf(l0_ffn_up), bf(l0_ffn_down), l0_norm_ffn_post)
    xp, conv_a_prompt, block_means = l0_post_prompt(
        hp_, yb_p.reshape(bp * s_len, width), xp, l0_conv_w, w_out0, l0_norm_mix_post, ffn0, s_len,
        page_table, cache_kt)

    qkv_t = hs_[:, 3 * width:].reshape(bd, t_len, 3, n_heads, hd).transpose(2, 0, 3, 4, 1)
    qt4, knt4, vnt4 = qkv_t[0], qkv_t[1], qkv_t[2]
    n_blk = page_table.shape[1] * page // MOBA_BLOCK
    sel = moba_select(block_means, qt4, n_blk)[:, :, :t_len * MOBA_TOPK]
    yb_s = moba_attend(page_table, sel, qt4, knt4, vnt4, cache_kt, cache_vt)
    yb_s = yb_s.transpose(0, 3, 1, 2).reshape(bd * t_len, width)
    hist_rows = jnp.pad(state_conv_a, ((0, 0), (0, t_len - state_conv_a.shape[1]), (0, 0)))
    xs, gcu_s = l0_post_sample(hs_, yb_s, xs, hist_rows.reshape(bd * t_len, width), l0_conv_w,
                               w_out0, l0_norm_mix_post, t_len)
    conv_a_sample = gcu_s.reshape(bd, t_len, width)[:, t_len - 2:]
    k_sample = knt4.transpose(0, 1, 3, 2)
    v_sample = vnt4.transpose(0, 1, 3, 2)

    xs = ffn(xs, ffn0)

    w_zx = bf(l1_w_in[:, :d_inner + conv_dim])
    w_dt = jnp.pad(l1_w_in[:, d_inner + conv_dim:], ((0, 0), (0, 128 - ssm_heads)))
    w_out1 = bf(l1_w_out)
    zx_p, dt_p = norm_matmul(xp, l1_norm_mix_pre, w_zx, w_dt)
    zx_s, dt_s = norm_matmul(xs, l1_norm_mix_pre, w_zx, w_dt)

    zeros_state = jnp.zeros((bp, d_inner, SSM_STATE), F32)
    y_p, conv_ssm_prompt, fs_p = ssd_mixer(zx_p.reshape(bp, s_len, -1), dt_p.reshape(bp, s_len, 128),
                                           None, zeros_state, l1_conv_w, l1_conv_b, l1_dt_bias,
                                           l1_a_log, l1_d_skip, d_inner)
    y_s, conv_ssm_sample, fs_s = ssd_mixer(zx_s.reshape(bd, t_len, -1), dt_s.reshape(bd, t_len, 128),
                                           state_conv_ssm, state_ssm.reshape(bd, d_inner, SSM_STATE),
                                           l1_conv_w, l1_conv_b, l1_dt_bias, l1_a_log, l1_d_skip,
                                           d_inner)
    ssm_prompt = fs_p.reshape(bp, ssm_heads, SSM_HEAD_DIM, SSM_STATE)
    ssm_sample = fs_s.reshape(bd, ssm_heads, SSM_HEAD_DIM, SSM_STATE)

    ffn1 = (l1_norm_ffn_pre, bf(l1_ffn_gate), bf(l1_ffn_up), bf(l1_ffn_down), l1_norm_ffn_post)
    xp = l1_post(y_p.reshape(bp * s_len, d_inner), zx_p, xp, l1_norm_gate, w_out1, l1_norm_mix_post,
                 ffn1)
    xs = l1_post(y_s.reshape(bd * t_len, d_inner), zx_s, xs, l1_norm_gate, w_out1, l1_norm_mix_post,
                 ffn1)

    return (xp.reshape(bp, s_len, d), xs.reshape(bd, t_len, d), k_prompt, v_prompt, k_sample,
            v_sample, conv_a_prompt, conv_a_sample, conv_ssm_prompt, conv_ssm_sample, ssm_prompt,
            ssm_sample)
```

```python
import functools

import jax
import jax.numpy as jnp
from jax import lax
from jax.experimental import pallas as pl
from jax.experimental.pallas import tpu as pltpu

F32 = jnp.float32
BF16 = jnp.bfloat16
HIGHEST = lax.Precision.HIGHEST

NORM_EPS = 1e-6
MOBA_BLOCK = 256
MOBA_TOPK = 3
SSM_CHUNK = 128
SSM_HEAD_DIM = 64
SSM_STATE = 128
SSM_GROUPS = 4
NEG = -1e30
LOG2E = 1.4426950408889634
ONES_ROWS = 16

VMEM_LIMIT_BYTES = 56 * 1024 * 1024
ROW_TILE = 512


def _params(*sem):
    return pltpu.CompilerParams(dimension_semantics=sem, vmem_limit_bytes=VMEM_LIMIT_BYTES)


def _rms(x, w):
    return x * lax.rsqrt(jnp.mean(x * x, axis=-1, keepdims=True) + NORM_EPS) * w


def _dot(a, b):
    return jnp.dot(a, b, preferred_element_type=F32)


def _dot_nt(a, b, precision=None):
    return lax.dot_general(a, b, (((1,), (1,)), ((), ())), preferred_element_type=F32,
                           precision=precision)


def _row_tile(m, tile=ROW_TILE):
    return tile if m % tile == 0 else m


PROJ_ROW_TILE = 512
PROJ_COL_CHUNK = 1024


def _norm_matmul_kernel(*refs, narrow):
    if narrow:
        x_ref, nw_ref, w_ref, whi_ref, wlo_ref, o_ref, o2_ref = refs
    else:
        x_ref, nw_ref, w_ref, o_ref = refs
    xn = _rms(x_ref[...], nw_ref[...])
    xh = xn.astype(BF16)
    n = w_ref.shape[1]
    for c0 in range(0, n, PROJ_COL_CHUNK):
        c1 = min(c0 + PROJ_COL_CHUNK, n)
        o_ref[:, c0:c1] = _dot(xh, w_ref[:, c0:c1])
    if narrow:
        xl = (xn - xh.astype(F32)).astype(BF16)
        o2_ref[...] = _dot(xh, whi_ref[...]) + _dot(xl, whi_ref[...]) + _dot(xh, wlo_ref[...])


def norm_matmul(x, nw, w_bf16, w_narrow=None):
    m, d = x.shape
    n = w_bf16.shape[1]
    tm = PROJ_ROW_TILE if m % PROJ_ROW_TILE == 0 else m
    const = lambda i: (0, 0)
    resident = lambda shape: pl.BlockSpec(shape, const, pipeline_mode=pl.Buffered(1))
    in_specs = [pl.BlockSpec((tm, d), lambda i: (i, 0)), pl.BlockSpec((1, d), const),
                resident((d, n))]
    out_specs = [pl.BlockSpec((tm, n), lambda i: (i, 0))]
    out_shape = [jax.ShapeDtypeStruct((m, n), F32)]
    args = [x, nw.reshape(1, d), w_bf16]
    if w_narrow is not None:
        n2 = w_narrow.shape[1]
        whi = w_narrow.astype(BF16)
        wlo = (w_narrow - whi.astype(F32)).astype(BF16)
        in_specs += [resident((d, n2)), resident((d, n2))]
        out_specs.append(pl.BlockSpec((tm, n2), lambda i: (i, 0)))
        out_shape.append(jax.ShapeDtypeStruct((m, n2), F32))
        args += [whi, wlo]
    outs = pl.pallas_call(
        functools.partial(_norm_matmul_kernel, narrow=w_narrow is not None),
        grid=(m // tm,),
        in_specs=in_specs,
        out_specs=out_specs,
        out_shape=out_shape,
        compiler_params=_params("parallel"),
        name="norm_matmul",
    )(*args)
    return outs if w_narrow is not None else outs[0]


FFN_CHUNK = 256
FFN_ROW_TILE = 1024


def _ffn_apply(x, pre_ref, wg_ref, wu_ref, wd_ref, post_ref, side_task=None):
    h = _rms(x, pre_ref[...]).astype(BF16)
    n_chunks = wg_ref.shape[1] // FFN_CHUNK
    acc = jnp.zeros(x.shape, F32)
    for c in range(n_chunks):
        sl = slice(c * FFN_CHUNK, (c + 1) * FFN_CHUNK)
        g = _dot(h, wg_ref[:, sl])
        u = _dot(h, wu_ref[:, sl])
        a = (g * jax.nn.sigmoid(g) * u).astype(BF16)
        acc = acc + _dot(a, wd_ref[sl, :])
        if side_task is not None:
            side_task(c, n_chunks)
    return x + _rms(acc, post_ref[...])


def _ffn_kernel(x_ref, pre_ref, wg_ref, wu_ref, wd_ref, post_ref, o_ref):
    o_ref[...] = _ffn_apply(x_ref[...], pre_ref, wg_ref, wu_ref, wd_ref, post_ref)


def _ffn_operands(ffn_w, d, const=lambda i: (0, 0)):
    pre_w, wg, wu, wd, post_w = ffn_w
    resident = lambda w: pl.BlockSpec(w.shape, const, pipeline_mode=pl.Buffered(1))
    args = [pre_w.reshape(1, d), wg, wu, wd, post_w.reshape(1, d)]
    specs = [pl.BlockSpec((1, d), const), resident(wg), resident(wu), resident(wd),
             pl.BlockSpec((1, d), const)]
    return args, specs


def ffn(x, ffn_w):
    m, d = x.shape
    tm = _row_tile(m, FFN_ROW_TILE)
    ffn_args, ffn_specs = _ffn_operands(ffn_w, d)
    return pl.pallas_call(
        _ffn_kernel,
        grid=(m // tm,),
        in_specs=[pl.BlockSpec((tm, d), lambda i: (i, 0))] + ffn_specs,
        out_specs=pl.BlockSpec((tm, d), lambda i: (i, 0)),
        out_shape=jax.ShapeDtypeStruct((m, d), F32),
        compiler_params=_params("parallel"),
        name="ffn",
    )(x, *ffn_args)


def _moba_work_items(nb):
    items = [(qi, i, int(i == qi // 2)) for qi in range(nb) for i in range(qi // 2 + 1)]
    return [list(col) for col in zip(*items)]


def _moba_prompt_kernel(tab_ref, q_ref, k_ref, v_ref, o_ref, kp_ref, vp_ref, vt_ref, kb_ref,
                        qb_ref, sel_ref):
    s_len = k_ref.shape[1]
    nb = s_len // MOBA_BLOCK
    n_items = tab_ref.shape[0] // 3
    npages = kp_ref.shape[1]
    hd = kp_ref.shape[3]
    page = kp_ref.shape[4]
    ppb = MOBA_BLOCK // page
    scale = hd ** -0.5

    kb_ref[...] = k_ref[0].astype(BF16)
    qb_ref[...] = (q_ref[0] * (scale * LOG2E)).astype(BF16)
    km = jnp.mean(k_ref[0].reshape(nb, MOBA_BLOCK, 2 * hd), axis=1)
    blk = lax.broadcasted_iota(jnp.int32, (nb, s_len), 0)
    own = lax.broadcasted_iota(jnp.int32, (nb, s_len), 1) // MOBA_BLOCK
    for hh in range(2):
        lo = hh * hd
        gate = _dot_nt(km[:, lo:lo + hd], q_ref[0, :, lo:lo + hd], precision=HIGHEST)
        cnt = jnp.zeros((nb, s_len), F32)
        for jp in range(nb - 1):
            gj = gate[jp:jp + 1, :]
            beats = ((gj > gate) | ((gj == gate) & (jp < blk))) & (jp < own)
            cnt = cnt + jnp.where(beats, 1.0, 0.0)
        picked = ((cnt < MOBA_TOPK) & (blk < own)) | (blk == own)
        limit_shift = jnp.where(picked, 0, -4 * MOBA_BLOCK)
        for qb in range(nb):
            sel_ref[hh, qb] = limit_shift[:, qb * MOBA_BLOCK:(qb + 1) * MOBA_BLOCK]
    for pg in range(npages):
        rows = slice(pg * page, (pg + 1) * page)
        kt = k_ref[0, rows, :].T
        vt = v_ref[0, rows, :].T
        kp_ref[0, pg] = kt.reshape(2, hd, page)
        vp_ref[0, pg] = vt.reshape(2, hd, page)
        cols = slice((pg % ppb) * page, (pg % ppb + 1) * page)
        for hh in range(2):
            vt_ref[pg // ppb, hh, 0:hd, cols] = vt[hh * hd:(hh + 1) * hd].astype(BF16)
    ones_row = jnp.where(lax.broadcasted_iota(jnp.int32, (ONES_ROWS, MOBA_BLOCK), 0) == 0, 1.0, 0.0)
    for j in range(nb):
        for hh in range(2):
            vt_ref[j, hh, hd:hd + ONES_ROWS, :] = ones_row.astype(BF16)

    row = lax.broadcasted_iota(jnp.int32, (MOBA_BLOCK, MOBA_BLOCK), 0)
    col_row = lax.broadcasted_iota(jnp.int32, (1, MOBA_BLOCK), 1)

    def scores(qi, j, hh):
        jc = jnp.minimum(j, qi)
        q_tile = qb_ref[pl.ds(pl.multiple_of(qi * MOBA_BLOCK, MOBA_BLOCK), MOBA_BLOCK),
                        hh * hd:(hh + 1) * hd]
        k_tile = kb_ref[pl.ds(pl.multiple_of(jc * MOBA_BLOCK, MOBA_BLOCK), MOBA_BLOCK),
                        hh * hd:(hh + 1) * hd]
        shift = jnp.where(j < qi, MOBA_BLOCK, jnp.where(j == qi, 0, -4 * MOBA_BLOCK))
        limit = col_row + (1 + shift) + sel_ref[hh, qi, pl.ds(jc, 1), :]
        return jnp.where(row < limit, _dot_nt(k_tile, q_tile), -jnp.inf)

    def weighted_values(qi, i, state):
        ja = jnp.minimum(2 * i, qi)
        jb = jnp.minimum(2 * i + 1, qi)
        return [alpha * acc + _dot(vt_ref[ja, hh], pa) + _dot(vt_ref[jb, hh], pb)
                for hh, (_, acc, alpha, pa, pb) in enumerate(state)]

    def write_block(qi, accs):
        o_ref[0, pl.ds(pl.multiple_of(qi * MOBA_BLOCK, MOBA_BLOCK), MOBA_BLOCK), :] = (
            jnp.concatenate([acc[0:hd] / acc[hd:hd + 1] for acc in accs], axis=0).T)

    def body(t, state):
        qi, i = tab_ref[t], tab_ref[n_items + t]
        tiles = [scores(qi, 2 * i + b, hh) for hh in range(2) for b in range(2)]
        prev = jnp.maximum(t - 1, 0)
        qi_prev = tab_ref[prev]
        accs = weighted_values(qi_prev, tab_ref[n_items + prev], state)
        fresh = (i == 0).astype(F32)
        new = []
        for hh in range(2):
            m = state[hh][0]
            m = m + fresh * (NEG - m)
            sa, sb = tiles[2 * hh], tiles[2 * hh + 1]
            m_new = jnp.maximum(m, jnp.max(jnp.maximum(sa, sb), axis=0, keepdims=True))
            alpha = jnp.exp2(m - m_new) * (1.0 - fresh)
            new.append((m_new, accs[hh], alpha, jnp.exp2(sa - m_new).astype(BF16),
                        jnp.exp2(sb - m_new).astype(BF16)))

        @pl.when((tab_ref[2 * n_items + prev] == 1) & (t > 0))
        def _():
            write_block(qi_prev, accs)

        return tuple(new)

    no_p = jnp.zeros((MOBA_BLOCK, MOBA_BLOCK), BF16)
    init = (jnp.full((1, MOBA_BLOCK), NEG, F32), jnp.zeros((hd + ONES_ROWS, MOBA_BLOCK), F32),
            jnp.zeros((1, MOBA_BLOCK), F32), no_p, no_p)
    state = lax.fori_loop(0, n_items, body, (init, init))
    write_block(nb - 1, weighted_values(nb - 1, (nb - 1) // 2, state))


def moba_prompt(h3, n_heads, hd, page):
    bsz, s_len, _ = h3.shape
    width = n_heads * hd
    lanes = 2 * hd
    pairs = n_heads // 2
    q0 = 3 * width // lanes
    nb = s_len // MOBA_BLOCK
    pages_shape = jax.ShapeDtypeStruct((bsz, s_len // page, n_heads, hd, page), F32)
    page_spec = pl.BlockSpec((1, s_len // page, 2, hd, page), lambda b, p, tab: (b, 0, p, 0, 0))
    items = jnp.asarray(sum(_moba_work_items(nb), []), jnp.int32)
    grid_spec = pltpu.PrefetchScalarGridSpec(
        num_scalar_prefetch=1,
        grid=(bsz, pairs),
        in_specs=[pl.BlockSpec((1, s_len, lanes), lambda b, p, tab: (b, 0, q0 + p)),
                  pl.BlockSpec((1, s_len, lanes), lambda b, p, tab: (b, 0, q0 + pairs + p)),
                  pl.BlockSpec((1, s_len, lanes), lambda b, p, tab: (b, 0, q0 + 2 * pairs + p))],
        out_specs=[pl.BlockSpec((1, s_len, lanes), lambda b, p, tab: (b, 0, p)),
                   page_spec, page_spec],
        scratch_shapes=[pltpu.VMEM((nb, 2, hd + ONES_ROWS, MOBA_BLOCK), BF16),
                        pltpu.VMEM((s_len, lanes), BF16),
                        pltpu.VMEM((s_len, lanes), BF16),
                        pltpu.VMEM((2, nb, nb, MOBA_BLOCK), jnp.int32)],
    )
    return pl.pallas_call(
        _moba_prompt_kernel,
        grid_spec=grid_spec,
        out_shape=[jax.ShapeDtypeStruct((bsz, s_len, width), F32), pages_shape, pages_shape],
        compiler_params=_params("parallel", "parallel"),
        name="moba_prompt",
    )(items, h3, h3, h3)


def _conv3_mix(gb, gcu, prev2, prev1, li, cw):
    s1 = jnp.where(li == 0, prev1[0], pltpu.roll(gcu, 1, 0))
    s2 = pltpu.roll(gcu, 2, 0)
    s2 = jnp.where(li == 0, prev2[0], jnp.where(li == 1, prev2[1], s2))
    conv = cw[0:1] * s2 + cw[1:2] * s1 + cw[2:3] * gcu
    return gb * conv


def _out_proj_residual(ya, yb, x, wo_ref, pw):
    half = ya.shape[1]
    y = _dot(ya.astype(BF16), wo_ref[0:half, :]) + _dot(yb.astype(BF16), wo_ref[half:, :])
    return x + _rms(y, pw)


def _cache_block_means(seq, n_pages, pt_ref, ck_ref, km_ref, buf, sem):
    _, n_h, hd, _ = km_ref.shape
    page = ck_ref.shape[3]
    ppc = buf.shape[0] // 2
    pages_per_block = MOBA_BLOCK // page
    lane3 = lax.broadcasted_iota(jnp.int32, (n_h, hd, 128), 2)

    def pages_of(c):
        return [p for p in range(c * ppc, (c + 1) * ppc) if p < n_pages]

    def copy(p):
        half = (p // ppc) % 2
        return pltpu.make_async_copy(ck_ref.at[pt_ref[seq * n_pages + p]],
                                     buf.at[half * ppc + p % ppc], sem.at[half])

    def start(c):
        for p in pages_of(c):
            copy(p).start(priority=p % 2)

    def run(c, n_chunks):
        if c + 1 < n_chunks:
            start(c + 1)
        pages = pages_of(c)
        for p in pages:
            copy(p).wait()
        half = (c % 2) * ppc
        blocks = len(pages) // pages_per_block
        if blocks == 0:
            return
        sums = []
        for u in range(blocks):
            acc = buf[half + u * pages_per_block]
            for jj in range(1, pages_per_block):
                acc = acc + buf[half + u * pages_per_block + jj]
            sums.append(acc)
        means = jnp.sum(jnp.concatenate(sums, axis=0), axis=-1, keepdims=True) * (1.0 / MOBA_BLOCK)
        km = km_ref[0]
        first = c * ppc // pages_per_block
        for u in range(blocks):
            km = jnp.where(lane3 == first + u, means[u * n_h:(u + 1) * n_h], km)
        km_ref[0] = km

    return start, run


def _l0_post_prompt_kernel(pt_ref, gb_ref, gc_ref, u_ref, gch_ref, uh_ref, yb_ref, x_ref, cw_ref,
                           wo_ref, pw_ref, pre_ref, wg_ref, wu_ref, wd_ref, post_ref, ck_ref,
                           xo_ref, st_ref, km_ref, buf, sem, *, tiles_per_seq, n_pages):
    i = pl.program_id(0)
    tm = gb_ref.shape[0]
    start, run = _cache_block_means(i, n_pages, pt_ref, ck_ref, km_ref, buf, sem)
    km_ref[...] = jnp.zeros(km_ref.shape, F32)
    start(0)

    gcu = gc_ref[...] * u_ref[...]
    halo = gch_ref[...] * uh_ref[...]
    halo = halo * (i % tiles_per_seq != 0).astype(F32)
    li = lax.broadcasted_iota(jnp.int32, gcu.shape, 0)
    ya = _conv3_mix(gb_ref[...], gcu, (halo[6:7], halo[7:8]), (halo[7:8],), li, cw_ref[...])
    x_mid = _out_proj_residual(ya, yb_ref[...], x_ref[...], wo_ref, pw_ref[...])
    xo_ref[...] = _ffn_apply(x_mid, pre_ref, wg_ref, wu_ref, wd_ref, post_ref, side_task=run)

    @pl.when(i % tiles_per_seq == tiles_per_seq - 1)
    def _():
        st_ref[0] = gcu[tm - 2:tm, :]


def l0_post_prompt(h2, yb, x, conv_w, wo, post_w, ffn_w, seq_len, page_table, cache_kt):
    m, d = x.shape
    width = yb.shape[1]
    tm = ROW_TILE
    tiles_per_seq = seq_len // tm
    bd, n_pages = page_table.shape
    _, n_h, hd, page = cache_kt.shape
    n_chunks = ffn_w[1].shape[1] // FFN_CHUNK
    ppb = MOBA_BLOCK // page
    pages_per_chunk = -(-n_pages // (n_chunks * ppb)) * ppb
    assert m // tm == bd and n_pages // ppb <= 128
    halo_idx = lambda i: jnp.maximum(i * (tm // 8) - 1, 0)
    const = lambda i, pt: (0, 0)
    ffn_args, ffn_specs = _ffn_operands(ffn_w, d, const)
    grid_spec = pltpu.PrefetchScalarGridSpec(
        num_scalar_prefetch=1,
        grid=(m // tm,),
        in_specs=[pl.BlockSpec((tm, width), lambda i, pt: (i, 0)),
                  pl.BlockSpec((tm, width), lambda i, pt: (i, 1)),
                  pl.BlockSpec((tm, width), lambda i, pt: (i, 2)),
                  pl.BlockSpec((8, width), lambda i, pt: (halo_idx(i), 1)),
                  pl.BlockSpec((8, width), lambda i, pt: (halo_idx(i), 2)),
                  pl.BlockSpec((tm, width), lambda i, pt: (i, 0)),
                  pl.BlockSpec((tm, d), lambda i, pt: (i, 0)),
                  pl.BlockSpec(conv_w.shape, const),
                  pl.BlockSpec(wo.shape, const, pipeline_mode=pl.Buffered(1)),
                  pl.BlockSpec((1, d), const)] + ffn_specs + [pl.BlockSpec(memory_space=pl.ANY)],
        out_specs=[pl.BlockSpec((tm, d), lambda i, pt: (i, 0)),
                   pl.BlockSpec((1, 2, width), lambda i, pt: (i // tiles_per_seq, 0, 0)),
                   pl.BlockSpec((1, n_h, hd, 128), lambda i, pt: (i, 0, 0, 0))],
        scratch_shapes=[pltpu.VMEM((2 * pages_per_chunk, n_h, hd, page), F32),
                        pltpu.SemaphoreType.DMA((2,))],
    )
    return pl.pallas_call(
        functools.partial(_l0_post_prompt_kernel, tiles_per_seq=tiles_per_seq, n_pages=n_pages),
        grid_spec=grid_spec,
        out_shape=[jax.ShapeDtypeStruct((m, d), F32),
                   jax.ShapeDtypeStruct((m // seq_len, 2, width), F32),
                   jax.ShapeDtypeStruct((bd, n_h, hd, 128), F32)],
        compiler_params=_params("arbitrary"),
        name="l0_post_prompt",
    )(page_table.reshape(-1), h2, h2, h2, h2, h2, yb, x, conv_w, wo, post_w.reshape(1, d), *ffn_args,
      cache_kt)


def _l0_post_sample_kernel(gb_ref, gc_ref, u_ref, hist_ref, yb_ref, x_ref, cw_ref, wo_ref, pw_ref,
                           xo_ref, gcu_ref, *, seq_len):
    gcu = gc_ref[...] * u_ref[...]
    rows = gcu.shape[0]
    t = lax.broadcasted_iota(jnp.int32, gcu.shape, 0) % seq_len
    hist = hist_ref[...]
    hist_next = pltpu.roll(hist, rows - 1, 0)
    ya = _conv3_mix(gb_ref[...], gcu, (hist, hist), (hist_next,), t, cw_ref[...])
    xo_ref[...] = _out_proj_residual(ya, yb_ref[...], x_ref[...], wo_ref, pw_ref[...])
    gcu_ref[...] = gcu


def l0_post_sample(h2, yb, x, hist_rows, conv_w, wo, post_w, seq_len):
    m, d = x.shape
    width = yb.shape[1]
    const = lambda i: (0, 0)
    return pl.pallas_call(
        functools.partial(_l0_post_sample_kernel, seq_len=seq_len),
        grid=(1,),
        in_specs=[pl.BlockSpec((m, width), lambda i: (0, 0)),
                  pl.BlockSpec((m, width), lambda i: (0, 1)),
                  pl.BlockSpec((m, width), lambda i: (0, 2)),
                  pl.BlockSpec((m, width), const),
                  pl.BlockSpec((m, width), const),
                  pl.BlockSpec((m, d), const),
                  pl.BlockSpec(conv_w.shape, const),
                  pl.BlockSpec(wo.shape, const),
                  pl.BlockSpec((1, d), const)],
        out_specs=[pl.BlockSpec((m, d), const), pl.BlockSpec((m, width), const)],
        out_shape=[jax.ShapeDtypeStruct((m, d), F32), jax.ShapeDtypeStruct((m, width), F32)],
        compiler_params=_params("arbitrary"),
        name="l0_post_sample",
    )(h2, h2, h2, hist_rows, yb, x, conv_w, wo, post_w.reshape(1, d))


ATTEND_SLOTS = 4


def _moba_select_kernel(km_ref, qt_ref, sel_ref, *, n_blk):
    _, n_h, hd, t_len = qt_ref.shape
    lane = lax.broadcasted_iota(jnp.int32, (n_h, 128), 1)
    sub = lax.broadcasted_iota(jnp.int32, (n_h, 128), 0)
    out = jnp.zeros((n_h, 128), jnp.int32)
    for t in range(t_len):
        gate = jnp.full((n_h, 128), -jnp.inf, F32)
        for h in range(n_h):
            g_h = jnp.sum(km_ref[0, h] * qt_ref[0, h][:, t:t + 1], axis=0, keepdims=True)
            gate = jnp.where(sub == h, g_h, gate)
        gate = jnp.where(lane < n_blk, gate, -jnp.inf)
        for r in range(MOBA_TOPK):
            best = jnp.max(gate, axis=-1, keepdims=True)
            idx = jnp.min(jnp.where(gate == best, lane, 128), axis=-1, keepdims=True)
            out = jnp.where(lane == t * MOBA_TOPK + r, idx, out)
            gate = jnp.where(lane == idx, -jnp.inf, gate)
    sel_ref[0] = out


def moba_select(block_means, qt4, n_blk):
    bd, n_h, hd, t_len = qt4.shape
    assert n_blk <= 128 and t_len * MOBA_TOPK <= 128
    return pl.pallas_call(
        functools.partial(_moba_select_kernel, n_blk=n_blk),
        grid=(bd,),
        in_specs=[pl.BlockSpec((1, n_h, hd, 128), lambda b: (b, 0, 0, 0)),
                  pl.BlockSpec((1, n_h, hd, t_len), lambda b: (b, 0, 0, 0))],
        out_specs=pl.BlockSpec((1, n_h, 128), lambda b: (b, 0, 0)),
        out_shape=jax.ShapeDtypeStruct((bd, n_h, 128), jnp.int32),
        compiler_params=_params("parallel"),
        name="moba_select",
    )(block_means, qt4)


def _moba_attend_kernel(pt_ref, sel_ref, qt_ref, knt_ref, vnt_ref, ck_ref, cv_ref, o_ref,
                        kbuf, vbuf, sem, s_ref, *, pages_per_block, n_pages):
    bd, n_h, hd, t_len = qt_ref.shape
    page = ck_ref.shape[3]
    per_t = MOBA_TOPK * pages_per_block
    scale = hd ** -0.5
    step = lax.broadcasted_iota(jnp.int32, (1, t_len), 1)
    step_col = lax.broadcasted_iota(jnp.int32, (hd, t_len), 1)

    def copies(pair, slot):
        b = pair // n_h
        h = pair % n_h
        out = []
        for t in range(t_len):
            for r in range(MOBA_TOPK):
                blk = sel_ref[(b * n_h + h) * (t_len * MOBA_TOPK) + t * MOBA_TOPK + r]
                for jj in range(pages_per_block):
                    phys = pt_ref[b * n_pages + blk * pages_per_block + jj]
                    s = t * per_t + r * pages_per_block + jj
                    out.append(pltpu.make_async_copy(ck_ref.at[phys, h], kbuf.at[slot, s],
                                                     sem.at[slot]))
                    out.append(pltpu.make_async_copy(cv_ref.at[phys, h], vbuf.at[slot, s],
                                                     sem.at[slot]))
        return out

    for ahead in range(ATTEND_SLOTS - 1):
        for c in copies(ahead, ahead):
            c.start()

    def per_pair(pair, _):
        slot = pair % ATTEND_SLOTS
        b = pair // n_h
        h = pair % n_h
        nxt = pair + ATTEND_SLOTS - 1

        @pl.when(nxt < bd * n_h)
        def _():
            for c in copies(nxt, nxt % ATTEND_SLOTS):
                c.start()

        for c in copies(pair, slot):
            c.wait()

        q_all = qt_ref[b, h] * scale
        k_new = knt_ref[b, h]
        v_new = vnt_ref[b, h]
        for t in range(t_len):
            qc = q_all[:, t:t + 1]
            for s in range(per_t):
                s_ref[t * 8 + s:t * 8 + s + 1, :] = jnp.sum(kbuf[slot, t * per_t + s] * qc, axis=0,
                                                             keepdims=True)
        out = jnp.zeros((hd, t_len), F32)
        for t in range(t_len):
            qc = q_all[:, t:t + 1]
            s_sel = s_ref[t * 8:(t + 1) * 8, :]
            s_own = jnp.where(step <= t, jnp.sum(k_new * qc, axis=0, keepdims=True), NEG)
            m = jnp.maximum(jnp.max(s_own, axis=-1, keepdims=True),
                            jnp.max(jnp.max(s_sel, axis=-1, keepdims=True), axis=0, keepdims=True))
            p_own = jnp.exp(s_own - m)
            p_sel = jnp.exp(s_sel - m)
            l = (jnp.sum(p_own, axis=-1, keepdims=True)
                 + jnp.sum(jnp.sum(p_sel, axis=-1, keepdims=True), axis=0, keepdims=True))
            acc = vbuf[slot, t * per_t] * p_sel[0:1, :]
            for s in range(1, per_t):
                acc = acc + vbuf[slot, t * per_t + s] * p_sel[s:s + 1, :]
            o_t = (jnp.sum(v_new * p_own, axis=-1, keepdims=True)
                   + jnp.sum(acc, axis=-1, keepdims=True)) / l
            out = jnp.where(step_col == t, o_t, out)
        o_ref[b, h] = out
        return 0

    s_ref[...] = jnp.full(s_ref.shape, NEG, F32)
    lax.fori_loop(0, bd * n_h, per_pair, 0)


def moba_attend(page_table, sel, qt4, knt4, vnt4, cache_kt, cache_vt):
    bd, n_pages = page_table.shape
    _, n_h, hd, t_len = qt4.shape
    page = cache_kt.shape[3]
    ppb = MOBA_BLOCK // page
    n_slab = t_len * MOBA_TOPK * ppb
    full = pl.BlockSpec(qt4.shape, lambda i, pt, sl: (0, 0, 0, 0))
    grid_spec = pltpu.PrefetchScalarGridSpec(
        num_scalar_prefetch=2,
        grid=(1,),
        in_specs=[full, full, full, pl.BlockSpec(memory_space=pl.ANY),
                  pl.BlockSpec(memory_space=pl.ANY)],
        out_specs=full,
        scratch_shapes=[pltpu.VMEM((ATTEND_SLOTS, n_slab, hd, page), F32),
                        pltpu.VMEM((ATTEND_SLOTS, n_slab, hd, page), F32),
                        pltpu.SemaphoreType.DMA((ATTEND_SLOTS,)),
                        pltpu.VMEM((t_len * 8, page), F32)],
    )
    return pl.pallas_call(
        functools.partial(_moba_attend_kernel, pages_per_block=ppb, n_pages=n_pages),
        grid_spec=grid_spec,
        out_shape=jax.ShapeDtypeStruct(qt4.shape, F32),
        compiler_params=_params("arbitrary"),
        name="moba_attend",
    )(page_table.reshape(-1), sel.reshape(-1), qt4, knt4, vnt4, cache_kt, cache_vt)


def _rows8(x):
    return x.reshape(x.shape[0] // 8, 8, x.shape[1])


def _softplus(x):
    return jnp.maximum(x, 0.0) + jnp.log1p(jnp.exp(-jnp.abs(x)))


def _ssd_kernel(*refs, d_inner, rows_valid, has_halo):
    if has_halo:
        (zx_ref, halo_ref, dt_ref, s0_ref, cw_ref, cb_ref, dtb_ref, alog_ref, dskip_ref, e_ref,
         y_ref, cs_ref, fs_ref, st_ref, xin_ref, dtin_ref, y_scr) = refs
        hist_ref = None
    else:
        (zx_ref, hist_ref, dt_ref, s0_ref, cw_ref, cb_ref, dtb_ref, alog_ref, dskip_ref, e_ref,
         y_ref, cs_ref, fs_ref, st_ref, xin_ref, dtin_ref, y_scr) = refs
        halo_ref = None
    c = pl.program_id(1)
    n_chunks = pl.num_programs(1)
    cl = SSM_CHUNK
    rows_blk = zx_ref.shape[1]
    conv_dim = cw_ref.shape[2]
    n_st = SSM_STATE
    hp = SSM_HEAD_DIM
    heads_per_group = d_inner // hp // SSM_GROUPS
    gw = heads_per_group * hp

    @pl.when(c == 0)
    def _():
        st_ref[...] = s0_ref[0].T

    if has_halo:
        xin_ref[0:8, :] = halo_ref[0, :, d_inner:d_inner + conv_dim] * (c != 0).astype(F32)
    else:
        xin_ref[0:8, :] = jnp.zeros((8, conv_dim), F32)
        xin_ref[5:8, :] = hist_ref[0]
    if rows_blk == cl:
        xin_ref[8:8 + cl, :] = zx_ref[0, :, d_inner:d_inner + conv_dim]
        dt_raw = dt_ref[0]
    else:
        xin_ref[8:16, :] = jnp.zeros((8, conv_dim), F32)
        dtin_ref[...] = jnp.zeros(dtin_ref.shape, F32)
        xin_ref[8:8 + rows_blk, :] = zx_ref[0, :, d_inner:d_inner + conv_dim]
        dtin_ref[0:rows_blk, :] = dt_ref[0]
        dt_raw = dtin_ref[...]

    conv_rows = cl if rows_blk == cl else 8
    assert rows_blk <= conv_rows
    xbc = xin_ref[8:8 + conv_rows, :]
    taps = [_rows8(xin_ref[5 + j:5 + j + conv_rows, :]) * cw_ref[j][None] for j in range(3)]
    conv = taps[0] + taps[1] + taps[2] + _rows8(xbc) * cw_ref[3][None] + cb_ref[...][None]
    conv = conv.reshape(conv_rows, conv_dim)
    act = conv * jax.nn.sigmoid(conv)
    if conv_rows < cl:
        act = jnp.concatenate([act, jnp.zeros((cl - conv_rows, conv_dim), F32)], axis=0)

    @pl.when(c == n_chunks - 1)
    def _():
        cs_ref[0] = xbc[rows_valid - 3:rows_valid, :]

    dt = _softplus(dt_raw + dtb_ref[...])
    if rows_valid < cl:
        dt = jnp.where(lax.broadcasted_iota(jnp.int32, dt.shape, 0) < rows_valid, dt, 0.0)
    a = -jnp.exp(alog_ref[...])
    dta = dt * a
    r_i = lax.broadcasted_iota(jnp.int32, (cl, cl), 0)
    c_i = lax.broadcasted_iota(jnp.int32, (cl, cl), 1)
    causal = c_i <= r_i
    first_head_lanes = lax.broadcasted_iota(jnp.int32, (cl, 2 * hp), 1) < hp
    tril = jnp.where(causal, 1.0, 0.0)
    acum = jnp.dot(tril, dta, preferred_element_type=F32, precision=HIGHEST)
    acum_t = acum.T
    alast = acum[cl - 1:cl, :]
    narrow = jnp.concatenate([dt, jnp.exp(acum), jnp.exp(alast - acum),
                              jnp.broadcast_to(jnp.exp(alast), (8, alast.shape[1]))], axis=0)
    hi = narrow.astype(BF16)
    mid = (narrow - hi.astype(F32)).astype(BF16)
    wide = _dot(jnp.concatenate([hi, mid], axis=1), e_ref[...])
    dt_w = wide[0:cl]
    eac_w = wide[cl:2 * cl]
    dte_w = wide[2 * cl:3 * cl]
    cd_w = wide[3 * cl:3 * cl + 8]

    xs = act[:, 0:d_inner]
    xdt = xs * dt_w
    xdt_b = xdt.astype(BF16)
    xdtd_b = (xdt * dte_w).astype(BF16)
    for g in range(SSM_GROUPS):
        b_g = act[:, d_inner + g * n_st:d_inner + (g + 1) * n_st]
        c_g = act[:, d_inner + (SSM_GROUPS + g) * n_st:d_inner + (SSM_GROUPS + g + 1) * n_st]
        c_gb = c_g.astype(BF16)
        cb = _dot_nt(c_gb, b_g.astype(BF16))
        s_g = st_ref[:, g * gw:(g + 1) * gw]
        y_off = _dot(c_gb, s_g.astype(BF16)) * eac_w[:, g * gw:(g + 1) * gw]
        for r in range(0, heads_per_group, 2):
            h = g * heads_per_group + r
            stacked = []
            for hh in (h, h + 1):
                seg = acum[:, hh:hh + 1] - acum_t[hh:hh + 1, :]
                decay = jnp.exp(jnp.where(causal, seg, NEG))
                stacked.append((cb * decay).astype(BF16))
            both = _dot(jnp.concatenate(stacked, axis=0), xdt_b[:, h * hp:(h + 2) * hp])
            y_d = jnp.where(first_head_lanes, both[0:cl], both[cl:2 * cl])
            y_scr[:, h * hp:(h + 2) * hp] = y_d + y_off[:, r * hp:(r + 2) * hp]
        kept = (_rows8(s_g) * cd_w[:, g * gw:(g + 1) * gw][None]).reshape(n_st, gw)
        st_ref[:, g * gw:(g + 1) * gw] = kept + _dot(b_g.T.astype(BF16),
                                                     xdtd_b[:, g * gw:(g + 1) * gw])
    y = y_scr[...] + (_rows8(xs) * dskip_ref[...][None]).reshape(cl, d_inner)
    y_ref[0] = y[0:rows_blk, :]

    @pl.when(c == n_chunks - 1)
    def _():
        fs_ref[0] = st_ref[...].T


def ssd_mixer(zx3, dt3, hist, state0, conv_w, conv_b, dt_bias, a_log, d_skip, d_inner):
    nseq, t_len, _ = zx3.shape
    conv_dim = conv_w.shape[1]
    n_heads = d_inner // SSM_HEAD_DIM
    cl = SSM_CHUNK
    has_halo = hist is None
    if has_halo:
        rows_blk, rows_valid, n_chunks = cl, cl, t_len // cl
    else:
        rows_blk, rows_valid, n_chunks = t_len, t_len, 1
    pad = lambda v: jnp.pad(v.astype(F32), (0, 128 - n_heads)).reshape(1, 128)
    expand = (jnp.arange(256)[:, None] % 128
              == (jnp.arange(d_inner)[None, :] // SSM_HEAD_DIM)).astype(BF16)
    rows8 = lambda v: jnp.broadcast_to(v[..., None, :], v.shape[:-1] + (8, v.shape[-1]))
    d_wide = rows8(jnp.repeat(d_skip.astype(F32), SSM_HEAD_DIM))
    const2 = lambda s, c: (0, 0)
    if has_halo:
        second = pl.BlockSpec((1, 8, zx3.shape[2]),
                              lambda s, c: (s, jnp.maximum(c * (cl // 8) - 1, 0), 0))
        second_arg = zx3
    else:
        second = pl.BlockSpec((1,) + hist.shape[1:], lambda s, c: (s, 0, 0))
        second_arg = hist
    outs = pl.pallas_call(
        functools.partial(_ssd_kernel, d_inner=d_inner, rows_valid=rows_valid, has_halo=has_halo),
        grid=(nseq, n_chunks),
        in_specs=[pl.BlockSpec((1, rows_blk, zx3.shape[2]), lambda s, c: (s, c, 0)),
                  second,
                  pl.BlockSpec((1, rows_blk, 128), lambda s, c: (s, c, 0)),
                  pl.BlockSpec((1, d_inner, SSM_STATE), lambda s, c: (s, 0, 0)),
                  pl.BlockSpec((conv_w.shape[0], 8, conv_dim), lambda s, c: (0, 0, 0)),
                  pl.BlockSpec((8, conv_dim), const2),
                  pl.BlockSpec((1, 128), const2),
                  pl.BlockSpec((1, 128), const2),
                  pl.BlockSpec((8, d_inner), const2),
                  pl.BlockSpec((256, d_inner), const2)],
        out_specs=[pl.BlockSpec((1, rows_blk, d_inner), lambda s, c: (s, c, 0)),
                   pl.BlockSpec((1, 3, conv_dim), lambda s, c: (s, 0, 0)),
                   pl.BlockSpec((1, d_inner, SSM_STATE), lambda s, c: (s, 0, 0))],
        out_shape=[jax.ShapeDtypeStruct((nseq, t_len, d_inner), F32),
                   jax.ShapeDtypeStruct((nseq, 3, conv_dim), F32),
                   jax.ShapeDtypeStruct((nseq, d_inner, SSM_STATE), F32)],
        scratch_shapes=[pltpu.VMEM((SSM_STATE, d_inner), F32),
                        pltpu.VMEM((cl + 8, conv_dim), F32),
                        pltpu.VMEM((cl, 128), F32),
                        pltpu.VMEM((cl, d_inner), F32)],
        compiler_params=_params("parallel", "arbitrary"),
        name="ssd_mixer",
    )(zx3, second_arg, dt3, state0, rows8(conv_w), rows8(conv_b), pad(dt_bias), pad(a_log),
      d_wide, expand)
    return outs


L1_POST_ROW_TILE = 512


def _l1_post_kernel(y_ref, z_ref, x_ref, gw_ref, wo_ref, pw_ref, pre_ref, wg_ref, wu_ref, wd_ref,
                    post_ref, o_ref):
    z = z_ref[...]
    g = y_ref[...] * (z * jax.nn.sigmoid(z))
    o = _dot(_rms(g, gw_ref[...]).astype(BF16), wo_ref[...])
    x_mid = x_ref[...] + _rms(o, pw_ref[...])
    o_ref[...] = _ffn_apply(x_mid, pre_ref, wg_ref, wu_ref, wd_ref, post_ref)


def l1_post(y, zx, x, gate_w, wo, post_w, ffn_w):
    m, d = x.shape
    d_inner = y.shape[1]
    tm = _row_tile(m, L1_POST_ROW_TILE)
    const = lambda i: (0, 0)
    ffn_args, ffn_specs = _ffn_operands(ffn_w, d)
    return pl.pallas_call(
        _l1_post_kernel,
        grid=(m // tm,),
        in_specs=[pl.BlockSpec((tm, d_inner), lambda i: (i, 0)),
                  pl.BlockSpec((tm, d_inner), lambda i: (i, 0)),
                  pl.BlockSpec((tm, d), lambda i: (i, 0)),
                  pl.BlockSpec((1, d_inner), const),
                  pl.BlockSpec(wo.shape, const, pipeline_mode=pl.Buffered(1)),
                  pl.BlockSpec((1, d), const)] + ffn_specs,
        out_specs=pl.BlockSpec((tm, d), lambda i: (i, 0)),
        out_shape=jax.ShapeDtypeStruct((m, d), F32),
        compiler_params=_params("parallel"),
        name="l1_post",
    )(y, zx, x, gate_w.reshape(1, d_inner), wo, post_w.reshape(1, d), *ffn_args)


def kernel(x_prompt, x_sample, cache_k, cache_v, page_table, state_conv_a, state_conv_ssm, state_ssm, l0_norm_mix_pre, l0_w_in, l0_conv_w, l0_w_out, l0_norm_mix_post, l0_norm_ffn_pre, l0_ffn_gate, l0_ffn_up, l0_ffn_down, l0_norm_ffn_post, l1_norm_mix_pre, l1_w_in, l1_conv_w, l1_conv_b, l1_dt_bias, l1_a_log, l1_d_skip, l1_norm_gate, l1_w_out, l1_norm_mix_post, l1_norm_ffn_pre, l1_ffn_gate, l1_ffn_up, l1_ffn_down, l1_norm_ffn_post):
    bp, s_len, d = x_prompt.shape
    bd, t_len, _ = x_sample.shape
    n_heads, page, hd = cache_k.shape[1:]
    width = n_heads * hd
    assert (page_table.shape[1] * page) % MOBA_BLOCK == 0
    conv_dim = l1_conv_w.shape[1]
    ssm_heads = l1_dt_bias.shape[0]
    d_inner = ssm_heads * SSM_HEAD_DIM

    bf = lambda w: w.astype(BF16)
    xp = x_prompt.reshape(bp * s_len, d)
    xs = x_sample.reshape(bd * t_len, d)

    w_in0 = bf(l0_w_in)
    w_out0 = bf(l0_w_out)
    hp_ = norm_matmul(xp, l0_norm_mix_pre, w_in0)
    hs_ = norm_matmul(xs, l0_norm_mix_pre, w_in0)

    yb_p, kp_t, vp_t = moba_prompt(hp_.reshape(bp, s_len, -1), n_heads, hd, page)
    k_prompt = jnp.swapaxes(kp_t, 3, 4)
    v_prompt = jnp.swapaxes(vp_t, 3, 4)
    cache_kt = jnp.swapaxes(cache_k, 2, 3)
    cache_vt = jnp.swapaxes(cache_v, 2, 3)
    ffn0 = (l0_norm_ffn_pre, bf(l0_ffn_gate), bf(l0_ffn_up), bf(l0_ffn_down), l0_norm_ffn_post)
    xp, conv_a_prompt, block_means = l0_post_prompt(
        hp_, yb_p.reshape(bp * s_len, width), xp, l0_conv_w, w_out0, l0_norm_mix_post, ffn0, s_len,
        page_table, cache_kt)

    qkv_t = hs_[:, 3 * width:].reshape(bd, t_len, 3, n_heads, hd).transpose(2, 0, 3, 4, 1)
    qt4, knt4, vnt4 = qkv_t[0], qkv_t[1], qkv_t[2]
    n_blk = page_table.shape[1] * page // MOBA_BLOCK
    sel = moba_select(block_means, qt4, n_blk)[:, :, :t_len * MOBA_TOPK]
    yb_s = moba_attend(page_table, sel, qt4, knt4, vnt4, cache_kt, cache_vt)
    yb_s = yb_s.transpose(0, 3, 1, 2).reshape(bd * t_len, width)
    hist_rows = jnp.pad(state_conv_a, ((0, 0), (0, t_len - state_conv_a.shape[1]), (0, 0)))
    xs, gcu_s = l0_post_sample(hs_, yb_s, xs, hist_rows.reshape(bd * t_len, width), l0_conv_w,
                               w_out0, l0_norm_mix_post, t_len)
    conv_a_sample = gcu_s.reshape(bd, t_len, width)[:, t_len - 2:]
    k_sample = knt4.transpose(0, 1, 3, 2)
    v_sample = vnt4.transpose(0, 1, 3, 2)

    xs = ffn(xs, ffn0)

    w_zx = bf(l1_w_in[:, :d_inner + conv_dim])
    w_dt = jnp.pad(l1_w_in[:, d_inner + conv_dim:], ((0, 0), (0, 128 - ssm_heads)))
    w_out1 = bf(l1_w_out)
    zx_p, dt_p = norm_matmul(xp, l1_norm_mix_pre, w_zx, w_dt)
    zx_s, dt_s = norm_matmul(xs, l1_norm_mix_pre, w_zx, w_dt)

    zeros_state = jnp.zeros((bp, d_inner, SSM_STATE), F32)
    y_p, conv_ssm_prompt, fs_p = ssd_mixer(zx_p.reshape(bp, s_len, -1), dt_p.reshape(bp, s_len, 128),
                                           None, zeros_state, l1_conv_w, l1_conv_b, l1_dt_bias,
                                           l1_a_log, l1_d_skip, d_inner)
    y_s, conv_ssm_sample, fs_s = ssd_mixer(zx_s.reshape(bd, t_len, -1), dt_s.reshape(bd, t_len, 128),
                                           state_conv_ssm, state_ssm.reshape(bd, d_inner, SSM_STATE),
                                           l1_conv_w, l1_conv_b, l1_dt_bias, l1_a_log, l1_d_skip,
                                           d_inner)
    ssm_prompt = fs_p.reshape(bp, ssm_heads, SSM_HEAD_DIM, SSM_STATE)
    ssm_sample = fs_s.reshape(bd, ssm_heads, SSM_HEAD_DIM, SSM_STATE)

    ffn1 = (l1_norm_ffn_pre, bf(l1_ffn_gate), bf(l1_ffn_up), bf(l1_ffn_down), l1_norm_ffn_post)
    xp = l1_post(y_p.reshape(bp * s_len, d_inner), zx_p, xp, l1_norm_gate, w_out1, l1_norm_mix_post,
                 ffn1)
    xs = l1_post(y_s.reshape(bd * t_len, d_inner), zx_s, xs, l1_norm_gate, w_out1, l1_norm_mix_post,
                 ffn1)

    return (xp.reshape(bp, s_len, d), xs.reshape(bd, t_len, d), k_prompt, v_prompt, k_sample,
            v_sample, conv_a_prompt, conv_a_sample, conv_ssm_prompt, conv_ssm_sample, ssm_prompt,
            ssm_sample)
```
